```python
import math
import jax, jax.numpy as jnp
from jax import lax
import numpy as np

D_MODEL = 1024
BATCH = 8
SEQ = 4096
DEPTH = 2

A_HEADS = 4
A_HEAD_DIM = D_MODEL // (2 * A_HEADS)
A_DIM = A_HEADS * A_HEAD_DIM
SHORT_CONV = 4
B_CHANNELS = D_MODEL // 2
B_CONV = 31
AB_IN = 4 * A_DIM + 2 * A_HEADS + 2 * B_CHANNELS
AB_OUT = A_DIM + B_CHANNELS
C_HEADS = 4
C_V_HEAD = D_MODEL // C_HEADS
C_QK_HEAD = C_V_HEAD // 2
C_QK = C_HEADS * C_QK_HEAD
C_V = C_HEADS * C_V_HEAD
C_IN = 2 * C_QK + 2 * C_V + 2 * C_HEADS
GATE_CAP = 15.0
CHUNK = 64
D_FF = 7 * D_MODEL // 2
N_EXPERTS = 8
TOP_K = 2
PLE_DIM = 256
N_EVEN = (DEPTH + 1) // 2
N_ODD = DEPTH // 2
DN_ALPHA = (2 * DEPTH) ** 0.25
DN_BETA = (8 * DEPTH) ** -0.25

kernel_name = "hybrid_deltanet_conformer_mlstm_moe_deepnorm"


def layer_norm(x, g, b, eps=1e-5):
    xf = x.astype(jnp.float32)
    mu = jnp.mean(xf, -1, keepdims=True)
    var = jnp.mean(jnp.square(xf - mu), -1, keepdims=True)
    return ((xf - mu) * lax.rsqrt(var + eps) * g + b).astype(x.dtype)


def rms_norm(x, g, eps=1e-6):
    xf = x.astype(jnp.float32)
    return (xf * lax.rsqrt(jnp.mean(jnp.square(xf), -1, keepdims=True) + eps) * g).astype(x.dtype)


def l2norm(t, eps=1e-6):
    return t * lax.rsqrt(jnp.sum(jnp.square(t), -1, keepdims=True) + eps)


def soft_cap(t):
    return GATE_CAP * jnp.tanh(t / GATE_CAP)


def causal_depthwise_conv(x, w):
    K, C = w.shape
    xp = jnp.pad(x, ((0, 0), (K - 1, 0), (0, 0)))
    return lax.conv_general_dilated(xp, w.astype(x.dtype)[:, None, :], window_strides=(1,),
                                    padding='VALID', dimension_numbers=('NWC', 'WIO', 'NWC'),
                                    feature_group_count=C)


def gated_delta_rule(q, k, v, beta, g):
    Bsz, H, S, dk = q.shape
    dv = v.shape[-1]
    n = S // CHUNK
    q = q.reshape(Bsz, H, n, CHUNK, dk)
    k = k.reshape(Bsz, H, n, CHUNK, dk)
    v = v.reshape(Bsz, H, n, CHUNK, dv)
    beta = beta.reshape(Bsz, H, n, CHUNK)
    gc = jnp.cumsum(g.reshape(Bsz, H, n, CHUNK), axis=-1)
    causal = jnp.tril(jnp.ones((CHUNK, CHUNK), dtype=bool))
    strict = jnp.tril(jnp.ones((CHUNK, CHUNK), dtype=bool), -1)
    decay = jnp.exp(jnp.where(causal, gc[..., :, None] - gc[..., None, :], -jnp.inf))
    k_beta = k * beta[..., None]
    lower = jnp.where(strict, jnp.einsum('bhnid,bhnjd->bhnij', k_beta, k) * decay, 0.0)
    tmat = jnp.eye(CHUNK, dtype=q.dtype) + lower
    u = lax.linalg.triangular_solve(tmat, v * beta[..., None], left_side=True, lower=True,
                                    unit_diagonal=True)
    w = lax.linalg.triangular_solve(tmat, k_beta * jnp.exp(gc)[..., None], left_side=True,
                                    lower=True, unit_diagonal=True)
    attn = jnp.einsum('bhnid,bhnjd->bhnij', q, k) * decay
    q_dec = q * jnp.exp(gc)[..., None]
    k_dec = k * jnp.exp(gc[..., -1:] - gc)[..., None]
    chunk_decay = jnp.exp(gc[..., -1])

    def step(state, xs):
        u_c, w_c, q_c, k_c, a_c, d_c = xs
        v_new = u_c - jnp.einsum('bhlk,bhkv->bhlv', w_c, state)
        o_c = jnp.einsum('bhlk,bhkv->bhlv', q_c, state) + jnp.einsum('bhls,bhsv->bhlv', a_c, v_new)
        state = d_c[..., None, None] * state + jnp.einsum('bhlk,bhlv->bhkv', k_c, v_new)
        return state, o_c

    s0 = jnp.zeros((Bsz, H, dk, dv), q.dtype)
    xs = tuple(jnp.moveaxis(t, 2, 0) for t in (u, w, q_dec, k_dec, attn, chunk_decay))
    _, o = lax.scan(step, s0, xs)
    return jnp.moveaxis(o, 0, 2).reshape(Bsz, H, S, dv)


def mlstm_chunked(q, k, v, log_i, log_f):
    Bsz, H, S, dk = q.shape
    dv = v.shape[-1]
    n = S // CHUNK
    q = q.reshape(Bsz, H, n, CHUNK, dk)
    k = k.reshape(Bsz, H, n, CHUNK, dk)
    v = v.reshape(Bsz, H, n, CHUNK, dv)
    log_i = log_i.reshape(Bsz, H, n, CHUNK)
    b = jnp.cumsum(log_f.reshape(Bsz, H, n, CHUNK), axis=-1)
    causal = jnp.tril(jnp.ones((CHUNK, CHUNK), dtype=bool))
    dmat = jnp.where(causal, b[..., :, None] - b[..., None, :] + log_i[..., None, :], -jnp.inf)
    m_intra = jnp.max(dmat, -1)
    pmat = jnp.exp(dmat - m_intra[..., None]) * jnp.einsum('bhnld,bhnsd->bhnls', q, k)
    num_intra = jnp.einsum('bhnls,bhnsv->bhnlv', pmat, v)
    den_intra = jnp.sum(pmat, -1)
    g_kv = b[..., -1:] - b + log_i
    m_kv = jnp.max(g_kv, -1)
    b_last = b[..., -1]

    def step(carry, xs):
        c_st, n_st, m_st = carry
        q_c, k_c, v_c, b_c, mi_c, num_c, den_c, g_c, mkv_c, bl_c = xs
        inter = b_c + m_st[..., None]
        m_t = jnp.maximum(inter, mi_c)
        s_inter = jnp.exp(inter - m_t)
        s_intra = jnp.exp(mi_c - m_t)
        num = s_inter[..., None] * jnp.einsum('bhlk,bhkv->bhlv', q_c, c_st) + s_intra[..., None] * num_c
        den = s_inter * jnp.einsum('bhlk,bhk->bhl', q_c, n_st) + s_intra * den_c
        h = num / jnp.maximum(jnp.abs(den), jnp.exp(-m_t))[..., None]
        m_new = jnp.maximum(bl_c + m_st, mkv_c)
        kw = k_c * jnp.exp(g_c - m_new[..., None])[..., None]
        dec = jnp.exp(bl_c + m_st - m_new)
        c_st = dec[..., None, None] * c_st + jnp.einsum('bhlk,bhlv->bhkv', kw, v_c)
        n_st = dec[..., None] * n_st + jnp.sum(kw, -2)
        return (c_st, n_st, m_new), h

    init = (jnp.zeros((Bsz, H, dk, dv), q.dtype), jnp.zeros((Bsz, H, dk), q.dtype),
            jnp.zeros((Bsz, H), q.dtype))
    xs = tuple(jnp.moveaxis(t, 2, 0) for t in
               (q, k, v, b, m_intra, num_intra, den_intra, g_kv, m_kv, b_last))
    _, h = lax.scan(step, init, xs)
    return jnp.moveaxis(h, 0, 2).reshape(Bsz, H, S, dv)


def deltanet_conformer_mixer(x, w_in, conv_qkv, a_log, dt_bias, o_norm_g, dw_w, dw_b, cn_g, cn_b, w_out):
    Bsz, S, _ = x.shape
    f32 = jnp.float32
    h = x @ w_in
    qkv, z, b_beta, a_dec, glu = jnp.split(
        h, [3 * A_DIM, 4 * A_DIM, 4 * A_DIM + A_HEADS, 4 * A_DIM + 2 * A_HEADS], axis=-1)
    qkv = jax.nn.silu(causal_depthwise_conv(qkv, conv_qkv))
    q, k, v = jnp.split(qkv, 3, axis=-1)
    to_heads = lambda t: t.reshape(Bsz, S, A_HEADS, A_HEAD_DIM).transpose(0, 2, 1, 3).astype(f32)
    q = l2norm(to_heads(q)) * (A_HEAD_DIM ** -0.5)
    k = l2norm(to_heads(k))
    v = to_heads(v)
    beta = jax.nn.sigmoid(b_beta.astype(f32)).transpose(0, 2, 1)
    g = (-jnp.exp(a_log.astype(f32)) * jax.nn.softplus(a_dec.astype(f32) + dt_bias.astype(f32))
         ).transpose(0, 2, 1)
    o = gated_delta_rule(q, k, v, beta, g).transpose(0, 2, 1, 3)
    o = rms_norm(o, o_norm_g.astype(f32)) * jax.nn.silu(
        z.reshape(Bsz, S, A_HEADS, A_HEAD_DIM).astype(f32))
    o_a = o.reshape(Bsz, S, A_DIM).astype(x.dtype)
    g_a, g_b = jnp.split(glu, 2, axis=-1)
    u = g_a * jax.nn.sigmoid(g_b)
    u = causal_depthwise_conv(u, dw_w) + dw_b
    u = jax.nn.silu(layer_norm(u, cn_g, cn_b))
    return jnp.concatenate([o_a, u], axis=-1) @ w_out


def mlstm_mixer(x, w_in, b_i, b_f, norm_g, w_out):
    Bsz, S, _ = x.shape
    f32 = jnp.float32
    h = x @ w_in
    q, k, v, o_pre, i_pre, f_pre = jnp.split(
        h, [C_QK, 2 * C_QK, 2 * C_QK + C_V, 2 * C_QK + 2 * C_V, 2 * C_QK + 2 * C_V + C_HEADS], axis=-1)
    heads = lambda t, d: t.reshape(Bsz, S, C_HEADS, d).transpose(0, 2, 1, 3).astype(f32)
    q = heads(q, C_QK_HEAD) * (C_QK_HEAD ** -0.5)
    k = heads(k, C_QK_HEAD)
    v = heads(v, C_V_HEAD)
    log_i = soft_cap(i_pre.astype(f32) + b_i.astype(f32)).transpose(0, 2, 1)
    log_f = jax.nn.log_sigmoid(soft_cap(f_pre.astype(f32) + b_f.astype(f32))).transpose(0, 2, 1)
    hh = mlstm_chunked(q, k, v, log_i, log_f).transpose(0, 2, 1, 3)
    hh = rms_norm(hh, norm_g.astype(f32).reshape(C_HEADS, C_V_HEAD)).reshape(Bsz, S, C_V)
    hh = (hh * jax.nn.sigmoid(o_pre.astype(f32))).astype(x.dtype)
    return hh @ w_out


def swiglu(x, w_gate, w_up, w_down):
    return (jax.nn.silu(x @ w_gate) * (x @ w_up)) @ w_down


def moe_swiglu(x, w_router, b_router, w_gate, w_up, w_down):
    logits = (x @ w_router).astype(jnp.float32) + b_router.astype(jnp.float32)
    top_val, top_idx = lax.top_k(logits, TOP_K)
    top_w = jax.nn.softmax(top_val, axis=-1)
    combine = jnp.sum(jax.nn.one_hot(top_idx, N_EXPERTS, dtype=jnp.float32) * top_w[..., None], axis=-2)
    combine = combine.astype(x.dtype)
    y = jnp.zeros_like(x)
    for e in range(N_EXPERTS):
        y = y + combine[..., e:e + 1] * swiglu(x, w_gate[e], w_up[e], w_down[e])
    return y


def setup_inputs(seed: int = 0) -> dict:
    key = jax.random.key(seed)
    k = jax.random.split(key, 32)
    f32 = jnp.float32
    nrm = lambda kk, shape, scale: jax.random.normal(kk, shape, f32) * scale
    x = nrm(k[0], (BATCH, SEQ, D_MODEL), 1.0)
    p = nrm(k[1], (DEPTH, BATCH, SEQ, PLE_DIM), 1.0)
    ab_w_in = nrm(k[2], (N_EVEN, D_MODEL, AB_IN), D_MODEL ** -0.5)
    ab_conv_qkv = nrm(k[3], (N_EVEN, SHORT_CONV, 3 * A_DIM), SHORT_CONV ** -0.5)
    ab_a_log = jnp.log(jax.random.uniform(k[4], (N_EVEN, A_HEADS), f32, 1.0, 16.0))
    dt = jnp.exp(jax.random.uniform(k[5], (N_EVEN, A_HEADS), f32, math.log(1e-3), math.log(1e-1)))
    ab_dt_bias = dt + jnp.log(-jnp.expm1(-dt))
    ab_o_norm_g = 1.0 + nrm(k[6], (N_EVEN, A_HEAD_DIM), 0.02)
    ab_dw_w = nrm(k[7], (N_EVEN, B_CONV, B_CHANNELS), B_CONV ** -0.5)
    ab_dw_b = nrm(k[8], (N_EVEN, B_CHANNELS), 0.02)
    ab_cn_g = 1.0 + nrm(k[9], (N_EVEN, B_CHANNELS), 0.02)
    ab_cn_b = nrm(k[10], (N_EVEN, B_CHANNELS), 0.02)
    ab_w_out = nrm(k[11], (N_EVEN, AB_OUT, D_MODEL), AB_OUT ** -0.5 * DN_BETA)
    ffn_w_gate = nrm(k[12], (N_EVEN, D_MODEL, D_FF), D_MODEL ** -0.5)
    ffn_w_up = nrm(k[13], (N_EVEN, D_MODEL, D_FF), D_MODEL ** -0.5)
    ffn_w_down = nrm(k[14], (N_EVEN, D_FF, D_MODEL), D_FF ** -0.5 * DN_BETA)
    c_w_in = nrm(k[15], (N_ODD, D_MODEL, C_IN), D_MODEL ** -0.5)
    c_b_i = -2.0 + nrm(k[16], (N_ODD, C_HEADS), 0.1)
    c_b_f = jnp.linspace(3.0, 6.0, C_HEADS, dtype=f32)[None, :] + nrm(k[17], (N_ODD, C_HEADS), 0.1)
    c_norm_g = 1.0 + nrm(k[18], (N_ODD, C_V), 0.02)
    c_w_out = nrm(k[19], (N_ODD, C_V, D_MODEL), C_V ** -0.5 * DN_BETA)
    moe_w_router = nrm(k[20], (N_ODD, D_MODEL, N_EXPERTS), D_MODEL ** -0.5)
    moe_b_router = nrm(k[21], (N_ODD, N_EXPERTS), 0.01)
    moe_w_gate = nrm(k[22], (N_ODD, N_EXPERTS, D_MODEL, D_FF), D_MODEL ** -0.5)
    moe_w_up = nrm(k[23], (N_ODD, N_EXPERTS, D_MODEL, D_FF), D_MODEL ** -0.5)
    moe_w_down = nrm(k[24], (N_ODD, N_EXPERTS, D_FF, D_MODEL), D_FF ** -0.5 * DN_BETA)
    ln_mix_g = 1.0 + nrm(k[25], (DEPTH, D_MODEL), 0.02)
    ln_mix_b = nrm(k[26], (DEPTH, D_MODEL), 0.02)
    ln_ffn_g = 1.0 + nrm(k[27], (DEPTH, D_MODEL), 0.02)
    ln_ffn_b = nrm(k[28], (DEPTH, D_MODEL), 0.02)
    ple_w_proj = nrm(k[29], (DEPTH, PLE_DIM, D_MODEL), PLE_DIM ** -0.5)
    ple_w_gate = nrm(k[30], (DEPTH, D_MODEL, D_MODEL), D_MODEL ** -0.5)
    return {"x": x, "p": p, "ab_w_in": ab_w_in, "ab_conv_qkv": ab_conv_qkv, "ab_a_log": ab_a_log,
            "ab_dt_bias": ab_dt_bias, "ab_o_norm_g": ab_o_norm_g, "ab_dw_w": ab_dw_w, "ab_dw_b": ab_dw_b,
            "ab_cn_g": ab_cn_g, "ab_cn_b": ab_cn_b, "ab_w_out": ab_w_out, "ffn_w_gate": ffn_w_gate,
            "ffn_w_up": ffn_w_up, "ffn_w_down": ffn_w_down, "c_w_in": c_w_in, "c_b_i": c_b_i,
            "c_b_f": c_b_f, "c_norm_g": c_norm_g, "c_w_out": c_w_out, "moe_w_router": moe_w_router,
            "moe_b_router": moe_b_router, "moe_w_gate": moe_w_gate, "moe_w_up": moe_w_up,
            "moe_w_down": moe_w_down, "ln_mix_g": ln_mix_g, "ln_mix_b": ln_mix_b, "ln_ffn_g": ln_ffn_g,
            "ln_ffn_b": ln_ffn_b, "ple_w_proj": ple_w_proj, "ple_w_gate": ple_w_gate}


def reference(x, p, ab_w_in, ab_conv_qkv, ab_a_log, ab_dt_bias, ab_o_norm_g, ab_dw_w, ab_dw_b,
              ab_cn_g, ab_cn_b, ab_w_out, ffn_w_gate, ffn_w_up, ffn_w_down, c_w_in, c_b_i, c_b_f,
              c_norm_g, c_w_out, moe_w_router, moe_b_router, moe_w_gate, moe_w_up, moe_w_down,
              ln_mix_g, ln_mix_b, ln_ffn_g, ln_ffn_b, ple_w_proj, ple_w_gate):
    for i in range(DEPTH):
        j = i // 2
        if i % 2 == 0:
            mix = deltanet_conformer_mixer(x, ab_w_in[j], ab_conv_qkv[j], ab_a_log[j], ab_dt_bias[j],
                                           ab_o_norm_g[j], ab_dw_w[j], ab_dw_b[j], ab_cn_g[j],
                                           ab_cn_b[j], ab_w_out[j])
            x = layer_norm(DN_ALPHA * x + mix, ln_mix_g[i], ln_mix_b[i])
            ff = swiglu(x, ffn_w_gate[j], ffn_w_up[j], ffn_w_down[j])
        else:
            mix = mlstm_mixer(x, c_w_in[j], c_b_i[j], c_b_f[j], c_norm_g[j], c_w_out[j])
            x = layer_norm(DN_ALPHA * x + mix, ln_mix_g[i], ln_mix_b[i])
            ff = moe_swiglu(x, moe_w_router[j], moe_b_router[j], moe_w_gate[j], moe_w_up[j],
                            moe_w_down[j])
        x = layer_norm(DN_ALPHA * x + ff, ln_ffn_g[i], ln_ffn_b[i])
        x = x + jax.nn.sigmoid(x @ ple_w_gate[i]) * (p[i] @ ple_w_proj[i])
    return x
```

```python
import functools

import jax
import jax.numpy as jnp
from jax import lax
from jax.experimental import pallas as pl
from jax.experimental.pallas import tpu as pltpu

F32 = jnp.float32
BF16 = jnp.bfloat16

D_MODEL = 1024
N_HEADS = 4
CHUNK = 64
A_HEAD = 128
A_DIM = N_HEADS * A_HEAD
SHORT_CONV = 4
B_CH = 512
B_CONV = 31
C_QK_HEAD = 128
C_V_HEAD = 256
C_QK = N_HEADS * C_QK_HEAD
C_V = N_HEADS * C_V_HEAD
GATE_CAP = 15.0
D_FF = 3584
N_EXPERTS = 8
PLE_DIM = 256
DEPTH = 2
DN_ALPHA = (2 * DEPTH) ** 0.25
LANES = 128
QKV_HALO = 8
DW_HALO = 32
VMEM_LIMIT = 56 * 1024 * 1024


def _dot(a, b):
    return jnp.dot(a, b, preferred_element_type=F32)


def _dot_nt(a, b):
    return lax.dot_general(a, b, (((1,), (1,)), ((), ())), preferred_element_type=F32)


def _dot_tn(a, b):
    return lax.dot_general(a, b, (((0,), (0,)), ((), ())), preferred_element_type=F32)


def _sigmoid(t):
    return 1.0 / (1.0 + jnp.exp(-t))


def _silu(t):
    return t * _sigmoid(t)


def _softplus(t):
    return jnp.maximum(t, 0.0) + jnp.log1p(jnp.exp(-jnp.abs(t)))


def _layer_norm(t, g, b):
    mu = jnp.mean(t, axis=-1, keepdims=True)
    c = t - mu
    var = jnp.mean(c * c, axis=-1, keepdims=True)
    return c * lax.rsqrt(var + 1e-5) * g + b


def _chunk_cumsum(val):
    row = lax.broadcasted_iota(jnp.int32, val.shape, 0) & (CHUNK - 1)
    sh = 1
    while sh < CHUNK:
        val = val + jnp.where(row >= sh, pltpu.roll(val, sh, axis=0), 0.0)
        sh *= 2
    return val


def _params(*sem):
    return pltpu.CompilerParams(dimension_semantics=sem, vmem_limit_bytes=VMEM_LIMIT)


def _ab_front_kernel(x_ref, wm_ref, ws_ref, cw_ref, gp_ref, dww_ref, dwp_ref,
                     q_ref, k_ref, v_ref, z_ref, sc_ref, sct_ref, u_ref,
                     qkv_ext, u_ext):
    s = pl.program_id(1)
    ts = x_ref.shape[1]
    xb = x_ref[0].astype(BF16)

    @pl.when(s == 0)
    def _():
        qkv_ext[0:QKV_HALO, :] = jnp.zeros((QKV_HALO, 3 * A_DIM), F32)
        u_ext[0:DW_HALO, :] = jnp.zeros((DW_HALO, B_CH), F32)

    @pl.when(s > 0)
    def _():
        qkv_ext[0:QKV_HALO, :] = qkv_ext[ts:ts + QKV_HALO, :]
        u_ext[0:DW_HALO, :] = u_ext[ts:ts + DW_HALO, :]

    qkv_ext[QKV_HALO:QKV_HALO + ts, :] = _dot(xb, wm_ref[:, 0:3 * A_DIM])
    acc = None
    for j in range(SHORT_CONV):
        off = QKV_HALO - (SHORT_CONV - 1) + j
        term = cw_ref[j:j + 1, :] * qkv_ext[off:off + ts, :]
        acc = term if acc is None else acc + term
    qkv = _silu(acc)
    for h in range(N_HEADS):
        lo, hi = h * A_HEAD, (h + 1) * A_HEAD
        qh = qkv[:, lo:hi]
        kh = qkv[:, A_DIM + lo:A_DIM + hi]
        qn = lax.rsqrt(jnp.sum(qh * qh, axis=-1, keepdims=True) + 1e-6)
        kn = lax.rsqrt(jnp.sum(kh * kh, axis=-1, keepdims=True) + 1e-6)
        q_ref[0, :, lo:hi] = qh * qn * (A_HEAD ** -0.5)
        k_ref[0, :, lo:hi] = kh * kn
    v_ref[0] = qkv[:, 2 * A_DIM:3 * A_DIM]
    z_ref[0] = _dot(xb, wm_ref[:, 3 * A_DIM:4 * A_DIM])

    hs = _dot(xb, ws_ref[...])
    lane = lax.broadcasted_iota(jnp.int32, hs.shape, 1)
    beta = _sigmoid(hs)
    g = -jnp.exp(gp_ref[0:1, :]) * _softplus(hs + gp_ref[1:2, :])
    sc = jnp.where(lane < N_HEADS, beta, _chunk_cumsum(g))
    sc_ref[0] = sc
    sct_ref[0] = jnp.transpose(sc)[0:8, :]

    glu = _dot(xb, wm_ref[:, 4 * A_DIM:4 * A_DIM + 2 * B_CH])
    u_ext[DW_HALO:DW_HALO + ts, :] = glu[:, 0:B_CH] * _sigmoid(glu[:, B_CH:2 * B_CH])
    acc = None
    for j in range(B_CONV):
        off = DW_HALO - (B_CONV - 1) + j
        term = dww_ref[j:j + 1, :] * u_ext[off:off + ts, :]
        acc = term if acc is None else acc + term
    u = acc + dwp_ref[0:1, :]
    u = _layer_norm(u, dwp_ref[1:2, :], dwp_ref[2:3, :])
    u_ref[0] = _silu(u).astype(BF16)


def _ab_front(x, wm, ws, cw, gp, dww, dwp, ts):
    B, S, D = x.shape
    grid = (B, S // ts)
    tok = lambda w: pl.BlockSpec((1, ts, w), lambda b, s: (b, s, 0))
    full = lambda a: pl.BlockSpec(a.shape, lambda b, s: (0,) * a.ndim)
    out_shape = (
        jax.ShapeDtypeStruct((B, S, A_DIM), F32),
        jax.ShapeDtypeStruct((B, S, A_DIM), F32),
        jax.ShapeDtypeStruct((B, S, A_DIM), F32),
        jax.ShapeDtypeStruct((B, S, A_DIM), F32),
        jax.ShapeDtypeStruct((B, S, LANES), F32),
        jax.ShapeDtypeStruct((B, 8, S), F32),
        jax.ShapeDtypeStruct((B, S, B_CH), BF16),
    )
    out_specs = (tok(A_DIM), tok(A_DIM), tok(A_DIM), tok(A_DIM), tok(LANES),
                 pl.BlockSpec((1, 8, ts), lambda b, s: (b, 0, s)), tok(B_CH))
    return pl.pallas_call(
        _ab_front_kernel,
        grid=grid,
        in_specs=[tok(D), full(wm), full(ws), full(cw), full(gp), full(dww), full(dwp)],
        out_specs=out_specs,
        out_shape=out_shape,
        scratch_shapes=[pltpu.VMEM((ts + QKV_HALO, 3 * A_DIM), F32),
                        pltpu.VMEM((ts + DW_HALO, B_CH), F32)],
        compiler_params=_params("arbitrary", "arbitrary"),
        name="ab_front",
    )(x, wm, ws, cw, gp, dww, dwp)


def _gdn_kernel(q_ref, k_ref, v_ref, z_ref, sc_ref, sct_ref, ng_ref, o_ref, st_ref):
    s = pl.program_id(1)
    ts = q_ref.shape[1]

    @pl.when(s == 0)
    def _():
        st_ref[...] = jnp.zeros(st_ref.shape, F32)

    row = lax.broadcasted_iota(jnp.int32, (CHUNK, CHUNK), 0)
    col = lax.broadcasted_iota(jnp.int32, (CHUNK, CHUNK), 1)
    causal = row >= col
    strict = row > col
    eye = jnp.where(row == col, 1.0, 0.0).astype(F32)

    for c in range(ts // CHUNK):
        r0, r1 = c * CHUNK, (c + 1) * CHUNK
        for h in range(N_HEADS):
            lo, hi = h * A_HEAD, (h + 1) * A_HEAD
            qh = q_ref[0, r0:r1, lo:hi]
            kh = k_ref[0, r0:r1, lo:hi]
            vh = v_ref[0, r0:r1, lo:hi]
            beta = sc_ref[0, r0:r1, h:h + 1]
            gc = sc_ref[0, r0:r1, N_HEADS + h:N_HEADS + h + 1]
            gc_row = sct_ref[0, N_HEADS + h:N_HEADS + h + 1, r0:r1]
            gc_last = gc[CHUNK - 1:CHUNK, :]
            decay = jnp.where(causal, jnp.exp(jnp.where(causal, gc - gc_row, 0.0)), 0.0)
            kb = kh * beta
            egc = jnp.exp(gc)
            kq = _dot_nt(jnp.concatenate([kb, qh], axis=0).astype(BF16), kh.astype(BF16))
            a_mat = jnp.where(strict, kq[0:CHUNK] * decay, 0.0)
            attn = (kq[CHUNK:2 * CHUNK] * decay).astype(BF16)
            pw = -a_mat
            tinv = eye + pw
            for _ in range(5):
                pwb = pw.astype(BF16)
                pw = _dot(pwb, pwb)
                tinv = tinv + _dot(tinv.astype(BF16), pw.astype(BF16))
            rhs = jnp.concatenate([vh * beta, kb * egc], axis=1).astype(BF16)
            uw = _dot(tinv.astype(BF16), rhs).astype(BF16)
            auw = _dot(attn, uw)
            k_dec = (kh * jnp.exp(gc_last - gc)).astype(BF16)
            kuw = _dot_tn(k_dec, uw)
            q_eff = (qh * egc - auw[:, A_HEAD:]).astype(BF16)
            st = st_ref[h]
            stb = st.astype(BF16)
            o = _dot(q_eff, stb) + auw[:, 0:A_HEAD]
            st_ref[h] = (jnp.exp(gc_last) * st + kuw[:, 0:A_HEAD]
                         - _dot(kuw[:, A_HEAD:].astype(BF16), stb))
            o = o * lax.rsqrt(jnp.mean(o * o, axis=-1, keepdims=True) + 1e-6) * ng_ref[0:1, :]
            o_ref[0, r0:r1, lo:hi] = (o * _silu(z_ref[0, r0:r1, lo:hi])).astype(BF16)


def _gdn(q, k, v, z, sc, sct, ng, ts):
    B, S, _ = q.shape
    tok = lambda w: pl.BlockSpec((1, ts, w), lambda b, s: (b, s, 0))
    return pl.pallas_call(
        _gdn_kernel,
        grid=(B, S // ts),
        in_specs=[tok(A_DIM), tok(A_DIM), tok(A_DIM), tok(A_DIM), tok(LANES),
                  pl.BlockSpec((1, 8, ts), lambda b, s: (b, 0, s)),
                  pl.BlockSpec(ng.shape, lambda b, s: (0, 0))],
        out_specs=tok(A_DIM),
        out_shape=jax.ShapeDtypeStruct((B, S, A_DIM), BF16),
        scratch_shapes=[pltpu.VMEM((N_HEADS, A_HEAD, A_HEAD), F32)],
        compiler_params=_params("arbitrary", "arbitrary"),
        name="gdn_scan",
    )(q, k, v, z, sc, sct, ng)


def _proj_ln_kernel(*refs, n_in):
    a_refs = refs[0:n_in]
    w_refs = refs[n_in:2 * n_in]
    x_ref, lnp_ref, o_ref = refs[2 * n_in:2 * n_in + 3]
    mix = None
    for a_ref, w_ref in zip(a_refs, w_refs):
        t = _dot(a_ref[...], w_ref[...])
        mix = t if mix is None else mix + t
    o_ref[...] = _layer_norm(DN_ALPHA * x_ref[...] + mix, lnp_ref[0:1, :], lnp_ref[1:2, :])


def _proj_ln(acts, weights, x, lnp, tm):
    T, D = x.shape
    n_in = len(acts)
    in_specs = ([pl.BlockSpec((tm, a.shape[1]), lambda i: (i, 0)) for a in acts]
                + [pl.BlockSpec(w.shape, lambda i: (0, 0)) for w in weights]
                + [pl.BlockSpec((tm, D), lambda i: (i, 0)),
                   pl.BlockSpec(lnp.shape, lambda i: (0, 0))])
    return pl.pallas_call(
        functools.partial(_proj_ln_kernel, n_in=n_in),
        grid=(T // tm,),
        in_specs=in_specs,
        out_specs=pl.BlockSpec((tm, D), lambda i: (i, 0)),
        out_shape=jax.ShapeDtypeStruct((T, D), F32),
        compiler_params=_params("arbitrary"),
        name="proj_ln",
    )(*acts, *weights, x, lnp)


def _ffn_epilogue(x, acc, lnp_ref, p_ref, wpg_ref, wpp_ref, o_ref):
    y = _layer_norm(DN_ALPHA * x + acc, lnp_ref[0:1, :], lnp_ref[1:2, :])
    gate = _sigmoid(_dot(y.astype(BF16), wpg_ref[...]))
    o_ref[...] = y + gate * _dot(p_ref[...].astype(BF16), wpp_ref[...])


def _ffn_kernel(x_ref, wg_ref, wu_ref, wd_ref, lnp_ref, p_ref, wpg_ref, wpp_ref,
                o_ref, xb_ref, acc_ref):
    j = pl.program_id(1)

    @pl.when(j == 0)
    def _():
        xb_ref[...] = x_ref[...].astype(BF16)
        acc_ref[...] = jnp.zeros(acc_ref.shape, F32)

    xb = xb_ref[...]
    hid = _silu(_dot(xb, wg_ref[...])) * _dot(xb, wu_ref[...])
    acc_ref[...] += _dot(hid.astype(BF16), wd_ref[...])

    @pl.when(j == pl.num_programs(1) - 1)
    def _():
        _ffn_epilogue(x_ref[...], acc_ref[...], lnp_ref, p_ref, wpg_ref, wpp_ref, o_ref)


def _ffn(x, wg, wu, wd, lnp, p, wpg, wpp, tm, tf):
    T, D = x.shape
    F = wg.shape[1]
    return pl.pallas_call(
        _ffn_kernel,
        grid=(T // tm, F // tf),
        in_specs=[pl.BlockSpec((tm, D), lambda i, j: (i, 0)),
                  pl.BlockSpec((D, tf), lambda i, j: (0, j)),
                  pl.BlockSpec((D, tf), lambda i, j: (0, j)),
                  pl.BlockSpec((tf, D), lambda i, j: (j, 0)),
                  pl.BlockSpec(lnp.shape, lambda i, j: (0, 0)),
                  pl.BlockSpec((tm, PLE_DIM), lambda i, j: (i, 0)),
                  pl.BlockSpec(wpg.shape, lambda i, j: (0, 0)),
                  pl.BlockSpec(wpp.shape, lambda i, j: (0, 0))],
        out_specs=pl.BlockSpec((tm, D), lambda i, j: (i, 0)),
        out_shape=jax.ShapeDtypeStruct((T, D), F32),
        scratch_shapes=[pltpu.VMEM((tm, D), BF16), pltpu.VMEM((tm, D), F32)],
        compiler_params=_params("arbitrary", "arbitrary"),
        name="ffn",
    )(x, wg, wu, wd, lnp, p, wpg, wpp)


def _moe_dense_kernel(x_ref, cmb_ref, wg_ref, wu_ref, wd_ref, lnp_ref, p_ref, wpg_ref, wpp_ref,
                      o_ref, xb_ref, acc_ref):
    e = pl.program_id(1)
    j = pl.program_id(2)
    first = jnp.logical_and(e == 0, j == 0)
    last = jnp.logical_and(e == pl.num_programs(1) - 1, j == pl.num_programs(2) - 1)

    @pl.when(first)
    def _():
        xb_ref[...] = x_ref[...].astype(BF16)
        acc_ref[...] = jnp.zeros(acc_ref.shape, F32)

    xb = xb_ref[...]
    hid = _silu(_dot(xb, wg_ref[0])) * _dot(xb, wu_ref[0])
    lane = lax.broadcasted_iota(jnp.int32, cmb_ref.shape, 1)
    w_e = jnp.sum(jnp.where(lane == e, cmb_ref[...], 0.0), axis=-1, keepdims=True)
    acc_ref[...] += w_e * _dot(hid.astype(BF16), wd_ref[0])

    @pl.when(last)
    def _():
        _ffn_epilogue(x_ref[...], acc_ref[...], lnp_ref, p_ref, wpg_ref, wpp_ref, o_ref)


def _moe_dense(x, cmb, wg, wu, wd, lnp, p, wpg, wpp, tm, tf):
    T, D = x.shape
    E, _, F = wg.shape
    return pl.pallas_call(
        _moe_dense_kernel,
        grid=(T // tm, E, F // tf),
        in_specs=[pl.BlockSpec((tm, D), lambda i, e, j: (i, 0)),
                  pl.BlockSpec((tm, LANES), lambda i, e, j: (i, 0)),
                  pl.BlockSpec((1, D, tf), lambda i, e, j: (e, 0, j)),
                  pl.BlockSpec((1, D, tf), lambda i, e, j: (e, 0, j)),
                  pl.BlockSpec((1, tf, D), lambda i, e, j: (e, j, 0)),
                  pl.BlockSpec(lnp.shape, lambda i, e, j: (0, 0)),
                  pl.BlockSpec((tm, PLE_DIM), lambda i, e, j: (i, 0)),
                  pl.BlockSpec(wpg.shape, lambda i, e, j: (0, 0)),
                  pl.BlockSpec(wpp.shape, lambda i, e, j: (0, 0))],
        out_specs=pl.BlockSpec((tm, D), lambda i, e, j: (i, 0)),
        out_shape=jax.ShapeDtypeStruct((T, D), F32),
        scratch_shapes=[pltpu.VMEM((tm, D), BF16), pltpu.VMEM((tm, D), F32)],
        compiler_params=_params("arbitrary", "arbitrary", "arbitrary"),
        name="moe_dense",
    )(x, cmb, wg, wu, wd, lnp, p, wpg, wpp)


def _router_kernel(x_ref, wh_ref, wl_ref, br_ref, cmb_ref):
    x = x_ref[...]
    xh = x.astype(BF16)
    xl = (x - xh.astype(F32)).astype(BF16)
    logits = (_dot(xh, wh_ref[...]) + _dot(xl, wh_ref[...]) + _dot(xh, wl_ref[...])
              + br_ref[0:1, :])
    lane = lax.broadcasted_iota(jnp.int32, logits.shape, 1)
    neg = jnp.float32(-jnp.inf)
    logits = jnp.where(lane < N_EXPERTS, logits, neg)
    m1 = jnp.max(logits, axis=-1, keepdims=True)
    i1 = jnp.min(jnp.where(logits == m1, lane, LANES), axis=-1, keepdims=True)
    sel1 = lane == i1
    rest = jnp.where(sel1, neg, logits)
    m2 = jnp.max(rest, axis=-1, keepdims=True)
    i2 = jnp.min(jnp.where(rest == m2, lane, LANES), axis=-1, keepdims=True)
    sel2 = lane == i2
    e2 = jnp.exp(m2 - m1)
    den = 1.0 + e2
    cmb_ref[...] = jnp.where(sel1, 1.0 / den, 0.0) + jnp.where(sel2, e2 / den, 0.0)


def _router(x, wh, wl, br, tm):
    T, D = x.shape
    return pl.pallas_call(
        _router_kernel,
        grid=(T // tm,),
        in_specs=[pl.BlockSpec((tm, D), lambda i: (i, 0)),
                  pl.BlockSpec(wh.shape, lambda i: (0, 0)),
                  pl.BlockSpec(wl.shape, lambda i: (0, 0)),
                  pl.BlockSpec(br.shape, lambda i: (0, 0))],
        out_specs=pl.BlockSpec((tm, LANES), lambda i: (i, 0)),
        out_shape=jax.ShapeDtypeStruct((T, LANES), F32),
        compiler_params=_params("arbitrary"),
        name="router",
    )(x, wh, wl, br)


def _c_front_kernel(x_ref, wm_ref, ws_ref, gb_ref, q_ref, k_ref, v_ref, op_ref, sc_ref, sct_ref):
    xb = x_ref[0].astype(BF16)
    q_ref[0] = _dot(xb, wm_ref[:, 0:C_QK]) * (C_QK_HEAD ** -0.5)
    k_ref[0] = _dot(xb, wm_ref[:, C_QK:2 * C_QK])
    v_ref[0] = _dot(xb, wm_ref[:, 2 * C_QK:2 * C_QK + C_V])
    op_ref[0] = _dot(xb, wm_ref[:, 2 * C_QK + C_V:2 * C_QK + 2 * C_V])
    hs = _dot(xb, ws_ref[...]) + gb_ref[0:1, :]
    capped = GATE_CAP * jnp.tanh(hs / GATE_CAP)
    lane = lax.broadcasted_iota(jnp.int32, hs.shape, 1)
    log_f = jnp.minimum(capped, 0.0) - jnp.log1p(jnp.exp(-jnp.abs(capped)))
    sc = jnp.where(lane < N_HEADS, capped, _chunk_cumsum(log_f))
    sc_ref[0] = sc
    sct_ref[0] = jnp.transpose(sc)[0:8, :]


def _c_front(x, wm, ws, gb, ts):
    B, S, D = x.shape
    tok = lambda w: pl.BlockSpec((1, ts, w), lambda b, s: (b, s, 0))
    full = lambda a: pl.BlockSpec(a.shape, lambda b, s: (0,) * a.ndim)
    out_shape = (
        jax.ShapeDtypeStruct((B, S, C_QK), F32),
        jax.ShapeDtypeStruct((B, S, C_QK), F32),
        jax.ShapeDtypeStruct((B, S, C_V), F32),
        jax.ShapeDtypeStruct((B, S, C_V), F32),
        jax.ShapeDtypeStruct((B, S, LANES), F32),
        jax.ShapeDtypeStruct((B, 8, S), F32),
    )
    out_specs = (tok(C_QK), tok(C_QK), tok(C_V), tok(C_V), tok(LANES),
                 pl.BlockSpec((1, 8, ts), lambda b, s: (b, 0, s)))
    return pl.pallas_call(
        _c_front_kernel,
        grid=(B, S // ts),
        in_specs=[tok(D), full(wm), full(ws), full(gb)],
        out_specs=out_specs,
        out_shape=out_shape,
        compiler_params=_params("arbitrary", "arbitrary"),
        name="c_front",
    )(x, wm, ws, gb)


def _mlstm_kernel(q_ref, k_ref, v_ref, op_ref, sc_ref, sct_ref, ng_ref, o_ref,
                  c_ref, n_ref, m_ref):
    s = pl.program_id(1)
    ts = q_ref.shape[1]

    @pl.when(s == 0)
    def _():
        c_ref[...] = jnp.zeros(c_ref.shape, F32)
        n_ref[...] = jnp.zeros(n_ref.shape, F32)
        m_ref[...] = jnp.zeros(m_ref.shape, F32)

    row = lax.broadcasted_iota(jnp.int32, (CHUNK, CHUNK), 0)
    col = lax.broadcasted_iota(jnp.int32, (CHUNK, CHUNK), 1)
    causal = row >= col
    neg = jnp.float32(-jnp.inf)

    for c in range(ts // CHUNK):
        r0, r1 = c * CHUNK, (c + 1) * CHUNK
        for h in range(N_HEADS):
            klo, khi = h * C_QK_HEAD, (h + 1) * C_QK_HEAD
            vlo, vhi = h * C_V_HEAD, (h + 1) * C_V_HEAD
            qh = q_ref[0, r0:r1, klo:khi]
            kh = k_ref[0, r0:r1, klo:khi]
            vb = v_ref[0, r0:r1, vlo:vhi].astype(BF16)
            li = sc_ref[0, r0:r1, h:h + 1]
            bc = sc_ref[0, r0:r1, N_HEADS + h:N_HEADS + h + 1]
            li_row = sct_ref[0, h:h + 1, r0:r1]
            b_row = sct_ref[0, N_HEADS + h:N_HEADS + h + 1, r0:r1]
            b_last = bc[CHUNK - 1:CHUNK, :]
            dmat = jnp.where(causal, bc - b_row + li_row, neg)
            m_intra = jnp.max(dmat, axis=-1, keepdims=True)
            qb = qh.astype(BF16)
            pmat = jnp.exp(dmat - m_intra) * _dot_nt(qb, kh.astype(BF16))
            num_intra = _dot(pmat.astype(BF16), vb)
            den_intra = jnp.sum(pmat, axis=-1, keepdims=True)
            g_kv = b_last - bc + li
            m_kv = jnp.max(g_kv, axis=0, keepdims=True)

            m_st = m_ref[h, :, 0:1]
            c_st = c_ref[h]
            n_st = n_ref[h]
            inter = bc + m_st
            m_t = jnp.maximum(inter, m_intra)
            s_inter = jnp.exp(inter - m_t)
            s_intra = jnp.exp(m_intra - m_t)
            num = s_inter * _dot(qb, c_st.astype(BF16)) + s_intra * num_intra
            den = (s_inter * jnp.sum(qh * n_st, axis=-1, keepdims=True) + s_intra * den_intra)
            hh = num / jnp.maximum(jnp.abs(den), jnp.exp(-m_t))
            m_new = jnp.maximum(b_last + m_st, m_kv)
            kw = kh * jnp.exp(g_kv - m_new)
            dec = jnp.exp(b_last + m_st - m_new)
            c_ref[h] = dec * c_st + _dot_tn(kw.astype(BF16), vb)
            n_ref[h] = dec * n_st + jnp.sum(kw, axis=0, keepdims=True)
            m_ref[h] = jnp.broadcast_to(m_new, (1, LANES))

            hh = hh * lax.rsqrt(jnp.mean(hh * hh, axis=-1, keepdims=True) + 1e-6)
            hh = hh * ng_ref[0:1, vlo:vhi] * _sigmoid(op_ref[0, r0:r1, vlo:vhi])
            o_ref[0, r0:r1, vlo:vhi] = hh.astype(BF16)


def _mlstm(q, k, v, op, sc, sct, ng, ts):
    B, S, _ = q.shape
    tok = lambda w: pl.BlockSpec((1, ts, w), lambda b, s: (b, s, 0))
    return pl.pallas_call(
        _mlstm_kernel,
        grid=(B, S // ts),
        in_specs=[tok(C_QK), tok(C_QK), tok(C_V), tok(C_V), tok(LANES),
                  pl.BlockSpec((1, 8, ts), lambda b, s: (b, 0, s)),
                  pl.BlockSpec(ng.shape, lambda b, s: (0, 0))],
        out_specs=tok(C_V),
        out_shape=jax.ShapeDtypeStruct((B, S, C_V), BF16),
        scratch_shapes=[pltpu.VMEM((N_HEADS, C_QK_HEAD, C_V_HEAD), F32),
                        pltpu.VMEM((N_HEADS, 1, C_QK_HEAD), F32),
                        pltpu.VMEM((N_HEADS, 1, LANES), F32)],
        compiler_params=_params("arbitrary", "arbitrary"),
        name="mlstm_scan",
    )(q, k, v, op, sc, sct, ng)


def _pad_lanes(a, offset=0):
    return jnp.pad(a, ((0, 0), (offset, LANES - offset - a.shape[1])))


def kernel(x, p, ab_w_in, ab_conv_qkv, ab_a_log, ab_dt_bias, ab_o_norm_g, ab_dw_w, ab_dw_b,
           ab_cn_g, ab_cn_b, ab_w_out, ffn_w_gate, ffn_w_up, ffn_w_down, c_w_in, c_b_i, c_b_f,
           c_norm_g, c_w_out, moe_w_router, moe_b_router, moe_w_gate, moe_w_up, moe_w_down,
           ln_mix_g, ln_mix_b, ln_ffn_g, ln_ffn_b, ple_w_proj, ple_w_gate):
    B, S, D = x.shape
    T = B * S
    ts_front, ts_scan, tm, tf = 512, 256, 512, 512

    w_in = ab_w_in[0]
    n_main = 4 * A_DIM
    wm = jnp.concatenate([w_in[:, :n_main], w_in[:, n_main + 2 * N_HEADS:]], axis=1).astype(BF16)
    ws = _pad_lanes(w_in[:, n_main:n_main + 2 * N_HEADS]).astype(BF16)
    gp = jnp.concatenate([_pad_lanes(ab_a_log[0][None, :], N_HEADS),
                          _pad_lanes(ab_dt_bias[0][None, :], N_HEADS)], axis=0)
    dwp = jnp.stack([ab_dw_b[0], ab_cn_g[0], ab_cn_b[0]], axis=0)
    q, k, v, z, sc, sct, u = _ab_front(x, wm, ws, ab_conv_qkv[0], gp, ab_dw_w[0], dwp, ts_front)
    o_a = _gdn(q, k, v, z, sc, sct, ab_o_norm_g[0][None, :], ts_scan)
    w_out = ab_w_out[0].astype(BF16)
    lnp = jnp.stack([ln_mix_g[0], ln_mix_b[0]], axis=0)
    xf = x.reshape(T, D)
    x1 = _proj_ln([o_a.reshape(T, A_DIM), u.reshape(T, B_CH)], [w_out[:A_DIM], w_out[A_DIM:]],
                  xf, lnp, tm)
    lnp = jnp.stack([ln_ffn_g[0], ln_ffn_b[0]], axis=0)
    x2 = _ffn(x1, ffn_w_gate[0].astype(BF16), ffn_w_up[0].astype(BF16),
              ffn_w_down[0].astype(BF16), lnp, p[0].reshape(T, PLE_DIM),
              ple_w_gate[0].astype(BF16), ple_w_proj[0].astype(BF16), tm, tf)

    w_in = c_w_in[0]
    n_main = 2 * C_QK + 2 * C_V
    wm = w_in[:, :n_main].astype(BF16)
    ws = _pad_lanes(w_in[:, n_main:]).astype(BF16)
    gb = _pad_lanes(jnp.concatenate([c_b_i[0], c_b_f[0]])[None, :])
    q, k, v, op, sc, sct = _c_front(x2.reshape(B, S, D), wm, ws, gb, ts_front)
    hh = _mlstm(q, k, v, op, sc, sct, c_norm_g[0][None, :], ts_scan)
    lnp = jnp.stack([ln_mix_g[1], ln_mix_b[1]], axis=0)
    x3 = _proj_ln([hh.reshape(T, C_V)], [c_w_out[0].astype(BF16)], x2, lnp, tm)

    wr = _pad_lanes(moe_w_router[0])
    wr_hi = wr.astype(BF16)
    wr_lo = (wr - wr_hi.astype(F32)).astype(BF16)
    cmb = _router(x3, wr_hi, wr_lo, _pad_lanes(moe_b_router[0][None, :]), tm)
    lnp = jnp.stack([ln_ffn_g[1], ln_ffn_b[1]], axis=0)
    x4 = _moe_dense(x3, cmb, moe_w_gate[0].astype(BF16), moe_w_up[0].astype(BF16),
                    moe_w_down[0].astype(BF16), lnp, p[1].reshape(T, PLE_DIM),
                    ple_w_gate[1].astype(BF16), ple_w_proj[1].astype(BF16), tm, tf)
    return x4.reshape(B, S, D)
```

```python
import functools

import jax
import jax.numpy as jnp
from jax import lax
from jax.experimental import pallas as pl
from jax.experimental.pallas import tpu as pltpu

F32 = jnp.float32
BF16 = jnp.bfloat16

D_MODEL = 1024
N_HEADS = 4
CHUNK = 64
A_HEAD = 128
A_DIM = N_HEADS * A_HEAD
SHORT_CONV = 4
B_CH = 512
B_CONV = 31
C_QK_HEAD = 128
C_V_HEAD = 256
C_QK = N_HEADS * C_QK_HEAD
C_V = N_HEADS * C_V_HEAD
GATE_CAP = 15.0
D_FF = 3584
N_EXPERTS = 8
PLE_DIM = 256
DEPTH = 2
DN_ALPHA = (2 * DEPTH) ** 0.25
LANES = 128
QKV_HALO = 8
DW_HALO = 32
VMEM_LIMIT = 56 * 1024 * 1024


def _dot(a, b):
    return jnp.dot(a, b, preferred_element_type=F32)


def _dot_nt(a, b):
    return lax.dot_general(a, b, (((1,), (1,)), ((), ())), preferred_element_type=F32)


def _dot_tn(a, b):
    return lax.dot_general(a, b, (((0,), (0,)), ((), ())), preferred_element_type=F32)


def _sigmoid(t):
    return 1.0 / (1.0 + jnp.exp(-t))


def _silu(t):
    return t * _sigmoid(t)


def _softplus(t):
    return jnp.maximum(t, 0.0) + jnp.log1p(jnp.exp(-jnp.abs(t)))


def _layer_norm(t, g, b):
    mu = jnp.mean(t, axis=-1, keepdims=True)
    c = t - mu
    var = jnp.mean(c * c, axis=-1, keepdims=True)
    return c * lax.rsqrt(var + 1e-5) * g + b


def _chunk_cumsum(val):
    row = lax.broadcasted_iota(jnp.int32, val.shape, 0) & (CHUNK - 1)
    sh = 1
    while sh < CHUNK:
        val = val + jnp.where(row >= sh, pltpu.roll(val, sh, axis=0), 0.0)
        sh *= 2
    return val


def _params(*sem):
    return pltpu.CompilerParams(dimension_semantics=sem, vmem_limit_bytes=VMEM_LIMIT)


def _ab_front_kernel(x_ref, wm_ref, ws_ref, cw_ref, gp_ref, dww_ref, dwp_ref,
                     q_ref, k_ref, v_ref, z_ref, sc_ref, sct_ref, u_ref,
                     qkv_ext, u_ext):
    s = pl.program_id(1)
    ts = x_ref.shape[1]
    xb = x_ref[0].astype(BF16)

    @pl.when(s == 0)
    def _():
        qkv_ext[0:QKV_HALO, :] = jnp.zeros((QKV_HALO, 3 * A_DIM), F32)
        u_ext[0:DW_HALO, :] = jnp.zeros((DW_HALO, B_CH), F32)

    @pl.when(s > 0)
    def _():
        qkv_ext[0:QKV_HALO, :] = qkv_ext[ts:ts + QKV_HALO, :]
        u_ext[0:DW_HALO, :] = u_ext[ts:ts + DW_HALO, :]

    qkv_ext[QKV_HALO:QKV_HALO + ts, :] = _dot(xb, wm_ref[:, 0:3 * A_DIM])
    acc = None
    for j in range(SHORT_CONV):
        off = QKV_HALO - (SHORT_CONV - 1) + j
        term = cw_ref[j:j + 1, :] * qkv_ext[off:off + ts, :]
        acc = term if acc is None else acc + term
    qkv = _silu(acc)
    for h in range(N_HEADS):
        lo, hi = h * A_HEAD, (h + 1) * A_HEAD
        qh = qkv[:, lo:hi]
        kh = qkv[:, A_DIM + lo:A_DIM + hi]
        qn = lax.rsqrt(jnp.sum(qh * qh, axis=-1, keepdims=True) + 1e-6)
        kn = lax.rsqrt(jnp.sum(kh * kh, axis=-1, keepdims=True) + 1e-6)
        q_ref[0, :, lo:hi] = qh * qn * (A_HEAD ** -0.5)
        k_ref[0, :, lo:hi] = kh * kn
    v_ref[0] = qkv[:, 2 * A_DIM:3 * A_DIM]
    z_ref[0] = _dot(xb, wm_ref[:, 3 * A_DIM:4 * A_DIM])

    hs = _dot(xb, ws_ref[...])
    lane = lax.broadcasted_iota(jnp.int32, hs.shape, 1)
    beta = _sigmoid(hs)
    g = -jnp.exp(gp_ref[0:1, :]) * _softplus(hs + gp_ref[1:2, :])
    sc = jnp.where(lane < N_HEADS, beta, _chunk_cumsum(g))
    sc_ref[0] = sc
    sct_ref[0] = jnp.transpose(sc)[0:8, :]

    glu = _dot(xb, wm_ref[:, 4 * A_DIM:4 * A_DIM + 2 * B_CH])
    u_ext[DW_HALO:DW_HALO + ts, :] = glu[:, 0:B_CH] * _sigmoid(glu[:, B_CH:2 * B_CH])
    acc = None
    for j in range(B_CONV):
        off = DW_HALO - (B_CONV - 1) + j
        term = dww_ref[j:j + 1, :] * u_ext[off:off + ts, :]
        acc = term if acc is None else acc + term
    u = acc + dwp_ref[0:1, :]
    u = _layer_norm(u, dwp_ref[1:2, :], dwp_ref[2:3, :])
    u_ref[0] = _silu(u).astype(BF16)


def _ab_front(x, wm, ws, cw, gp, dww, dwp, ts):
    B, S, D = x.shape
    grid = (B, S // ts)
    tok = lambda w: pl.BlockSpec((1, ts, w), lambda b, s: (b, s, 0))
    full = lambda a: pl.BlockSpec(a.shape, lambda b, s: (0,) * a.ndim)
    out_shape = (
        jax.ShapeDtypeStruct((B, S, A_DIM), F32),
        jax.ShapeDtypeStruct((B, S, A_DIM), F32),
        jax.ShapeDtypeStruct((B, S, A_DIM), F32),
        jax.ShapeDtypeStruct((B, S, A_DIM), F32),
        jax.ShapeDtypeStruct((B, S, LANES), F32),
        jax.ShapeDtypeStruct((B, 8, S), F32),
        jax.ShapeDtypeStruct((B, S, B_CH), BF16),
    )
    out_specs = (tok(A_DIM), tok(A_DIM), tok(A_DIM), tok(A_DIM), tok(LANES),
                 pl.BlockSpec((1, 8, ts), lambda b, s: (b, 0, s)), tok(B_CH))
    return pl.pallas_call(
        _ab_front_kernel,
        grid=grid,
        in_specs=[tok(D), full(wm), full(ws), full(cw), full(gp), full(dww), full(dwp)],
        out_specs=out_specs,
        out_shape=out_shape,
        scratch_shapes=[pltpu.VMEM((ts + QKV_HALO, 3 * A_DIM), F32),
                        pltpu.VMEM((ts + DW_HALO, B_CH), F32)],
        compiler_params=_params("arbitrary", "arbitrary"),
        name="ab_front",
    )(x, wm, ws, cw, gp, dww, dwp)


def _gdn_kernel(q_ref, k_ref, v_ref, z_ref, sc_ref, sct_ref, ng_ref, o_ref, st_ref):
    s = pl.program_id(1)
    ts = q_ref.shape[1]

    @pl.when(s == 0)
    def _():
        st_ref[...] = jnp.zeros(st_ref.shape, F32)

    row = lax.broadcasted_iota(jnp.int32, (CHUNK, CHUNK), 0)
    col = lax.broadcasted_iota(jnp.int32, (CHUNK, CHUNK), 1)
    causal = row >= col
    strict = row > col
    eye = jnp.where(row == col, 1.0, 0.0).astype(F32)

    n_chunks = ts // CHUNK
    pairs = [(c, h) for c in range(n_chunks) for h in range(N_HEADS)]
    pw, tinv, attn, rhs, k_dec, q_dec, d_last = [], [], [], [], [], [], []
    for c, h in pairs:
        r0, r1 = c * CHUNK, (c + 1) * CHUNK
        lo, hi = h * A_HEAD, (h + 1) * A_HEAD
        qh = q_ref[0, r0:r1, lo:hi]
        kh = k_ref[0, r0:r1, lo:hi]
        vh = v_ref[0, r0:r1, lo:hi]
        beta = sc_ref[0, r0:r1, h:h + 1]
        gc = sc_ref[0, r0:r1, N_HEADS + h:N_HEADS + h + 1]
        gc_row = sct_ref[0, N_HEADS + h:N_HEADS + h + 1, r0:r1]
        gc_last = gc[CHUNK - 1:CHUNK, :]
        decay = jnp.where(causal, jnp.exp(jnp.where(causal, gc - gc_row, 0.0)), 0.0)
        kb = kh * beta
        egc = jnp.exp(gc)
        kq = _dot_nt(jnp.concatenate([kb, qh], axis=0).astype(BF16), kh.astype(BF16))
        a_neg = jnp.where(strict, -kq[0:CHUNK] * decay, 0.0)
        pw.append(a_neg)
        tinv.append(eye + a_neg)
        attn.append((kq[CHUNK:2 * CHUNK] * decay).astype(BF16))
        rhs.append(jnp.concatenate([vh * beta, kb * egc], axis=1).astype(BF16))
        k_dec.append((kh * jnp.exp(gc_last - gc)).astype(BF16))
        q_dec.append(qh * egc)
        d_last.append(jnp.exp(gc_last))
    for _ in range(5):
        for i in range(len(pairs)):
            pwb = pw[i].astype(BF16)
            pw[i] = _dot(pwb, pwb)
        for i in range(len(pairs)):
            tinv[i] = tinv[i] + _dot(tinv[i].astype(BF16), pw[i].astype(BF16))
    uw = [_dot(tinv[i].astype(BF16), rhs[i]).astype(BF16) for i in range(len(pairs))]
    auw = [_dot(attn[i], uw[i]) for i in range(len(pairs))]
    kuw = [_dot_tn(k_dec[i], uw[i]) for i in range(len(pairs))]
    st = [st_ref[h] for h in range(N_HEADS)]
    for i, (c, h) in enumerate(pairs):
        r0, r1 = c * CHUNK, (c + 1) * CHUNK
        lo, hi = h * A_HEAD, (h + 1) * A_HEAD
        stb = st[h].astype(BF16)
        q_eff = (q_dec[i] - auw[i][:, A_HEAD:]).astype(BF16)
        o = _dot(q_eff, stb) + auw[i][:, 0:A_HEAD]
        st[h] = (d_last[i] * st[h] + kuw[i][:, 0:A_HEAD]
                 - _dot(kuw[i][:, A_HEAD:].astype(BF16), stb))
        o = o * lax.rsqrt(jnp.mean(o * o, axis=-1, keepdims=True) + 1e-6) * ng_ref[0:1, :]
        o_ref[0, r0:r1, lo:hi] = (o * _silu(z_ref[0, r0:r1, lo:hi])).astype(BF16)
    for h in range(N_HEADS):
        st_ref[h] = st[h]


def _gdn(q, k, v, z, sc, sct, ng, ts):
    B, S, _ = q.shape
    tok = lambda w: pl.BlockSpec((1, ts, w), lambda b, s: (b, s, 0))
    return pl.pallas_call(
        _gdn_kernel,
        grid=(B, S // ts),
        in_specs=[tok(A_DIM), tok(A_DIM), tok(A_DIM), tok(A_DIM), tok(LANES),
                  pl.BlockSpec((1, 8, ts), lambda b, s: (b, 0, s)),
                  pl.BlockSpec(ng.shape, lambda b, s: (0, 0))],
        out_specs=tok(A_DIM),
        out_shape=jax.ShapeDtypeStruct((B, S, A_DIM), BF16),
        scratch_shapes=[pltpu.VMEM((N_HEADS, A_HEAD, A_HEAD), F32)],
        compiler_params=_params("arbitrary", "arbitrary"),
        name="gdn_scan",
    )(q, k, v, z, sc, sct, ng)


def _proj_ln_kernel(*refs, n_in):
    a_refs = refs[0:n_in]
    w_refs = refs[n_in:2 * n_in]
    x_ref, lnp_ref, o_ref = refs[2 * n_in:2 * n_in + 3]
    mix = None
    for a_ref, w_ref in zip(a_refs, w_refs):
        t = _dot(a_ref[...], w_ref[...])
        mix = t if mix is None else mix + t
    o_ref[...] = _layer_norm(DN_ALPHA * x_ref[...] + mix, lnp_ref[0:1, :], lnp_ref[1:2, :])


def _proj_ln(acts, weights, x, lnp, tm):
    T, D = x.shape
    n_in = len(acts)
    in_specs = ([pl.BlockSpec((tm, a.shape[1]), lambda i: (i, 0)) for a in acts]
                + [pl.BlockSpec(w.shape, lambda i: (0, 0)) for w in weights]
                + [pl.BlockSpec((tm, D), lambda i: (i, 0)),
                   pl.BlockSpec(lnp.shape, lambda i: (0, 0))])
    return pl.pallas_call(
        functools.partial(_proj_ln_kernel, n_in=n_in),
        grid=(T // tm,),
        in_specs=in_specs,
        out_specs=pl.BlockSpec((tm, D), lambda i: (i, 0)),
        out_shape=jax.ShapeDtypeStruct((T, D), F32),
        compiler_params=_params("arbitrary"),
        name="proj_ln",
    )(*acts, *weights, x, lnp)


def _ffn_epilogue(x, acc, lnp_ref, p_ref, wpg_ref, wpp_ref, o_ref):
    y = _layer_norm(DN_ALPHA * x + acc, lnp_ref[0:1, :], lnp_ref[1:2, :])
    gate = _sigmoid(_dot(y.astype(BF16), wpg_ref[...]))
    o_ref[...] = y + gate * _dot(p_ref[...].astype(BF16), wpp_ref[...])


def _ffn_kernel(x_ref, wg_ref, wu_ref, wd_ref, lnp_ref, p_ref, wpg_ref, wpp_ref,
                o_ref, xb_ref):
    j = pl.program_id(1)

    @pl.when(j == 0)
    def _():
        xb_ref[...] = x_ref[...].astype(BF16)
        o_ref[...] = jnp.zeros(o_ref.shape, F32)

    xb = xb_ref[...]
    hid = _silu(_dot(xb, wg_ref[...])) * _dot(xb, wu_ref[...])
    o_ref[...] += _dot(hid.astype(BF16), wd_ref[...])

    @pl.when(j == pl.num_programs(1) - 1)
    def _():
        _ffn_epilogue(x_ref[...], o_ref[...], lnp_ref, p_ref, wpg_ref, wpp_ref, o_ref)


def _ffn(x, wg, wu, wd, lnp, p, wpg, wpp, tm, tf):
    T, D = x.shape
    F = wg.shape[1]
    return pl.pallas_call(
        _ffn_kernel,
        grid=(T // tm, F // tf),
        in_specs=[pl.BlockSpec((tm, D), lambda i, j: (i, 0)),
                  pl.BlockSpec((D, tf), lambda i, j: (0, j)),
                  pl.BlockSpec((D, tf), lambda i, j: (0, j)),
                  pl.BlockSpec((tf, D), lambda i, j: (j, 0)),
                  pl.BlockSpec(lnp.shape, lambda i, j: (0, 0)),
                  pl.BlockSpec((tm, PLE_DIM), lambda i, j: (i, 0)),
                  pl.BlockSpec(wpg.shape, lambda i, j: (0, 0)),
                  pl.BlockSpec(wpp.shape, lambda i, j: (0, 0))],
        out_specs=pl.BlockSpec((tm, D), lambda i, j: (i, 0)),
        out_shape=jax.ShapeDtypeStruct((T, D), F32),
        scratch_shapes=[pltpu.VMEM((tm, D), BF16)],
        compiler_params=_params("arbitrary", "arbitrary"),
        name="ffn",
    )(x, wg, wu, wd, lnp, p, wpg, wpp)


ROUTE_IDX1, ROUTE_IDX2, ROUTE_W1, ROUTE_W2, ROUTE_RANK1, ROUTE_RANK2 = range(6)


def _router_kernel(x_ref, wh_ref, wl_ref, br_ref, info_ref, cnt_ref, carry_ref):
    i = pl.program_id(0)

    @pl.when(i == 0)
    def _():
        carry_ref[...] = jnp.zeros(carry_ref.shape, F32)

    x = x_ref[...]
    tm = x.shape[0]
    xh = x.astype(BF16)
    xl = (x - xh.astype(F32)).astype(BF16)
    logits = (_dot(xh, wh_ref[...]) + _dot(xl, wh_ref[...]) + _dot(xh, wl_ref[...])
              + br_ref[0:1, :])
    lane = lax.broadcasted_iota(jnp.int32, logits.shape, 1)
    neg = jnp.float32(-jnp.inf)
    logits = jnp.where(lane < N_EXPERTS, logits, neg)
    m1 = jnp.max(logits, axis=-1, keepdims=True)
    i1 = jnp.min(jnp.where(logits == m1, lane, LANES), axis=-1, keepdims=True)
    sel1 = lane == i1
    rest = jnp.where(sel1, neg, logits)
    m2 = jnp.max(rest, axis=-1, keepdims=True)
    i2 = jnp.min(jnp.where(rest == m2, lane, LANES), axis=-1, keepdims=True)
    sel2 = lane == i2
    e2 = jnp.exp(m2 - m1)
    den = 1.0 + e2

    sel = jnp.where(jnp.logical_or(sel1, sel2), 1.0, 0.0)
    row = lax.broadcasted_iota(jnp.int32, (tm, tm), 0)
    col = lax.broadcasted_iota(jnp.int32, (tm, tm), 1)
    earlier = jnp.where(row > col, 1.0, 0.0).astype(BF16)
    rank = _dot(earlier, sel.astype(BF16)) + carry_ref[0:1, :]
    rank1 = jnp.sum(jnp.where(sel1, rank, 0.0), axis=-1, keepdims=True)
    rank2 = jnp.sum(jnp.where(sel2, rank, 0.0), axis=-1, keepdims=True)

    info = jnp.zeros(logits.shape, F32)
    for slot, val in ((ROUTE_IDX1, i1.astype(F32)), (ROUTE_IDX2, i2.astype(F32)),
                      (ROUTE_W1, 1.0 / den), (ROUTE_W2, e2 / den),
                      (ROUTE_RANK1, rank1), (ROUTE_RANK2, rank2)):
        info = jnp.where(lane == slot, val, info)
    info_ref[...] = info
    carry_ref[0:1, :] = carry_ref[0:1, :] + jnp.sum(sel, axis=0, keepdims=True)
    cnt_ref[...] = jnp.broadcast_to(carry_ref[0:1, :], cnt_ref.shape)


def _router(x, wh, wl, br, tm):
    T, D = x.shape
    return pl.pallas_call(
        _router_kernel,
        grid=(T // tm,),
        in_specs=[pl.BlockSpec((tm, D), lambda i: (i, 0)),
                  pl.BlockSpec(wh.shape, lambda i: (0, 0)),
                  pl.BlockSpec(wl.shape, lambda i: (0, 0)),
                  pl.BlockSpec(br.shape, lambda i: (0, 0))],
        out_specs=(pl.BlockSpec((tm, LANES), lambda i: (i, 0)),
                   pl.BlockSpec((8, LANES), lambda i: (0, 0))),
        out_shape=(jax.ShapeDtypeStruct((T, LANES), F32),
                   jax.ShapeDtypeStruct((8, LANES), F32)),
        scratch_shapes=[pltpu.VMEM((8, LANES), F32)],
        compiler_params=_params("arbitrary"),
        name="router",
    )(x, wh, wl, br)


def _gather_rows(idx_vmem, idx_smem, src_hbm, dst_refs, n_rows, sem_idx, sem_rows):
    n_dst = len(dst_refs)
    stage = pltpu.make_async_copy(idx_vmem, idx_smem, sem_idx)
    stage.start()
    stage.wait()

    def row_copy(r, k, src_row):
        return pltpu.make_async_copy(src_hbm.at[pl.ds(src_row, 1), :],
                                     dst_refs[k].at[pl.ds(r, 1), :], sem_rows)

    def issue(r, carry):
        for k in range(n_dst):
            row_copy(r, k, idx_smem[0, n_dst * r + k]).start()
        return carry

    def drain(r, carry):
        for k in range(n_dst):
            row_copy(r, k, 0).wait()
        return carry

    lax.fori_loop(0, n_rows, issue, 0, unroll=8)
    return functools.partial(lax.fori_loop, 0, n_rows, drain, 0, unroll=8)


def _moe_expert_kernel(te_ref, nv_ref, src_ref, x_hbm, wg_ref, wu_ref, wd_ref, y_ref,
                       xg_ref, xb_ref, idx_ref, sem_idx, sem_rows):
    i = pl.program_id(0)
    j = pl.program_id(1)
    valid = i < nv_ref[0]

    @pl.when(j == 0)
    def _():
        y_ref[...] = jnp.zeros(y_ref.shape, F32)

    @pl.when(jnp.logical_and(valid, j == 0))
    def _():
        drain = _gather_rows(src_ref.at[0], idx_ref, x_hbm, [xg_ref], xg_ref.shape[0],
                             sem_idx, sem_rows)
        drain()
        xb_ref[...] = xg_ref[...].astype(BF16)

    @pl.when(valid)
    def _():
        xb = xb_ref[...]
        hid = _silu(_dot(xb, wg_ref[0])) * _dot(xb, wu_ref[0])
        y_ref[...] += _dot(hid.astype(BF16), wd_ref[0])


def _moe_experts(tile_expert, n_valid, src_rows, x, wg, wu, wd, tm, tf):
    T, D = x.shape
    E, _, F = wg.shape
    n_tiles = src_rows.shape[0]

    def w_map(i, j, te, nv):
        return (te[i], 0, jnp.where(i < nv[0], j, 0))

    def wd_map(i, j, te, nv):
        return (te[i], jnp.where(i < nv[0], j, 0), 0)

    grid_spec = pltpu.PrefetchScalarGridSpec(
        num_scalar_prefetch=2,
        grid=(n_tiles, F // tf),
        in_specs=[pl.BlockSpec((1, 1, tm), lambda i, j, te, nv: (i, 0, 0)),
                  pl.BlockSpec(memory_space=pl.ANY),
                  pl.BlockSpec((1, D, tf), w_map),
                  pl.BlockSpec((1, D, tf), w_map),
                  pl.BlockSpec((1, tf, D), wd_map)],
        out_specs=pl.BlockSpec((tm, D), lambda i, j, te, nv: (i, 0)),
        scratch_shapes=[pltpu.VMEM((tm, D), F32), pltpu.VMEM((tm, D), BF16),
                        pltpu.SMEM((1, tm), jnp.int32),
                        pltpu.SemaphoreType.DMA(()), pltpu.SemaphoreType.DMA(())],
    )
    return pl.pallas_call(
        _moe_expert_kernel,
        grid_spec=grid_spec,
        out_shape=jax.ShapeDtypeStruct((n_tiles * tm, D), F32),
        compiler_params=_params("arbitrary", "arbitrary"),
        name="moe_experts",
    )(tile_expert, n_valid, src_rows, x, wg, wu, wd)


def _moe_combine_kernel(dst_ref, x_ref, info_ref, y_hbm, lnp_ref, p_ref, wpg_ref, wpp_ref,
                        o_ref, y1_ref, y2_ref, idx_ref, sem_idx, sem_rows):
    tm = x_ref.shape[0]
    drain = _gather_rows(dst_ref.at[0], idx_ref, y_hbm, [y1_ref, y2_ref], tm, sem_idx, sem_rows)
    info = info_ref[...]
    w1 = info[:, ROUTE_W1:ROUTE_W1 + 1]
    w2 = info[:, ROUTE_W2:ROUTE_W2 + 1]
    drain()
    mix = w1 * y1_ref[...] + w2 * y2_ref[...]
    _ffn_epilogue(x_ref[...], mix, lnp_ref, p_ref, wpg_ref, wpp_ref, o_ref)


def _moe_combine(dst_rows, x, info, y, lnp, p, wpg, wpp, tm):
    T, D = x.shape
    return pl.pallas_call(
        _moe_combine_kernel,
        grid=(T // tm,),
        in_specs=[pl.BlockSpec((1, 1, 2 * tm), lambda i: (i, 0, 0)),
                  pl.BlockSpec((tm, D), lambda i: (i, 0)),
                  pl.BlockSpec((tm, LANES), lambda i: (i, 0)),
                  pl.BlockSpec(memory_space=pl.ANY),
                  pl.BlockSpec(lnp.shape, lambda i: (0, 0)),
                  pl.BlockSpec((tm, PLE_DIM), lambda i: (i, 0)),
                  pl.BlockSpec(wpg.shape, lambda i: (0, 0)),
                  pl.BlockSpec(wpp.shape, lambda i: (0, 0))],
        out_specs=pl.BlockSpec((tm, D), lambda i: (i, 0)),
        out_shape=jax.ShapeDtypeStruct((T, D), F32),
        scratch_shapes=[pltpu.VMEM((tm, D), F32), pltpu.VMEM((tm, D), F32),
                        pltpu.SMEM((1, 2 * tm), jnp.int32),
                        pltpu.SemaphoreType.DMA(()), pltpu.SemaphoreType.DMA(())],
        compiler_params=_params("arbitrary"),
        name="moe_combine",
    )(dst_rows, x, info, y, lnp, p, wpg, wpp)


def _route_plan(info, counts, tm):
    T = info.shape[0]
    n_tiles = (2 * T) // tm + N_EXPERTS
    as_int = lambda col: info[:, col].astype(jnp.int32)
    idx1, idx2 = as_int(ROUTE_IDX1), as_int(ROUTE_IDX2)
    counts = counts[0, :N_EXPERTS].astype(jnp.int32)
    padded = ((counts + tm - 1) // tm) * tm
    ends = jnp.cumsum(padded)
    starts = ends - padded
    dst = jnp.stack([starts[idx1] + as_int(ROUTE_RANK1), starts[idx2] + as_int(ROUTE_RANK2)], axis=1)
    tok = jnp.broadcast_to(jnp.arange(T, dtype=jnp.int32)[:, None], (T, 2))
    src = jnp.zeros((n_tiles * tm,), jnp.int32).at[dst.reshape(-1)].set(tok.reshape(-1))
    tile_start = jnp.arange(n_tiles, dtype=jnp.int32) * tm
    tile_expert = jnp.minimum(jnp.sum(tile_start[:, None] >= ends[None, :], axis=1),
                              N_EXPERTS - 1).astype(jnp.int32)
    n_valid = (ends[-1] // tm).astype(jnp.int32).reshape(1)
    return dst, src.reshape(n_tiles, 1, tm), tile_expert, n_valid


def _c_front_kernel(x_ref, wm_ref, ws_ref, gb_ref, q_ref, k_ref, v_ref, op_ref, sc_ref, sct_ref):
    xb = x_ref[0].astype(BF16)
    q_ref[0] = _dot(xb, wm_ref[:, 0:C_QK]) * (C_QK_HEAD ** -0.5)
    k_ref[0] = _dot(xb, wm_ref[:, C_QK:2 * C_QK])
    v_ref[0] = _dot(xb, wm_ref[:, 2 * C_QK:2 * C_QK + C_V])
    op_ref[0] = _dot(xb, wm_ref[:, 2 * C_QK + C_V:2 * C_QK + 2 * C_V])
    hs = _dot(xb, ws_ref[...]) + gb_ref[0:1, :]
    capped = GATE_CAP * jnp.tanh(hs / GATE_CAP)
    lane = lax.broadcasted_iota(jnp.int32, hs.shape, 1)
    log_f = jnp.minimum(capped, 0.0) - jnp.log1p(jnp.exp(-jnp.abs(capped)))
    sc = jnp.where(lane < N_HEADS, capped, _chunk_cumsum(log_f))
    sc_ref[0] = sc
    sct_ref[0] = jnp.transpose(sc)[0:8, :]


def _c_front(x, wm, ws, gb, ts):
    B, S, D = x.shape
    tok = lambda w: pl.BlockSpec((1, ts, w), lambda b, s: (b, s, 0))
    full = lambda a: pl.BlockSpec(a.shape, lambda b, s: (0,) * a.ndim)
    out_shape = (
        jax.ShapeDtypeStruct((B, S, C_QK), F32),
        jax.ShapeDtypeStruct((B, S, C_QK), F32),
        jax.ShapeDtypeStruct((B, S, C_V), F32),
        jax.ShapeDtypeStruct((B, S, C_V), F32),
        jax.ShapeDtypeStruct((B, S, LANES), F32),
        jax.ShapeDtypeStruct((B, 8, S), F32),
    )
    out_specs = (tok(C_QK), tok(C_QK), tok(C_V), tok(C_V), tok(LANES),
                 pl.BlockSpec((1, 8, ts), lambda b, s: (b, 0, s)))
    return pl.pallas_call(
        _c_front_kernel,
        grid=(B, S // ts),
        in_specs=[tok(D), full(wm), full(ws), full(gb)],
        out_specs=out_specs,
        out_shape=out_shape,
        compiler_params=_params("arbitrary", "arbitrary"),
        name="c_front",
    )(x, wm, ws, gb)


def _mlstm_kernel(q_ref, k_ref, v_ref, op_ref, sc_ref, sct_ref, ng_ref, o_ref,
                  c_ref, n_ref, m_ref):
    s = pl.program_id(1)
    ts = q_ref.shape[1]

    @pl.when(s == 0)
    def _():
        c_ref[...] = jnp.zeros(c_ref.shape, F32)
        n_ref[...] = jnp.zeros(n_ref.shape, F32)
        m_ref[...] = jnp.zeros(m_ref.shape, F32)

    row = lax.broadcasted_iota(jnp.int32, (CHUNK, CHUNK), 0)
    col = lax.broadcasted_iota(jnp.int32, (CHUNK, CHUNK), 1)
    causal = row >= col
    neg = jnp.float32(-jnp.inf)

    n_chunks = ts // CHUNK
    pairs = [(c, h) for c in range(n_chunks) for h in range(N_HEADS)]
    qs, qbs, ks, vbs, bcs, b_lasts, m_intras, g_kvs, m_kvs, dens, pmats = ([] for _ in range(11))
    for c, h in pairs:
        r0, r1 = c * CHUNK, (c + 1) * CHUNK
        klo, khi = h * C_QK_HEAD, (h + 1) * C_QK_HEAD
        vlo, vhi = h * C_V_HEAD, (h + 1) * C_V_HEAD
        qh = q_ref[0, r0:r1, klo:khi]
        kh = k_ref[0, r0:r1, klo:khi]
        li = sc_ref[0, r0:r1, h:h + 1]
        bc = sc_ref[0, r0:r1, N_HEADS + h:N_HEADS + h + 1]
        li_row = sct_ref[0, h:h + 1, r0:r1]
        b_row = sct_ref[0, N_HEADS + h:N_HEADS + h + 1, r0:r1]
        b_last = bc[CHUNK - 1:CHUNK, :]
        dmat = jnp.where(causal, bc - b_row + li_row, neg)
        m_intra = jnp.max(dmat, axis=-1, keepdims=True)
        qb = qh.astype(BF16)
        pmat = jnp.exp(dmat - m_intra) * _dot_nt(qb, kh.astype(BF16))
        g_kv = b_last - bc + li
        qs.append(qh)
        qbs.append(qb)
        ks.append(kh)
        vbs.append(v_ref[0, r0:r1, vlo:vhi].astype(BF16))
        bcs.append(bc)
        b_lasts.append(b_last)
        m_intras.append(m_intra)
        g_kvs.append(g_kv)
        m_kvs.append(jnp.max(g_kv, axis=0, keepdims=True))
        dens.append(jnp.sum(pmat, axis=-1, keepdims=True))
        pmats.append(pmat.astype(BF16))
    num_intra = [_dot(pmats[i], vbs[i]) for i in range(len(pairs))]

    m_run = [m_ref[h, :, 0:1] for h in range(N_HEADS)]
    m_sts, kws, decs = [], [], []
    for i, (c, h) in enumerate(pairs):
        m_sts.append(m_run[h])
        m_new = jnp.maximum(b_lasts[i] + m_run[h], m_kvs[i])
        kws.append(ks[i] * jnp.exp(g_kvs[i] - m_new))
        decs.append(jnp.exp(b_lasts[i] + m_run[h] - m_new))
        m_run[h] = m_new
    kv = [_dot_tn(kws[i].astype(BF16), vbs[i]) for i in range(len(pairs))]
    c_run = [c_ref[h] for h in range(N_HEADS)]
    n_run = [n_ref[h] for h in range(N_HEADS)]
    c_sts, n_sts = [], []
    for i, (c, h) in enumerate(pairs):
        c_sts.append(c_run[h].astype(BF16))
        n_sts.append(n_run[h])
        c_run[h] = decs[i] * c_run[h] + kv[i]
        n_run[h] = decs[i] * n_run[h] + jnp.sum(kws[i], axis=0, keepdims=True)
    for h in range(N_HEADS):
        c_ref[h] = c_run[h]
        n_ref[h] = n_run[h]
        m_ref[h] = jnp.broadcast_to(m_run[h], (1, LANES))

    q_c = [_dot(qbs[i], c_sts[i]) for i in range(len(pairs))]
    for i, (c, h) in enumerate(pairs):
        r0, r1 = c * CHUNK, (c + 1) * CHUNK
        vlo, vhi = h * C_V_HEAD, (h + 1) * C_V_HEAD
        inter = bcs[i] + m_sts[i]
        m_t = jnp.maximum(inter, m_intras[i])
        s_inter = jnp.exp(inter - m_t)
        s_intra = jnp.exp(m_intras[i] - m_t)
        num = s_inter * q_c[i] + s_intra * num_intra[i]
        den = (s_inter * jnp.sum(qs[i] * n_sts[i], axis=-1, keepdims=True) + s_intra * dens[i])
        hh = num / jnp.maximum(jnp.abs(den), jnp.exp(-m_t))
        hh = hh * lax.rsqrt(jnp.mean(hh * hh, axis=-1, keepdims=True) + 1e-6)
        hh = hh * ng_ref[0:1, vlo:vhi] * _sigmoid(op_ref[0, r0:r1, vlo:vhi])
        o_ref[0, r0:r1, vlo:vhi] = hh.astype(BF16)


def _mlstm(q, k, v, op, sc, sct, ng, ts):
    B, S, _ = q.shape
    tok = lambda w: pl.BlockSpec((1, ts, w), lambda b, s: (b, s, 0))
    return pl.pallas_call(
        _mlstm_kernel,
        grid=(B, S // ts),
        in_specs=[tok(C_QK), tok(C_QK), tok(C_V), tok(C_V), tok(LANES),
                  pl.BlockSpec((1, 8, ts), lambda b, s: (b, 0, s)),
                  pl.BlockSpec(ng.shape, lambda b, s: (0, 0))],
        out_specs=tok(C_V),
        out_shape=jax.ShapeDtypeStruct((B, S, C_V), BF16),
        scratch_shapes=[pltpu.VMEM((N_HEADS, C_QK_HEAD, C_V_HEAD), F32),
                        pltpu.VMEM((N_HEADS, 1, C_QK_HEAD), F32),
                        pltpu.VMEM((N_HEADS, 1, LANES), F32)],
        compiler_params=_params("arbitrary", "arbitrary"),
        name="mlstm_scan",
    )(q, k, v, op, sc, sct, ng)


def _pad_lanes(a, offset=0):
    return jnp.pad(a, ((0, 0), (offset, LANES - offset - a.shape[1])))


def kernel(x, p, ab_w_in, ab_conv_qkv, ab_a_log, ab_dt_bias, ab_o_norm_g, ab_dw_w, ab_dw_b,
           ab_cn_g, ab_cn_b, ab_w_out, ffn_w_gate, ffn_w_up, ffn_w_down, c_w_in, c_b_i, c_b_f,
           c_norm_g, c_w_out, moe_w_router, moe_b_router, moe_w_gate, moe_w_up, moe_w_down,
           ln_mix_g, ln_mix_b, ln_ffn_g, ln_ffn_b, ple_w_proj, ple_w_gate):
    B, S, D = x.shape
    T = B * S
    ts_front, ts_scan, tm, tm_big, tf = 512, 256, 512, 1024, 512

    w_in = ab_w_in[0]
    n_main = 4 * A_DIM
    wm = jnp.concatenate([w_in[:, :n_main], w_in[:, n_main + 2 * N_HEADS:]], axis=1).astype(BF16)
    ws = _pad_lanes(w_in[:, n_main:n_main + 2 * N_HEADS]).astype(BF16)
    gp = jnp.concatenate([_pad_lanes(ab_a_log[0][None, :], N_HEADS),
                          _pad_lanes(ab_dt_bias[0][None, :], N_HEADS)], axis=0)
    dwp = jnp.stack([ab_dw_b[0], ab_cn_g[0], ab_cn_b[0]], axis=0)
    q, k, v, z, sc, sct, u = _ab_front(x, wm, ws, ab_conv_qkv[0], gp, ab_dw_w[0], dwp, ts_front)
    o_a = _gdn(q, k, v, z, sc, sct, ab_o_norm_g[0][None, :], ts_scan)
    w_out = ab_w_out[0].astype(BF16)
    lnp = jnp.stack([ln_mix_g[0], ln_mix_b[0]], axis=0)
    xf = x.reshape(T, D)
    x1 = _proj_ln([o_a.reshape(T, A_DIM), u.reshape(T, B_CH)], [w_out[:A_DIM], w_out[A_DIM:]],
                  xf, lnp, tm)
    lnp = jnp.stack([ln_ffn_g[0], ln_ffn_b[0]], axis=0)
    x2 = _ffn(x1, ffn_w_gate[0].astype(BF16), ffn_w_up[0].astype(BF16),
              ffn_w_down[0].astype(BF16), lnp, p[0].reshape(T, PLE_DIM),
              ple_w_gate[0].astype(BF16), ple_w_proj[0].astype(BF16), tm_big, tf)

    w_in = c_w_in[0]
    n_main = 2 * C_QK + 2 * C_V
    wm = w_in[:, :n_main].astype(BF16)
    ws = _pad_lanes(w_in[:, n_main:]).astype(BF16)
    gb = _pad_lanes(jnp.concatenate([c_b_i[0], c_b_f[0]])[None, :])
    q, k, v, op, sc, sct = _c_front(x2.reshape(B, S, D), wm, ws, gb, ts_front)
    hh = _mlstm(q, k, v, op, sc, sct, c_norm_g[0][None, :], ts_scan)
    lnp = jnp.stack([ln_mix_g[1], ln_mix_b[1]], axis=0)
    x3 = _proj_ln([hh.reshape(T, C_V)], [c_w_out[0].astype(BF16)], x2, lnp, tm)

    wr = _pad_lanes(moe_w_router[0])
    wr_hi = wr.astype(BF16)
    wr_lo = (wr - wr_hi.astype(F32)).astype(BF16)
    info, counts = _router(x3, wr_hi, wr_lo, _pad_lanes(moe_b_router[0][None, :]), tm)
    dst, src, tile_expert, n_valid = _route_plan(info, counts, tm_big)
    y = _moe_experts(tile_expert, n_valid, src, x3, moe_w_gate[0].astype(BF16),
                     moe_w_up[0].astype(BF16), moe_w_down[0].astype(BF16), tm_big, tf)
    lnp = jnp.stack([ln_ffn_g[1], ln_ffn_b[1]], axis=0)
    x4 = _moe_combine(dst.reshape(T // tm, 1, 2 * tm), x3, info, y, lnp, p[1].reshape(T, PLE_DIM),
                      ple_w_gate[1].astype(BF16), ple_w_proj[1].astype(BF16), tm)
    return x4.reshape(B, S, D)
```

```python
import functools

import jax
import jax.numpy as jnp
from jax import lax
from jax.experimental import pallas as pl
from jax.experimental.pallas import tpu as pltpu

F32 = jnp.float32
BF16 = jnp.bfloat16

D_MODEL = 1024
N_HEADS = 4
CHUNK = 64
A_HEAD = 128
A_DIM = N_HEADS * A_HEAD
SHORT_CONV = 4
B_CH = 512
B_CONV = 31
C_QK_HEAD = 128
C_V_HEAD = 256
C_QK = N_HEADS * C_QK_HEAD
C_V = N_HEADS * C_V_HEAD
GATE_CAP = 15.0
D_FF = 3584
N_EXPERTS = 8
PLE_DIM = 256
DEPTH = 2
DN_ALPHA = (2 * DEPTH) ** 0.25
LANES = 128
SUBLANES = 8
QKV_HALO = 8
DW_HALO = 32
VMEM_LIMIT = 56 * 1024 * 1024


def _dot(a, b):
    return jnp.dot(a, b, preferred_element_type=F32)


def _dot_nt(a, b):
    return lax.dot_general(a, b, (((1,), (1,)), ((), ())), preferred_element_type=F32)


def _dot_tn(a, b):
    return lax.dot_general(a, b, (((0,), (0,)), ((), ())), preferred_element_type=F32)


def _sigmoid(t):
    return 1.0 / (1.0 + jnp.exp(-t))


def _silu(t):
    return t * _sigmoid(t)


def _softplus(t):
    return jnp.maximum(t, 0.0) + jnp.log1p(jnp.exp(-jnp.abs(t)))


def _layer_norm(t, g, b):
    mu = jnp.mean(t, axis=-1, keepdims=True)
    c = t - mu
    var = jnp.mean(c * c, axis=-1, keepdims=True)
    return c * lax.rsqrt(var + 1e-5) * g + b


def _chunk_cumsum(val):
    row = lax.broadcasted_iota(jnp.int32, val.shape, 0) & (CHUNK - 1)
    sh = 1
    while sh < CHUNK:
        val = val + jnp.where(row >= sh, pltpu.roll(val, sh, axis=0), 0.0)
        sh *= 2
    return val


def _params(*sem):
    return pltpu.CompilerParams(dimension_semantics=sem, vmem_limit_bytes=VMEM_LIMIT)


def _ab_front_kernel(x_ref, wm_ref, ws_ref, cw_ref, gp_ref, dww_ref, dwp_ref,
                     q_ref, k_ref, v_ref, z_ref, sc_ref, sct_ref, u_ref,
                     qkv_ext, u_ext, u_shift):
    s = pl.program_id(1)
    ts = x_ref.shape[1]
    xb = x_ref[0].astype(BF16)

    @pl.when(s == 0)
    def _():
        qkv_ext[0:QKV_HALO, :] = jnp.zeros((QKV_HALO, 3 * A_DIM), F32)
        u_ext[0:DW_HALO, :] = jnp.zeros((DW_HALO, B_CH), F32)

    @pl.when(s > 0)
    def _():
        qkv_ext[0:QKV_HALO, :] = qkv_ext[ts:ts + QKV_HALO, :]
        u_ext[0:DW_HALO, :] = u_ext[ts:ts + DW_HALO, :]

    qkv_ext[QKV_HALO:QKV_HALO + ts, :] = _dot(xb, wm_ref[:, 0:3 * A_DIM])
    acc = None
    for j in range(SHORT_CONV):
        off = QKV_HALO - (SHORT_CONV - 1) + j
        term = cw_ref[j:j + 1, :] * qkv_ext[off:off + ts, :]
        acc = term if acc is None else acc + term
    qkv = _silu(acc)
    for h in range(N_HEADS):
        lo, hi = h * A_HEAD, (h + 1) * A_HEAD
        qh = qkv[:, lo:hi]
        kh = qkv[:, A_DIM + lo:A_DIM + hi]
        qn = lax.rsqrt(jnp.sum(qh * qh, axis=-1, keepdims=True) + 1e-6)
        kn = lax.rsqrt(jnp.sum(kh * kh, axis=-1, keepdims=True) + 1e-6)
        q_ref[0, :, lo:hi] = qh * qn * (A_HEAD ** -0.5)
        k_ref[0, :, lo:hi] = kh * kn
    v_ref[0] = qkv[:, 2 * A_DIM:3 * A_DIM]
    z_ref[0] = _dot(xb, wm_ref[:, 3 * A_DIM:4 * A_DIM])

    hs = _dot(xb, ws_ref[...])
    lane = lax.broadcasted_iota(jnp.int32, hs.shape, 1)
    beta = _sigmoid(hs)
    g = -jnp.exp(gp_ref[0:1, :]) * _softplus(hs + gp_ref[1:2, :])
    sc = jnp.where(lane < N_HEADS, beta, _chunk_cumsum(g))
    sc_ref[0] = sc
    sct_ref[0] = jnp.transpose(sc)[0:8, :]

    glu = _dot(xb, wm_ref[:, 4 * A_DIM:4 * A_DIM + 2 * B_CH])
    u_ext[DW_HALO:DW_HALO + ts, :] = glu[:, 0:B_CH] * _sigmoid(glu[:, B_CH:2 * B_CH])
    first_off = DW_HALO - (B_CONV - 1)
    span = ts + DW_HALO - 8
    acc = None
    for res in range(8):
        offs = [o for o in range(first_off, DW_HALO + 1) if o % 8 == res]
        if res:
            u_shift[...] = u_ext[res:res + span, :]
        src = u_shift if res else u_ext
        for off in offs:
            lo = off - res if res else off
            term = dww_ref[off - first_off:off - first_off + 1, :] * src[lo:lo + ts, :]
            acc = term if acc is None else acc + term
    u = acc + dwp_ref[0:1, :]
    u = _layer_norm(u, dwp_ref[1:2, :], dwp_ref[2:3, :])
    u_ref[0] = _silu(u).astype(BF16)


def _ab_front(x, wm, ws, cw, gp, dww, dwp, ts):
    B, S, D = x.shape
    grid = (B, S // ts)
    tok = lambda w: pl.BlockSpec((1, ts, w), lambda b, s: (b, s, 0))
    full = lambda a: pl.BlockSpec(a.shape, lambda b, s: (0,) * a.ndim)
    out_shape = (
        jax.ShapeDtypeStruct((B, S, A_DIM), F32),
        jax.ShapeDtypeStruct((B, S, A_DIM), F32),
        jax.ShapeDtypeStruct((B, S, A_DIM), F32),
        jax.ShapeDtypeStruct((B, S, A_DIM), F32),
        jax.ShapeDtypeStruct((B, S, LANES), F32),
        jax.ShapeDtypeStruct((B, 8, S), F32),
        jax.ShapeDtypeStruct((B, S, B_CH), BF16),
    )
    out_specs = (tok(A_DIM), tok(A_DIM), tok(A_DIM), tok(A_DIM), tok(LANES),
                 pl.BlockSpec((1, 8, ts), lambda b, s: (b, 0, s)), tok(B_CH))
    return pl.pallas_call(
        _ab_front_kernel,
        grid=grid,
        in_specs=[tok(D), full(wm), full(ws), full(cw), full(gp), full(dww), full(dwp)],
        out_specs=out_specs,
        out_shape=out_shape,
        scratch_shapes=[pltpu.VMEM((ts + QKV_HALO, 3 * A_DIM), F32),
                        pltpu.VMEM((ts + DW_HALO, B_CH), F32),
                        pltpu.VMEM((ts + DW_HALO - 8, B_CH), F32)],
        compiler_params=_params("arbitrary", "arbitrary"),
        name="ab_front",
    )(x, wm, ws, cw, gp, dww, dwp)


def _gdn_kernel(q_ref, k_ref, v_ref, z_ref, sc_ref, sct_ref, ng_ref, o_ref, st_ref):
    s = pl.program_id(1)
    ts = q_ref.shape[1]

    @pl.when(s == 0)
    def _():
        st_ref[...] = jnp.zeros(st_ref.shape, F32)

    row = lax.broadcasted_iota(jnp.int32, (CHUNK, CHUNK), 0)
    col = lax.broadcasted_iota(jnp.int32, (CHUNK, CHUNK), 1)
    causal = row >= col
    strict = row > col
    eye = jnp.where(row == col, 1.0, 0.0).astype(F32)

    n_chunks = ts // CHUNK
    pairs = [(c, h) for c in range(n_chunks) for h in range(N_HEADS)]
    pw, tinv, attn, rhs, k_dec, q_dec, d_last = [], [], [], [], [], [], []
    for c, h in pairs:
        r0, r1 = c * CHUNK, (c + 1) * CHUNK
        lo, hi = h * A_HEAD, (h + 1) * A_HEAD
        qh = q_ref[0, r0:r1, lo:hi]
        kh = k_ref[0, r0:r1, lo:hi]
        vh = v_ref[0, r0:r1, lo:hi]
        beta = sc_ref[0, r0:r1, h:h + 1]
        gc = sc_ref[0, r0:r1, N_HEADS + h:N_HEADS + h + 1]
        gc_row = sct_ref[0, N_HEADS + h:N_HEADS + h + 1, r0:r1]
        gc_last = gc[CHUNK - 1:CHUNK, :]
        decay = jnp.where(causal, jnp.exp(jnp.where(causal, gc - gc_row, 0.0)), 0.0)
        kb = kh * beta
        egc = jnp.exp(gc)
        kq = _dot_nt(jnp.concatenate([kb, qh], axis=0).astype(BF16), kh.astype(BF16))
        a_neg = jnp.where(strict, -kq[0:CHUNK] * decay, 0.0)
        pw.append(a_neg)
        tinv.append(eye + a_neg)
        attn.append((kq[CHUNK:2 * CHUNK] * decay).astype(BF16))
        rhs.append(jnp.concatenate([vh * beta, kb * egc], axis=1).astype(BF16))
        k_dec.append((kh * jnp.exp(gc_last - gc)).astype(BF16))
        q_dec.append(qh * egc)
        d_last.append(jnp.exp(gc_last))
    for _ in range(5):
        for i in range(len(pairs)):
            pwb = pw[i].astype(BF16)
            pw[i] = _dot(pwb, pwb)
        for i in range(len(pairs)):
            tinv[i] = tinv[i] + _dot(tinv[i].astype(BF16), pw[i].astype(BF16))
    uw = [_dot(tinv[i].astype(BF16), rhs[i]).astype(BF16) for i in range(len(pairs))]
    auw = [_dot(attn[i], uw[i]) for i in range(len(pairs))]
    kuw = [_dot_tn(k_dec[i], uw[i]) for i in range(len(pairs))]
    st = [st_ref[h] for h in range(N_HEADS)]
    for i, (c, h) in enumerate(pairs):
        r0, r1 = c * CHUNK, (c + 1) * CHUNK
        lo, hi = h * A_HEAD, (h + 1) * A_HEAD
        stb = st[h].astype(BF16)
        q_eff = (q_dec[i] - auw[i][:, A_HEAD:]).astype(BF16)
        o = _dot(q_eff, stb) + auw[i][:, 0:A_HEAD]
        st[h] = (d_last[i] * st[h] + kuw[i][:, 0:A_HEAD]
                 - _dot(kuw[i][:, A_HEAD:].astype(BF16), stb))
        o = o * lax.rsqrt(jnp.mean(o * o, axis=-1, keepdims=True) + 1e-6) * ng_ref[0:1, :]
        o_ref[0, r0:r1, lo:hi] = (o * _silu(z_ref[0, r0:r1, lo:hi])).astype(BF16)
    for h in range(N_HEADS):
        st_ref[h] = st[h]


def _gdn(q, k, v, z, sc, sct, ng, ts):
    B, S, _ = q.shape
    tok = lambda w: pl.BlockSpec((1, ts, w), lambda b, s: (b, s, 0))
    return pl.pallas_call(
        _gdn_kernel,
        grid=(B, S // ts),
        in_specs=[tok(A_DIM), tok(A_DIM), tok(A_DIM), tok(A_DIM), tok(LANES),
                  pl.BlockSpec((1, 8, ts), lambda b, s: (b, 0, s)),
                  pl.BlockSpec(ng.shape, lambda b, s: (0, 0))],
        out_specs=tok(A_DIM),
        out_shape=jax.ShapeDtypeStruct((B, S, A_DIM), BF16),
        scratch_shapes=[pltpu.VMEM((N_HEADS, A_HEAD, A_HEAD), F32)],
        compiler_params=_params("arbitrary", "arbitrary"),
        name="gdn_scan",
    )(q, k, v, z, sc, sct, ng)


def _proj_ln_kernel(*refs, n_in):
    a_refs = refs[0:n_in]
    w_refs = refs[n_in:2 * n_in]
    x_ref, lnp_ref, o_ref = refs[2 * n_in:2 * n_in + 3]
    mix = None
    for a_ref, w_ref in zip(a_refs, w_refs):
        t = _dot(a_ref[...], w_ref[...])
        mix = t if mix is None else mix + t
    o_ref[...] = _layer_norm(DN_ALPHA * x_ref[...] + mix, lnp_ref[0:1, :], lnp_ref[1:2, :])


def _proj_ln(acts, weights, x, lnp, tm):
    T, D = x.shape
    n_in = len(acts)
    in_specs = ([pl.BlockSpec((tm, a.shape[1]), lambda i: (i, 0)) for a in acts]
                + [pl.BlockSpec(w.shape, lambda i: (0, 0)) for w in weights]
                + [pl.BlockSpec((tm, D), lambda i: (i, 0)),
                   pl.BlockSpec(lnp.shape, lambda i: (0, 0))])
    return pl.pallas_call(
        functools.partial(_proj_ln_kernel, n_in=n_in),
        grid=(T // tm,),
        in_specs=in_specs,
        out_specs=pl.BlockSpec((tm, D), lambda i: (i, 0)),
        out_shape=jax.ShapeDtypeStruct((T, D), F32),
        compiler_params=_params("arbitrary"),
        name="proj_ln",
    )(*acts, *weights, x, lnp)


def _ffn_epilogue(x, acc, lnp_ref, p_ref, wpg_ref, wpp_ref, o_ref):
    y = _layer_norm(DN_ALPHA * x + acc, lnp_ref[0:1, :], lnp_ref[1:2, :])
    gate = _sigmoid(_dot(y.astype(BF16), wpg_ref[...]))
    o_ref[...] = y + gate * _dot(p_ref[...].astype(BF16), wpp_ref[...])


def _ffn_kernel(x_ref, wg_ref, wu_ref, wd_ref, lnp_ref, p_ref, wpg_ref, wpp_ref,
                o_ref, xb_ref):
    j = pl.program_id(1)

    @pl.when(j == 0)
    def _():
        xb_ref[...] = x_ref[...].astype(BF16)
        o_ref[...] = jnp.zeros(o_ref.shape, F32)

    xb = xb_ref[...]
    hid = _silu(_dot(xb, wg_ref[...])) * _dot(xb, wu_ref[...])
    o_ref[...] += _dot(hid.astype(BF16), wd_ref[...])

    @pl.when(j == pl.num_programs(1) - 1)
    def _():
        _ffn_epilogue(x_ref[...], o_ref[...], lnp_ref, p_ref, wpg_ref, wpp_ref, o_ref)


def _ffn(x, wg, wu, wd, lnp, p, wpg, wpp, tm, tf):
    T, D = x.shape
    F = wg.shape[1]
    return pl.pallas_call(
        _ffn_kernel,
        grid=(T // tm, F // tf),
        in_specs=[pl.BlockSpec((tm, D), lambda i, j: (i, 0)),
                  pl.BlockSpec((D, tf), lambda i, j: (0, j)),
                  pl.BlockSpec((D, tf), lambda i, j: (0, j)),
                  pl.BlockSpec((tf, D), lambda i, j: (j, 0)),
                  pl.BlockSpec(lnp.shape, lambda i, j: (0, 0)),
                  pl.BlockSpec((tm, PLE_DIM), lambda i, j: (i, 0)),
                  pl.BlockSpec(wpg.shape, lambda i, j: (0, 0)),
                  pl.BlockSpec(wpp.shape, lambda i, j: (0, 0))],
        out_specs=pl.BlockSpec((tm, D), lambda i, j: (i, 0)),
        out_shape=jax.ShapeDtypeStruct((T, D), F32),
        scratch_shapes=[pltpu.VMEM((tm, D), BF16)],
        compiler_params=_params("arbitrary", "arbitrary"),
        name="ffn",
    )(x, wg, wu, wd, lnp, p, wpg, wpp)


ROUTE_IDX1, ROUTE_IDX2, ROUTE_W1, ROUTE_W2, ROUTE_RANK1, ROUTE_RANK2 = range(6)


def _router_kernel(x_ref, wh_ref, wl_ref, br_ref, info_ref, cnt_ref, carry_ref):
    i = pl.program_id(0)

    @pl.when(i == 0)
    def _():
        carry_ref[...] = jnp.zeros(carry_ref.shape, F32)

    x = x_ref[...]
    tm = x.shape[0]
    xh = x.astype(BF16)
    xl = (x - xh.astype(F32)).astype(BF16)
    logits = (_dot(xh, wh_ref[...]) + _dot(xl, wh_ref[...]) + _dot(xh, wl_ref[...])
              + br_ref[0:1, :])
    lane = lax.broadcasted_iota(jnp.int32, logits.shape, 1)
    neg = jnp.float32(-jnp.inf)
    logits = jnp.where(lane < N_EXPERTS, logits, neg)
    m1 = jnp.max(logits, axis=-1, keepdims=True)
    i1 = jnp.min(jnp.where(logits == m1, lane, LANES), axis=-1, keepdims=True)
    sel1 = lane == i1
    rest = jnp.where(sel1, neg, logits)
    m2 = jnp.max(rest, axis=-1, keepdims=True)
    i2 = jnp.min(jnp.where(rest == m2, lane, LANES), axis=-1, keepdims=True)
    sel2 = lane == i2
    e2 = jnp.exp(m2 - m1)
    den = 1.0 + e2

    sel = jnp.where(jnp.logical_or(sel1, sel2), 1.0, 0.0)
    row = lax.broadcasted_iota(jnp.int32, (tm, tm), 0)
    col = lax.broadcasted_iota(jnp.int32, (tm, tm), 1)
    earlier = jnp.where(row > col, 1.0, 0.0).astype(BF16)
    rank = _dot(earlier, sel.astype(BF16)) + carry_ref[0:1, :]
    rank1 = jnp.sum(jnp.where(sel1, rank, 0.0), axis=-1, keepdims=True)
    rank2 = jnp.sum(jnp.where(sel2, rank, 0.0), axis=-1, keepdims=True)

    info = jnp.zeros(logits.shape, F32)
    for slot, val in ((ROUTE_IDX1, i1.astype(F32)), (ROUTE_IDX2, i2.astype(F32)),
                      (ROUTE_W1, 1.0 / den), (ROUTE_W2, e2 / den),
                      (ROUTE_RANK1, rank1), (ROUTE_RANK2, rank2)):
        info = jnp.where(lane == slot, val, info)
    info_ref[...] = info
    carry_ref[0:1, :] = carry_ref[0:1, :] + jnp.sum(sel, axis=0, keepdims=True)
    cnt_ref[...] = jnp.broadcast_to(carry_ref[0:1, :], cnt_ref.shape)


def _router(x, wh, wl, br, tm):
    T, D = x.shape
    return pl.pallas_call(
        _router_kernel,
        grid=(T // tm,),
        in_specs=[pl.BlockSpec((tm, D), lambda i: (i, 0)),
                  pl.BlockSpec(wh.shape, lambda i: (0, 0)),
                  pl.BlockSpec(wl.shape, lambda i: (0, 0)),
                  pl.BlockSpec(br.shape, lambda i: (0, 0))],
        out_specs=(pl.BlockSpec((tm, LANES), lambda i: (i, 0)),
                   pl.BlockSpec((8, LANES), lambda i: (0, 0))),
        out_shape=(jax.ShapeDtypeStruct((T, LANES), F32),
                   jax.ShapeDtypeStruct((8, LANES), F32)),
        scratch_shapes=[pltpu.VMEM((8, LANES), F32)],
        compiler_params=_params("arbitrary"),
        name="router",
    )(x, wh, wl, br)


def _gather_rows(idx_vmem, idx_smem, src_hbm, dst_refs, n_rows, sem_idx, sem_rows):
    n_dst = len(dst_refs)
    stage = pltpu.make_async_copy(idx_vmem, idx_smem, sem_idx)
    stage.start()
    stage.wait()

    def row_copy(r, k, src_row):
        return pltpu.make_async_copy(src_hbm.at[pl.ds(src_row, 1), :],
                                     dst_refs[k].at[pl.ds(r, 1), :], sem_rows)

    def issue(g, carry):
        r0 = pl.multiple_of(g * SUBLANES, SUBLANES)
        for u in range(SUBLANES):
            for k in range(n_dst):
                row_copy(r0 + u, k, idx_smem[0, n_dst * (r0 + u) + k]).start()
        return carry

    def drain(g, carry):
        r0 = pl.multiple_of(g * SUBLANES, SUBLANES)
        for u in range(SUBLANES):
            for k in range(n_dst):
                row_copy(r0 + u, k, 0).wait()
        return carry

    lax.fori_loop(0, n_rows // SUBLANES, issue, 0)
    return functools.partial(lax.fori_loop, 0, n_rows // SUBLANES, drain, 0)


def _moe_expert_kernel(te_ref, nv_ref, src_ref, x_hbm, wg_ref, wu_ref, wd_ref, y_ref,
                       xg_ref, xb_ref, idx_ref, sem_idx, sem_rows):
    i = pl.program_id(0)
    j = pl.program_id(1)
    valid = i < nv_ref[0]

    @pl.when(j == 0)
    def _():
        y_ref[...] = jnp.zeros(y_ref.shape, F32)

    @pl.when(jnp.logical_and(valid, j == 0))
    def _():
        drain = _gather_rows(src_ref.at[0], idx_ref, x_hbm, [xg_ref], xg_ref.shape[0],
                             sem_idx, sem_rows)
        drain()
        xb_ref[...] = xg_ref[...].astype(BF16)

    @pl.when(valid)
    def _():
        xb = xb_ref[...]
        hid = _silu(_dot(xb, wg_ref[0])) * _dot(xb, wu_ref[0])
        y_ref[...] += _dot(hid.astype(BF16), wd_ref[0])


def _moe_experts(tile_expert, n_valid, src_rows, x, wg, wu, wd, tm, tf):
    T, D = x.shape
    E, _, F = wg.shape
    n_tiles = src_rows.shape[0]

    def w_map(i, j, te, nv):
        return (te[i], 0, jnp.where(i < nv[0], j, 0))

    def wd_map(i, j, te, nv):
        return (te[i], jnp.where(i < nv[0], j, 0), 0)

    grid_spec = pltpu.PrefetchScalarGridSpec(
        num_scalar_prefetch=2,
        grid=(n_tiles, F // tf),
        in_specs=[pl.BlockSpec((1, 1, tm), lambda i, j, te, nv: (i, 0, 0)),
                  pl.BlockSpec(memory_space=pl.ANY),
                  pl.BlockSpec((1, D, tf), w_map),
                  pl.BlockSpec((1, D, tf), w_map),
                  pl.BlockSpec((1, tf, D), wd_map)],
        out_specs=pl.BlockSpec((tm, D), lambda i, j, te, nv: (i, 0)),
        scratch_shapes=[pltpu.VMEM((tm, D), F32), pltpu.VMEM((tm, D), BF16),
                        pltpu.SMEM((1, tm), jnp.int32),
                        pltpu.SemaphoreType.DMA(()), pltpu.SemaphoreType.DMA(())],
    )
    return pl.pallas_call(
        _moe_expert_kernel,
        grid_spec=grid_spec,
        out_shape=jax.ShapeDtypeStruct((n_tiles * tm, D), F32),
        compiler_params=_params("arbitrary", "arbitrary"),
        name="moe_experts",
    )(tile_expert, n_valid, src_rows, x, wg, wu, wd)


def _moe_combine_kernel(dst_ref, x_ref, info_ref, y_hbm, lnp_ref, p_ref, wpg_ref, wpp_ref,
                        o_ref, y1_ref, y2_ref, idx_ref, sem_idx, sem_rows):
    tm = x_ref.shape[0]
    drain = _gather_rows(dst_ref.at[0], idx_ref, y_hbm, [y1_ref, y2_ref], tm, sem_idx, sem_rows)
    info = info_ref[...]
    w1 = info[:, ROUTE_W1:ROUTE_W1 + 1]
    w2 = info[:, ROUTE_W2:ROUTE_W2 + 1]
    drain()
    mix = w1 * y1_ref[...] + w2 * y2_ref[...]
    _ffn_epilogue(x_ref[...], mix, lnp_ref, p_ref, wpg_ref, wpp_ref, o_ref)


def _moe_combine(dst_rows, x, info, y, lnp, p, wpg, wpp, tm):
    T, D = x.shape
    return pl.pallas_call(
        _moe_combine_kernel,
        grid=(T // tm,),
        in_specs=[pl.BlockSpec((1, 1, 2 * tm), lambda i: (i, 0, 0)),
                  pl.BlockSpec((tm, D), lambda i: (i, 0)),
                  pl.BlockSpec((tm, LANES), lambda i: (i, 0)),
                  pl.BlockSpec(memory_space=pl.ANY),
                  pl.BlockSpec(lnp.shape, lambda i: (0, 0)),
                  pl.BlockSpec((tm, PLE_DIM), lambda i: (i, 0)),
                  pl.BlockSpec(wpg.shape, lambda i: (0, 0)),
                  pl.BlockSpec(wpp.shape, lambda i: (0, 0))],
        out_specs=pl.BlockSpec((tm, D), lambda i: (i, 0)),
        out_shape=jax.ShapeDtypeStruct((T, D), F32),
        scratch_shapes=[pltpu.VMEM((tm, D), F32), pltpu.VMEM((tm, D), F32),
                        pltpu.SMEM((1, 2 * tm), jnp.int32),
                        pltpu.SemaphoreType.DMA(()), pltpu.SemaphoreType.DMA(())],
        compiler_params=_params("arbitrary"),
        name="moe_combine",
    )(dst_rows, x, info, y, lnp, p, wpg, wpp)


def _route_plan(info, counts, tm):
    T = info.shape[0]
    n_tiles = (2 * T) // tm + N_EXPERTS
    as_int = lambda col: info[:, col].astype(jnp.int32)
    idx1, idx2 = as_int(ROUTE_IDX1), as_int(ROUTE_IDX2)
    counts = counts[0, :N_EXPERTS].astype(jnp.int32)
    padded = ((counts + tm - 1) // tm) * tm
    ends = jnp.cumsum(padded)
    starts = ends - padded
    dst = jnp.stack([starts[idx1] + as_int(ROUTE_RANK1), starts[idx2] + as_int(ROUTE_RANK2)], axis=1)
    tok = jnp.broadcast_to(jnp.arange(T, dtype=jnp.int32)[:, None], (T, 2))
    src = jnp.zeros((n_tiles * tm,), jnp.int32).at[dst.reshape(-1)].set(tok.reshape(-1))
    tile_start = jnp.arange(n_tiles, dtype=jnp.int32) * tm
    tile_expert = jnp.minimum(jnp.sum(tile_start[:, None] >= ends[None, :], axis=1),
                              N_EXPERTS - 1).astype(jnp.int32)
    n_valid = (ends[-1] // tm).astype(jnp.int32).reshape(1)
    return dst, src.reshape(n_tiles, 1, tm), tile_expert, n_valid


def _c_front_kernel(x_ref, wm_ref, ws_ref, gb_ref, q_ref, k_ref, v_ref, op_ref, sc_ref, sct_ref):
    xb = x_ref[0].astype(BF16)
    q_ref[0] = _dot(xb, wm_ref[:, 0:C_QK]) * (C_QK_HEAD ** -0.5)
    k_ref[0] = _dot(xb, wm_ref[:, C_QK:2 * C_QK])
    v_ref[0] = _dot(xb, wm_ref[:, 2 * C_QK:2 * C_QK + C_V])
    op_ref[0] = _dot(xb, wm_ref[:, 2 * C_QK + C_V:2 * C_QK + 2 * C_V])
    hs = _dot(xb, ws_ref[...]) + gb_ref[0:1, :]
    capped = GATE_CAP * jnp.tanh(hs / GATE_CAP)
    lane = lax.broadcasted_iota(jnp.int32, hs.shape, 1)
    log_f = jnp.minimum(capped, 0.0) - jnp.log1p(jnp.exp(-jnp.abs(capped)))
    sc = jnp.where(lane < N_HEADS, capped, _chunk_cumsum(log_f))
    sc_ref[0] = sc
    sct_ref[0] = jnp.transpose(sc)[0:8, :]


def _c_front(x, wm, ws, gb, ts):
    B, S, D = x.shape
    tok = lambda w: pl.BlockSpec((1, ts, w), lambda b, s: (b, s, 0))
    full = lambda a: pl.BlockSpec(a.shape, lambda b, s: (0,) * a.ndim)
    out_shape = (
        jax.ShapeDtypeStruct((B, S, C_QK), F32),
        jax.ShapeDtypeStruct((B, S, C_QK), F32),
        jax.ShapeDtypeStruct((B, S, C_V), F32),
        jax.ShapeDtypeStruct((B, S, C_V), F32),
        jax.ShapeDtypeStruct((B, S, LANES), F32),
        jax.ShapeDtypeStruct((B, 8, S), F32),
    )
    out_specs = (tok(C_QK), tok(C_QK), tok(C_V), tok(C_V), tok(LANES),
                 pl.BlockSpec((1, 8, ts), lambda b, s: (b, 0, s)))
    return pl.pallas_call(
        _c_front_kernel,
        grid=(B, S // ts),
        in_specs=[tok(D), full(wm), full(ws), full(gb)],
        out_specs=out_specs,
        out_shape=out_shape,
        compiler_params=_params("arbitrary", "arbitrary"),
        name="c_front",
    )(x, wm, ws, gb)


def _mlstm_kernel(q_ref, k_ref, v_ref, op_ref, sc_ref, sct_ref, ng_ref, o_ref,
                  c_ref, n_ref, m_ref):
    s = pl.program_id(1)
    ts = q_ref.shape[1]

    @pl.when(s == 0)
    def _():
        c_ref[...] = jnp.zeros(c_ref.shape, F32)
        n_ref[...] = jnp.zeros(n_ref.shape, F32)
        m_ref[...] = jnp.zeros(m_ref.shape, F32)

    row = lax.broadcasted_iota(jnp.int32, (CHUNK, CHUNK), 0)
    col = lax.broadcasted_iota(jnp.int32, (CHUNK, CHUNK), 1)
    causal = row >= col
    neg = jnp.float32(-jnp.inf)

    n_chunks = ts // CHUNK
    pairs = [(c, h) for c in range(n_chunks) for h in range(N_HEADS)]
    n = len(pairs)
    each = lambda f: [f(i) for i in range(n)]
    rows = lambda i: slice(pairs[i][0] * CHUNK, (pairs[i][0] + 1) * CHUNK)
    kcols = lambda i: slice(pairs[i][1] * C_QK_HEAD, (pairs[i][1] + 1) * C_QK_HEAD)
    vcols = lambda i: slice(pairs[i][1] * C_V_HEAD, (pairs[i][1] + 1) * C_V_HEAD)
    head = lambda i: pairs[i][1]

    qs = each(lambda i: q_ref[0, rows(i), kcols(i)])
    ks = each(lambda i: k_ref[0, rows(i), kcols(i)])
    qbs = each(lambda i: qs[i].astype(BF16))
    vbs = each(lambda i: v_ref[0, rows(i), vcols(i)].astype(BF16))
    qk = each(lambda i: _dot_nt(qbs[i], ks[i].astype(BF16)))
    lis = each(lambda i: sc_ref[0, rows(i), head(i):head(i) + 1])
    bcs = each(lambda i: sc_ref[0, rows(i), N_HEADS + head(i):N_HEADS + head(i) + 1])
    b_lasts = each(lambda i: bcs[i][CHUNK - 1:CHUNK, :])
    bc_wide = each(lambda i: jnp.broadcast_to(bcs[i], (CHUNK, CHUNK)))
    dmat = each(lambda i: jnp.where(
        causal,
        bc_wide[i] - sct_ref[0, N_HEADS + head(i):N_HEADS + head(i) + 1, rows(i)]
        + sct_ref[0, head(i):head(i) + 1, rows(i)], neg))
    m_intras = each(lambda i: jnp.max(dmat[i], axis=-1, keepdims=True))
    pmat = each(lambda i: jnp.exp(dmat[i] - m_intras[i]) * qk[i])
    dens = each(lambda i: jnp.sum(pmat[i], axis=-1, keepdims=True))
    num_intra = each(lambda i: _dot(pmat[i].astype(BF16), vbs[i]))
    g_kvs = each(lambda i: b_lasts[i] - bcs[i] + lis[i])
    m_kvs = each(lambda i: jnp.max(g_kvs[i], axis=0, keepdims=True))

    m_run = [m_ref[h, :, 0:1] for h in range(N_HEADS)]
    m_sts, m_news = [], []
    for i in range(n):
        m_sts.append(m_run[head(i)])
        m_run[head(i)] = jnp.maximum(b_lasts[i] + m_run[head(i)], m_kvs[i])
        m_news.append(m_run[head(i)])
    decs = each(lambda i: jnp.exp(b_lasts[i] + m_sts[i] - m_news[i]))
    k_scale = each(lambda i: jnp.exp(g_kvs[i] - m_news[i]))
    kws = each(lambda i: ks[i] * k_scale[i])
    kv = each(lambda i: _dot_tn(kws[i].astype(BF16), vbs[i]))
    k_sum = each(lambda i: jnp.sum(kws[i], axis=0, keepdims=True))
    dec_c = each(lambda i: jnp.broadcast_to(decs[i], (C_QK_HEAD, C_V_HEAD)))
    c_run = [c_ref[h] for h in range(N_HEADS)]
    n_run = [n_ref[h] for h in range(N_HEADS)]
    c_sts, n_sts = [], []
    for i in range(n):
        c_sts.append(c_run[head(i)].astype(BF16))
        n_sts.append(n_run[head(i)])
        c_run[head(i)] = dec_c[i] * c_run[head(i)] + kv[i]
        n_run[head(i)] = decs[i] * n_run[head(i)] + k_sum[i]
    for h in range(N_HEADS):
        c_ref[h] = c_run[h]
        n_ref[h] = n_run[h]
        m_ref[h] = jnp.broadcast_to(m_run[h], (1, LANES))

    q_c = each(lambda i: _dot(qbs[i], c_sts[i]))
    q_n = each(lambda i: jnp.sum(qs[i] * n_sts[i], axis=-1, keepdims=True))
    inter = each(lambda i: bcs[i] + m_sts[i])
    m_t = each(lambda i: jnp.maximum(inter[i], m_intras[i]))
    s_inter = each(lambda i: jnp.exp(inter[i] - m_t[i]))
    s_intra = each(lambda i: jnp.exp(m_intras[i] - m_t[i]))
    den = each(lambda i: jnp.maximum(jnp.abs(s_inter[i] * q_n[i] + s_intra[i] * dens[i]),
                                     jnp.exp(-m_t[i])))
    hh = each(lambda i: (s_inter[i] * q_c[i] + s_intra[i] * num_intra[i]) / den[i])
    ms = each(lambda i: jnp.mean(hh[i] * hh[i], axis=-1, keepdims=True))
    scale = each(lambda i: lax.rsqrt(ms[i] + 1e-6))
    for i in range(n):
        gate = ng_ref[0:1, vcols(i)] * _sigmoid(op_ref[0, rows(i), vcols(i)])
        o_ref[0, rows(i), vcols(i)] = (hh[i] * scale[i] * gate).astype(BF16)


def _mlstm(q, k, v, op, sc, sct, ng, ts):
    B, S, _ = q.shape
    tok = lambda w: pl.BlockSpec((1, ts, w), lambda b, s: (b, s, 0))
    return pl.pallas_call(
        _mlstm_kernel,
        grid=(B, S // ts),
        in_specs=[tok(C_QK), tok(C_QK), tok(C_V), tok(C_V), tok(LANES),
                  pl.BlockSpec((1, 8, ts), lambda b, s: (b, 0, s)),
                  pl.BlockSpec(ng.shape, lambda b, s: (0, 0))],
        out_specs=tok(C_V),
        out_shape=jax.ShapeDtypeStruct((B, S, C_V), BF16),
        scratch_shapes=[pltpu.VMEM((N_HEADS, C_QK_HEAD, C_V_HEAD), F32),
                        pltpu.VMEM((N_HEADS, 1, C_QK_HEAD), F32),
                        pltpu.VMEM((N_HEADS, 1, LANES), F32)],
        compiler_params=_params("arbitrary", "arbitrary"),
        name="mlstm_scan",
    )(q, k, v, op, sc, sct, ng)


def _pad_lanes(a, offset=0):
    return jnp.pad(a, ((0, 0), (offset, LANES - offset - a.shape[1])))


def kernel(x, p, ab_w_in, ab_conv_qkv, ab_a_log, ab_dt_bias, ab_o_norm_g, ab_dw_w, ab_dw_b,
           ab_cn_g, ab_cn_b, ab_w_out, ffn_w_gate, ffn_w_up, ffn_w_down, c_w_in, c_b_i, c_b_f,
           c_norm_g, c_w_out, moe_w_router, moe_b_router, moe_w_gate, moe_w_up, moe_w_down,
           ln_mix_g, ln_mix_b, ln_ffn_g, ln_ffn_b, ple_w_proj, ple_w_gate):
    B, S, D = x.shape
    T = B * S
    ts_front, ts_scan, tm, tm_big, tf = 512, 256, 512, 1024, 512

    w_in = ab_w_in[0]
    n_main = 4 * A_DIM
    wm = jnp.concatenate([w_in[:, :n_main], w_in[:, n_main + 2 * N_HEADS:]], axis=1).astype(BF16)
    ws = _pad_lanes(w_in[:, n_main:n_main + 2 * N_HEADS]).astype(BF16)
    gp = jnp.concatenate([_pad_lanes(ab_a_log[0][None, :], N_HEADS),
                          _pad_lanes(ab_dt_bias[0][None, :], N_HEADS)], axis=0)
    dwp = jnp.stack([ab_dw_b[0], ab_cn_g[0], ab_cn_b[0]], axis=0)
    q, k, v, z, sc, sct, u = _ab_front(x, wm, ws, ab_conv_qkv[0], gp, ab_dw_w[0], dwp, ts_front)
    o_a = _gdn(q, k, v, z, sc, sct, ab_o_norm_g[0][None, :], ts_scan)
    w_out = ab_w_out[0].astype(BF16)
    lnp = jnp.stack([ln_mix_g[0], ln_mix_b[0]], axis=0)
    xf = x.reshape(T, D)
    x1 = _proj_ln([o_a.reshape(T, A_DIM), u.reshape(T, B_CH)], [w_out[:A_DIM], w_out[A_DIM:]],
                  xf, lnp, tm)
    lnp = jnp.stack([ln_ffn_g[0], ln_ffn_b[0]], axis=0)
    x2 = _ffn(x1, ffn_w_gate[0].astype(BF16), ffn_w_up[0].astype(BF16),
              ffn_w_down[0].astype(BF16), lnp, p[0].reshape(T, PLE_DIM),
              ple_w_gate[0].astype(BF16), ple_w_proj[0].astype(BF16), tm_big, tf)

    w_in = c_w_in[0]
    n_main = 2 * C_QK + 2 * C_V
    wm = w_in[:, :n_main].astype(BF16)
    ws = _pad_lanes(w_in[:, n_main:]).astype(BF16)
    gb = _pad_lanes(jnp.concatenate([c_b_i[0], c_b_f[0]])[None, :])
    q, k, v, op, sc, sct = _c_front(x2.reshape(B, S, D), wm, ws, gb, ts_front)
    hh = _mlstm(q, k, v, op, sc, sct, c_norm_g[0][None, :], ts_scan)
    lnp = jnp.stack([ln_mix_g[1], ln_mix_b[1]], axis=0)
    x3 = _proj_ln([hh.reshape(T, C_V)], [c_w_out[0].astype(BF16)], x2, lnp, tm)

    wr = _pad_lanes(moe_w_router[0])
    wr_hi = wr.astype(BF16)
    wr_lo = (wr - wr_hi.astype(F32)).astype(BF16)
    info, counts = _router(x3, wr_hi, wr_lo, _pad_lanes(moe_b_router[0][None, :]), tm)
    dst, src, tile_expert, n_valid = _route_plan(info, counts, tm_big)
    y = _moe_experts(tile_expert, n_valid, src, x3, moe_w_gate[0].astype(BF16),
                     moe_w_up[0].astype(BF16), moe_w_down[0].astype(BF16), tm_big, tf)
    lnp = jnp.stack([ln_ffn_g[1], ln_ffn_b[1]], axis=0)
    x4 = _moe_combine(dst.reshape(T // tm, 1, 2 * tm), x3, info, y, lnp, p[1].reshape(T, PLE_DIM),
                      ple_w_gate[1].astype(BF16), ple_w_proj[1].astype(BF16), tm)
    return x4.reshape(B, S, D)
```

```python
import functools

import jax
import jax.numpy as jnp
from jax import lax
from jax.experimental import pallas as pl
from jax.experimental.pallas import tpu as pltpu

F32 = jnp.float32
BF16 = jnp.bfloat16

D_MODEL = 1024
N_HEADS = 4
CHUNK = 64
A_HEAD = 128
A_DIM = N_HEADS * A_HEAD
SHORT_CONV = 4
B_CH = 512
B_CONV = 31
C_QK_HEAD = 128
C_V_HEAD = 256
C_QK = N_HEADS * C_QK_HEAD
C_V = N_HEADS * C_V_HEAD
GATE_CAP = 15.0
D_FF = 3584
N_EXPERTS = 8
PLE_DIM = 256
DEPTH = 2
DN_ALPHA = (2 * DEPTH) ** 0.25
LANES = 128
SUBLANES = 8
QKV_HALO = 8
DW_HALO = 32
VMEM_LIMIT = 56 * 1024 * 1024


def _dot(a, b):
    return jnp.dot(a, b, preferred_element_type=F32)


def _dot_nt(a, b):
    return lax.dot_general(a, b, (((1,), (1,)), ((), ())), preferred_element_type=F32)


def _dot_tn(a, b):
    return lax.dot_general(a, b, (((0,), (0,)), ((), ())), preferred_element_type=F32)


def _sigmoid(t):
    return 1.0 / (1.0 + jnp.exp(-t))


def _silu(t):
    return t * _sigmoid(t)


def _softplus(t):
    return jnp.maximum(t, 0.0) + jnp.log1p(jnp.exp(-jnp.abs(t)))


def _layer_norm(t, g, b):
    mu = jnp.mean(t, axis=-1, keepdims=True)
    c = t - mu
    var = jnp.mean(c * c, axis=-1, keepdims=True)
    return c * lax.rsqrt(var + 1e-5) * g + b


def _chunk_cumsum(val):
    row = lax.broadcasted_iota(jnp.int32, val.shape, 0) & (CHUNK - 1)
    sh = 1
    while sh < CHUNK:
        val = val + jnp.where(row >= sh, pltpu.roll(val, sh, axis=0), 0.0)
        sh *= 2
    return val


def _params(*sem):
    return pltpu.CompilerParams(dimension_semantics=sem, vmem_limit_bytes=VMEM_LIMIT)


def _ab_front_kernel(x_ref, wm_ref, ws_ref, cw_ref, gp_ref, dww_ref, dwp_ref,
                     q_ref, k_ref, v_ref, z_ref, sc_ref, sct_ref, u_ref,
                     qkv_ext, u_ext, u_shift):
    s = pl.program_id(1)
    ts = x_ref.shape[1]
    xb = x_ref[0].astype(BF16)

    @pl.when(s == 0)
    def _():
        qkv_ext[0:QKV_HALO, :] = jnp.zeros((QKV_HALO, 3 * A_DIM), F32)
        u_ext[0:DW_HALO, :] = jnp.zeros((DW_HALO, B_CH), F32)

    @pl.when(s > 0)
    def _():
        qkv_ext[0:QKV_HALO, :] = qkv_ext[ts:ts + QKV_HALO, :]
        u_ext[0:DW_HALO, :] = u_ext[ts:ts + DW_HALO, :]

    qkv_ext[QKV_HALO:QKV_HALO + ts, :] = _dot(xb, wm_ref[:, 0:3 * A_DIM])
    acc = None
    for j in range(SHORT_CONV):
        off = QKV_HALO - (SHORT_CONV - 1) + j
        term = cw_ref[j:j + 1, :] * qkv_ext[off:off + ts, :]
        acc = term if acc is None else acc + term
    qkv = _silu(acc)
    for h in range(N_HEADS):
        lo, hi = h * A_HEAD, (h + 1) * A_HEAD
        qh = qkv[:, lo:hi]
        kh = qkv[:, A_DIM + lo:A_DIM + hi]
        qn = lax.rsqrt(jnp.sum(qh * qh, axis=-1, keepdims=True) + 1e-6)
        kn = lax.rsqrt(jnp.sum(kh * kh, axis=-1, keepdims=True) + 1e-6)
        q_ref[0, :, lo:hi] = qh * qn * (A_HEAD ** -0.5)
        k_ref[0, :, lo:hi] = kh * kn
    v_ref[0] = qkv[:, 2 * A_DIM:3 * A_DIM]
    z_ref[0] = _dot(xb, wm_ref[:, 3 * A_DIM:4 * A_DIM])

    hs = _dot(xb, ws_ref[...])
    lane = lax.broadcasted_iota(jnp.int32, hs.shape, 1)
    beta = _sigmoid(hs)
    g = -jnp.exp(gp_ref[0:1, :]) * _softplus(hs + gp_ref[1:2, :])
    sc = jnp.where(lane < N_HEADS, beta, _chunk_cumsum(g))
    sc_ref[0] = sc
    sct_ref[0] = jnp.transpose(sc)[0:8, :]

    glu = _dot(xb, wm_ref[:, 4 * A_DIM:4 * A_DIM + 2 * B_CH])
    u_ext[DW_HALO:DW_HALO + ts, :] = glu[:, 0:B_CH] * _sigmoid(glu[:, B_CH:2 * B_CH])
    first_off = DW_HALO - (B_CONV - 1)
    span = ts + DW_HALO - 8
    acc = None
    for res in range(8):
        offs = [o for o in range(first_off, DW_HALO + 1) if o % 8 == res]
        if res:
            u_shift[...] = u_ext[res:res + span, :]
        src = u_shift if res else u_ext
        for off in offs:
            lo = off - res if res else off
            term = dww_ref[off - first_off:off - first_off + 1, :] * src[lo:lo + ts, :]
            acc = term if acc is None else acc + term
    u = acc + dwp_ref[0:1, :]
    u = _layer_norm(u, dwp_ref[1:2, :], dwp_ref[2:3, :])
    u_ref[0] = _silu(u).astype(BF16)


def _ab_front(x, wm, ws, cw, gp, dww, dwp, ts):
    B, S, D = x.shape
    grid = (B, S // ts)
    tok = lambda w: pl.BlockSpec((1, ts, w), lambda b, s: (b, s, 0))
    full = lambda a: pl.BlockSpec(a.shape, lambda b, s: (0,) * a.ndim)
    out_shape = (
        jax.ShapeDtypeStruct((B, S, A_DIM), F32),
        jax.ShapeDtypeStruct((B, S, A_DIM), F32),
        jax.ShapeDtypeStruct((B, S, A_DIM), F32),
        jax.ShapeDtypeStruct((B, S, A_DIM), F32),
        jax.ShapeDtypeStruct((B, S, LANES), F32),
        jax.ShapeDtypeStruct((B, 8, S), F32),
        jax.ShapeDtypeStruct((B, S, B_CH), BF16),
    )
    out_specs = (tok(A_DIM), tok(A_DIM), tok(A_DIM), tok(A_DIM), tok(LANES),
                 pl.BlockSpec((1, 8, ts), lambda b, s: (b, 0, s)), tok(B_CH))
    return pl.pallas_call(
        _ab_front_kernel,
        grid=grid,
        in_specs=[tok(D), full(wm), full(ws), full(cw), full(gp), full(dww), full(dwp)],
        out_specs=out_specs,
        out_shape=out_shape,
        scratch_shapes=[pltpu.VMEM((ts + QKV_HALO, 3 * A_DIM), F32),
                        pltpu.VMEM((ts + DW_HALO, B_CH), F32),
                        pltpu.VMEM((ts + DW_HALO - 8, B_CH), F32)],
        compiler_params=_params("arbitrary", "arbitrary"),
        name="ab_front",
    )(x, wm, ws, cw, gp, dww, dwp)


def _gdn_kernel(q_ref, k_ref, v_ref, z_ref, sc_ref, sct_ref, ng_ref, o_ref, st_ref):
    s = pl.program_id(1)
    ts = q_ref.shape[1]

    @pl.when(s == 0)
    def _():
        st_ref[...] = jnp.zeros(st_ref.shape, F32)

    row = lax.broadcasted_iota(jnp.int32, (CHUNK, CHUNK), 0)
    col = lax.broadcasted_iota(jnp.int32, (CHUNK, CHUNK), 1)
    causal = row >= col
    strict = row > col
    eye = jnp.where(row == col, 1.0, 0.0).astype(F32)

    n_chunks = ts // CHUNK
    pairs = [(c, h) for c in range(n_chunks) for h in range(N_HEADS)]
    pw, tinv, attn, rhs, k_dec, q_dec, d_last = [], [], [], [], [], [], []
    for c, h in pairs:
        r0, r1 = c * CHUNK, (c + 1) * CHUNK
        lo, hi = h * A_HEAD, (h + 1) * A_HEAD
        qh = q_ref[0, r0:r1, lo:hi]
        kh = k_ref[0, r0:r1, lo:hi]
        vh = v_ref[0, r0:r1, lo:hi]
        beta = sc_ref[0, r0:r1, h:h + 1]
        gc = sc_ref[0, r0:r1, N_HEADS + h:N_HEADS + h + 1]
        gc_row = sct_ref[0, N_HEADS + h:N_HEADS + h + 1, r0:r1]
        gc_last = gc[CHUNK - 1:CHUNK, :]
        decay = jnp.where(causal, jnp.exp(jnp.where(causal, gc - gc_row, 0.0)), 0.0)
        kb = kh * beta
        egc = jnp.exp(gc)
        kq = _dot_nt(jnp.concatenate([kb, qh], axis=0).astype(BF16), kh.astype(BF16))
        a_neg = jnp.where(strict, -kq[0:CHUNK] * decay, 0.0)
        pw.append(a_neg)
        tinv.append(eye + a_neg)
        attn.append((kq[CHUNK:2 * CHUNK] * decay).astype(BF16))
        rhs.append(jnp.concatenate([vh * beta, kb * egc], axis=1).astype(BF16))
        k_dec.append((kh * jnp.exp(gc_last - gc)).astype(BF16))
        q_dec.append(qh * egc)
        d_last.append(jnp.exp(gc_last))
    for _ in range(5):
        for i in range(len(pairs)):
            pwb = pw[i].astype(BF16)
            pw[i] = _dot(pwb, pwb)
        for i in range(len(pairs)):
            tinv[i] = tinv[i] + _dot(tinv[i].astype(BF16), pw[i].astype(BF16))
    uw = [_dot(tinv[i].astype(BF16), rhs[i]).astype(BF16) for i in range(len(pairs))]
    auw = [_dot(attn[i], uw[i]) for i in range(len(pairs))]
    kuw = [_dot_tn(k_dec[i], uw[i]) for i in range(len(pairs))]
    st = [st_ref[h] for h in range(N_HEADS)]
    for i, (c, h) in enumerate(pairs):
        r0, r1 = c * CHUNK, (c + 1) * CHUNK
        lo, hi = h * A_HEAD, (h + 1) * A_HEAD
        stb = st[h].astype(BF16)
        q_eff = (q_dec[i] - auw[i][:, A_HEAD:]).astype(BF16)
        o = _dot(q_eff, stb) + auw[i][:, 0:A_HEAD]
        st[h] = (d_last[i] * st[h] + kuw[i][:, 0:A_HEAD]
                 - _dot(kuw[i][:, A_HEAD:].astype(BF16), stb))
        o = o * lax.rsqrt(jnp.mean(o * o, axis=-1, keepdims=True) + 1e-6) * ng_ref[0:1, :]
        o_ref[0, r0:r1, lo:hi] = (o * _silu(z_ref[0, r0:r1, lo:hi])).astype(BF16)
    for h in range(N_HEADS):
        st_ref[h] = st[h]


def _gdn(q, k, v, z, sc, sct, ng, ts):
    B, S, _ = q.shape
    tok = lambda w: pl.BlockSpec((1, ts, w), lambda b, s: (b, s, 0))
    return pl.pallas_call(
        _gdn_kernel,
        grid=(B, S // ts),
        in_specs=[tok(A_DIM), tok(A_DIM), tok(A_DIM), tok(A_DIM), tok(LANES),
                  pl.BlockSpec((1, 8, ts), lambda b, s: (b, 0, s)),
                  pl.BlockSpec(ng.shape, lambda b, s: (0, 0))],
        out_specs=tok(A_DIM),
        out_shape=jax.ShapeDtypeStruct((B, S, A_DIM), BF16),
        scratch_shapes=[pltpu.VMEM((N_HEADS, A_HEAD, A_HEAD), F32)],
        compiler_params=_params("arbitrary", "arbitrary"),
        name="gdn_scan",
    )(q, k, v, z, sc, sct, ng)


def _token_rows(s, n_tokens):
    return pl.ds(s, n_tokens, stride=D_MODEL // LANES)


def _proj_ln_kernel(*refs, n_in, token_major_copy):
    a_refs = refs[0:n_in]
    w_refs = refs[n_in:2 * n_in]
    x_ref, lnp_ref, o_ref = refs[2 * n_in:2 * n_in + 3]
    mix = None
    for a_ref, w_ref in zip(a_refs, w_refs):
        t = _dot(a_ref[...], w_ref[...])
        mix = t if mix is None else mix + t
    y = _layer_norm(DN_ALPHA * x_ref[...] + mix, lnp_ref[0:1, :], lnp_ref[1:2, :])
    o_ref[...] = y
    if token_major_copy:
        ot_ref = refs[2 * n_in + 3]
        for s in range(D_MODEL // LANES):
            ot_ref[_token_rows(s, y.shape[0]), :] = y[:, s * LANES:(s + 1) * LANES]


def _proj_ln(acts, weights, x, lnp, tm, token_major_copy=False):
    T, D = x.shape
    n_in = len(acts)
    in_specs = ([pl.BlockSpec((tm, a.shape[1]), lambda i: (i, 0)) for a in acts]
                + [pl.BlockSpec(w.shape, lambda i: (0, 0)) for w in weights]
                + [pl.BlockSpec((tm, D), lambda i: (i, 0)),
                   pl.BlockSpec(lnp.shape, lambda i: (0, 0))])
    out_specs = [pl.BlockSpec((tm, D), lambda i: (i, 0))]
    out_shape = [jax.ShapeDtypeStruct((T, D), F32)]
    if token_major_copy:
        out_specs.append(pl.BlockSpec((tm * D // LANES, LANES), lambda i: (i, 0)))
        out_shape.append(jax.ShapeDtypeStruct((T * D // LANES, LANES), F32))
    outs = pl.pallas_call(
        functools.partial(_proj_ln_kernel, n_in=n_in, token_major_copy=token_major_copy),
        grid=(T // tm,),
        in_specs=in_specs,
        out_specs=tuple(out_specs),
        out_shape=tuple(out_shape),
        compiler_params=_params("arbitrary"),
        name="proj_ln",
    )(*acts, *weights, x, lnp)
    return outs if token_major_copy else outs[0]


def _ffn_epilogue(x, acc, lnp_ref, p_ref, wpg_ref, wpp_ref, o_ref):
    y = _layer_norm(DN_ALPHA * x + acc, lnp_ref[0:1, :], lnp_ref[1:2, :])
    gate = _sigmoid(_dot(y.astype(BF16), wpg_ref[...]))
    o_ref[...] = y + gate * _dot(p_ref[...].astype(BF16), wpp_ref[...])


def _ffn_kernel(x_ref, wg_ref, wu_ref, wd_ref, lnp_ref, p_ref, wpg_ref, wpp_ref,
                o_ref, xb_ref):
    j = pl.program_id(1)

    @pl.when(j == 0)
    def _():
        xb_ref[...] = x_ref[...].astype(BF16)
        o_ref[...] = jnp.zeros(o_ref.shape, F32)

    xb = xb_ref[...]
    hid = _silu(_dot(xb, wg_ref[...])) * _dot(xb, wu_ref[...])
    o_ref[...] += _dot(hid.astype(BF16), wd_ref[...])

    @pl.when(j == pl.num_programs(1) - 1)
    def _():
        _ffn_epilogue(x_ref[...], o_ref[...], lnp_ref, p_ref, wpg_ref, wpp_ref, o_ref)


def _ffn(x, wg, wu, wd, lnp, p, wpg, wpp, tm, tf):
    T, D = x.shape
    F = wg.shape[1]
    return pl.pallas_call(
        _ffn_kernel,
        grid=(T // tm, F // tf),
        in_specs=[pl.BlockSpec((tm, D), lambda i, j: (i, 0)),
                  pl.BlockSpec((D, tf), lambda i, j: (0, j)),
                  pl.BlockSpec((D, tf), lambda i, j: (0, j)),
                  pl.BlockSpec((tf, D), lambda i, j: (j, 0)),
                  pl.BlockSpec(lnp.shape, lambda i, j: (0, 0)),
                  pl.BlockSpec((tm, PLE_DIM), lambda i, j: (i, 0)),
                  pl.BlockSpec(wpg.shape, lambda i, j: (0, 0)),
                  pl.BlockSpec(wpp.shape, lambda i, j: (0, 0))],
        out_specs=pl.BlockSpec((tm, D), lambda i, j: (i, 0)),
        out_shape=jax.ShapeDtypeStruct((T, D), F32),
        scratch_shapes=[pltpu.VMEM((tm, D), BF16)],
        compiler_params=_params("arbitrary", "arbitrary"),
        name="ffn",
    )(x, wg, wu, wd, lnp, p, wpg, wpp)


ROUTE_IDX1, ROUTE_IDX2, ROUTE_W1, ROUTE_W2, ROUTE_RANK1, ROUTE_RANK2 = range(6)


def _router_kernel(x_ref, wh_ref, wl_ref, br_ref, info_ref, cnt_ref, carry_ref):
    i = pl.program_id(0)

    @pl.when(i == 0)
    def _():
        carry_ref[...] = jnp.zeros(carry_ref.shape, F32)

    x = x_ref[...]
    tm = x.shape[0]
    xh = x.astype(BF16)
    xl = (x - xh.astype(F32)).astype(BF16)
    logits = (_dot(xh, wh_ref[...]) + _dot(xl, wh_ref[...]) + _dot(xh, wl_ref[...])
              + br_ref[0:1, :])
    lane = lax.broadcasted_iota(jnp.int32, logits.shape, 1)
    neg = jnp.float32(-jnp.inf)
    logits = jnp.where(lane < N_EXPERTS, logits, neg)
    m1 = jnp.max(logits, axis=-1, keepdims=True)
    i1 = jnp.min(jnp.where(logits == m1, lane, LANES), axis=-1, keepdims=True)
    sel1 = lane == i1
    rest = jnp.where(sel1, neg, logits)
    m2 = jnp.max(rest, axis=-1, keepdims=True)
    i2 = jnp.min(jnp.where(rest == m2, lane, LANES), axis=-1, keepdims=True)
    sel2 = lane == i2
    e2 = jnp.exp(m2 - m1)
    den = 1.0 + e2

    sel = jnp.where(jnp.logical_or(sel1, sel2), 1.0, 0.0)
    row = lax.broadcasted_iota(jnp.int32, (tm, tm), 0)
    col = lax.broadcasted_iota(jnp.int32, (tm, tm), 1)
    earlier = jnp.where(row > col, 1.0, 0.0).astype(BF16)
    rank = _dot(earlier, sel.astype(BF16)) + carry_ref[0:1, :]
    rank1 = jnp.sum(jnp.where(sel1, rank, 0.0), axis=-1, keepdims=True)
    rank2 = jnp.sum(jnp.where(sel2, rank, 0.0), axis=-1, keepdims=True)

    info = jnp.zeros(logits.shape, F32)
    for slot, val in ((ROUTE_IDX1, i1.astype(F32)), (ROUTE_IDX2, i2.astype(F32)),
                      (ROUTE_W1, 1.0 / den), (ROUTE_W2, e2 / den),
                      (ROUTE_RANK1, rank1), (ROUTE_RANK2, rank2)):
        info = jnp.where(lane == slot, val, info)
    info_ref[...] = info
    carry_ref[0:1, :] = carry_ref[0:1, :] + jnp.sum(sel, axis=0, keepdims=True)
    cnt_ref[...] = jnp.broadcast_to(carry_ref[0:1, :], cnt_ref.shape)


def _router(x, wh, wl, br, tm):
    T, D = x.shape
    return pl.pallas_call(
        _router_kernel,
        grid=(T // tm,),
        in_specs=[pl.BlockSpec((tm, D), lambda i: (i, 0)),
                  pl.BlockSpec(wh.shape, lambda i: (0, 0)),
                  pl.BlockSpec(wl.shape, lambda i: (0, 0)),
                  pl.BlockSpec(br.shape, lambda i: (0, 0))],
        out_specs=(pl.BlockSpec((tm, LANES), lambda i: (i, 0)),
                   pl.BlockSpec((8, LANES), lambda i: (0, 0))),
        out_shape=(jax.ShapeDtypeStruct((T, LANES), F32),
                   jax.ShapeDtypeStruct((8, LANES), F32)),
        scratch_shapes=[pltpu.VMEM((8, LANES), F32)],
        compiler_params=_params("arbitrary"),
        name="router",
    )(x, wh, wl, br)


def _stage_indices(idx_vmem, idx_smem, sem):
    cp = pltpu.make_async_copy(idx_vmem, idx_smem, sem)
    cp.start()
    cp.wait()


def _moe_expert_kernel(te_ref, nv_ref, nr_ref, src_ref, dst_ref, x_hbm, wg_ref, wu_ref, wd_ref, y_hbm,
                       xg_ref, xb_ref, acc_ref, yo_ref, gidx_ref, sidx_ref,
                       sem_idx, sem_g, sem_s, *, n_j):
    s = pl.program_id(0)
    j = pl.program_id(1)
    tm = xb_ref.shape[0]
    rows_per_step = tm // n_j
    n_blk = D_MODEL // LANES
    nv = nv_ref[0]
    gather_on = s < nv
    compute_on = jnp.logical_and(s >= 1, s - 1 < nv)
    scatter_on = jnp.logical_and(s >= 2, s - 2 < nv)
    slot_cur = lax.rem(s, 2)
    slot_prev = 1 - slot_cur

    def token_tile(first_row):
        return pl.ds(pl.multiple_of(first_row, n_blk), n_blk)

    def gather_copy(slot, r, src_row):
        return pltpu.make_async_copy(x_hbm.at[token_tile(src_row), :],
                                     xg_ref.at[slot, token_tile(r * n_blk), :], sem_g)

    def scatter_copy(slot, r, dst_row):
        return pltpu.make_async_copy(yo_ref.at[slot, token_tile(r * n_blk), :],
                                     y_hbm.at[token_tile(dst_row), :], sem_s)

    def real_rows(tile):
        return nr_ref[jnp.clip(tile, 0, nr_ref.shape[0] - 1)]

    def drain(copy_of_row, n_rows=None):
        def body(g, carry):
            r0 = pl.multiple_of(g * SUBLANES, SUBLANES)
            for u in range(SUBLANES):
                copy_of_row(r0 + u).wait()
            return carry

        def one(r, carry):
            copy_of_row(r).wait()
            return carry

        def all_rows():
            lax.fori_loop(0, tm // SUBLANES, body, 0)

        def some_rows():
            lax.fori_loop(0, n_rows, one, 0)

        if n_rows is None:
            all_rows()
        else:
            pl.when(n_rows == tm)(all_rows)
            pl.when(n_rows < tm)(some_rows)

    @pl.when(j == 0)
    def _():
        @pl.when(compute_on)
        def _():
            drain(lambda r: gather_copy(slot_prev, r, 0))
            for blk in range(n_blk):
                xb_ref[:, blk * LANES:(blk + 1) * LANES] = (
                    xg_ref[slot_prev, _token_rows(blk, tm), :].astype(BF16))
            acc_ref[...] = jnp.zeros(acc_ref.shape, F32)

        @pl.when(jnp.logical_and(s >= 3, s - 3 < nv))
        def _():
            drain(lambda r: scatter_copy(slot_prev, r, 0), real_rows(s - 3))

        @pl.when(gather_on)
        def _():
            _stage_indices(src_ref.at[0], gidx_ref, sem_idx)

        @pl.when(scatter_on)
        def _():
            _stage_indices(dst_ref.at[0], sidx_ref, sem_idx)

    base = pl.multiple_of(j * rows_per_step, rows_per_step)

    def issue_gathers():
        for u in range(rows_per_step):
            gather_copy(slot_cur, base + u, gidx_ref[0, base + u]).start()

    def issue_scatters():
        n_real = real_rows(s - 2)

        @pl.when(base + rows_per_step <= n_real)
        def _():
            for u in range(rows_per_step):
                scatter_copy(slot_cur, base + u, sidx_ref[0, base + u]).start()

        @pl.when(base + rows_per_step > n_real)
        def _():
            def one(r, carry):
                scatter_copy(slot_cur, r, sidx_ref[0, r]).start()
                return carry
            lax.fori_loop(base, jnp.maximum(base, jnp.minimum(base + rows_per_step, n_real)),
                          one, 0)

    def compute():
        xb = xb_ref[...]
        hid = _silu(_dot(xb, wg_ref[0])) * _dot(xb, wu_ref[0])
        acc_ref[...] += _dot(hid.astype(BF16), wd_ref[0])

    pl.when(gather_on)(issue_gathers)
    pl.when(compute_on)(compute)
    pl.when(scatter_on)(issue_scatters)

    last_j = j == n_j - 1

    @pl.when(jnp.logical_and(last_j, compute_on))
    def _():
        for blk in range(n_blk):
            yo_ref[slot_prev, _token_rows(blk, tm), :] = acc_ref[:, blk * LANES:(blk + 1) * LANES]

    @pl.when(jnp.logical_and(jnp.logical_and(last_j, s == pl.num_programs(0) - 1), scatter_on))
    def _():
        drain(lambda r: scatter_copy(slot_cur, r, 0), real_rows(s - 2))


def _moe_experts(tile_expert, n_valid, n_real, src_rows, dst_rows, x, wg, wu, wd, n_out_rows,
                 tm, tf):
    E, D, F = wg.shape
    n_tiles = src_rows.shape[0]
    n_j = F // tf
    assert tm % (n_j * SUBLANES) == 0

    def computing(s, nv):
        return jnp.logical_and(s >= 1, s - 1 < nv[0])

    def tile_of(s):
        return jnp.clip(s - 1, 0, n_tiles - 1)

    def w_map(s, j, te, nv, nr):
        return (te[tile_of(s)], 0, jnp.where(computing(s, nv), j, 0))

    def wd_map(s, j, te, nv, nr):
        return (te[tile_of(s)], jnp.where(computing(s, nv), j, 0), 0)

    grid_spec = pltpu.PrefetchScalarGridSpec(
        num_scalar_prefetch=3,
        grid=(n_tiles + 2, n_j),
        in_specs=[pl.BlockSpec((1, 1, tm),
                               lambda s, j, te, nv, nr: (jnp.minimum(s, n_tiles - 1), 0, 0)),
                  pl.BlockSpec((1, 1, tm),
                               lambda s, j, te, nv, nr: (jnp.clip(s - 2, 0, n_tiles - 1), 0, 0)),
                  pl.BlockSpec(memory_space=pl.ANY),
                  pl.BlockSpec((1, D, tf), w_map),
                  pl.BlockSpec((1, D, tf), w_map),
                  pl.BlockSpec((1, tf, D), wd_map)],
        out_specs=pl.BlockSpec(memory_space=pl.ANY),
        scratch_shapes=[pltpu.VMEM((2, tm * D // LANES, LANES), F32),
                        pltpu.VMEM((tm, D), BF16),
                        pltpu.VMEM((tm, D), F32),
                        pltpu.VMEM((2, tm * D // LANES, LANES), F32),
                        pltpu.SMEM((1, tm), jnp.int32), pltpu.SMEM((1, tm), jnp.int32),
                        pltpu.SemaphoreType.DMA(()), pltpu.SemaphoreType.DMA(()),
                        pltpu.SemaphoreType.DMA(())],
    )
    return pl.pallas_call(
        functools.partial(_moe_expert_kernel, n_j=n_j),
        grid_spec=grid_spec,
        out_shape=jax.ShapeDtypeStruct((n_out_rows * D // LANES, LANES), F32),
        compiler_params=_params("arbitrary", "arbitrary"),
        name="moe_experts",
    )(tile_expert, n_valid, n_real, src_rows, dst_rows, x, wg, wu, wd)


def _moe_combine_kernel(x_ref, info_ref, ya_ref, yb_ref, lnp_ref, p_ref, wpg_ref, wpp_ref, o_ref):
    info = info_ref[...]
    tm = x_ref.shape[0]
    w1 = info[:, ROUTE_W1:ROUTE_W1 + 1]
    w2 = info[:, ROUTE_W2:ROUTE_W2 + 1]
    mix = jnp.concatenate(
        [w1 * ya_ref[_token_rows(blk, tm), :] + w2 * yb_ref[_token_rows(blk, tm), :]
         for blk in range(D_MODEL // LANES)], axis=1)
    _ffn_epilogue(x_ref[...], mix, lnp_ref, p_ref, wpg_ref, wpp_ref, o_ref)


def _moe_combine(x, info, y, lnp, p, wpg, wpp, tm):
    T, D = x.shape
    y_block = (tm * D // LANES, LANES)
    return pl.pallas_call(
        _moe_combine_kernel,
        grid=(T // tm,),
        in_specs=[pl.BlockSpec((tm, D), lambda i: (i, 0)),
                  pl.BlockSpec((tm, LANES), lambda i: (i, 0)),
                  pl.BlockSpec(y_block, lambda i: (i, 0)),
                  pl.BlockSpec(y_block, lambda i: (T // tm + i, 0)),
                  pl.BlockSpec(lnp.shape, lambda i: (0, 0)),
                  pl.BlockSpec((tm, PLE_DIM), lambda i: (i, 0)),
                  pl.BlockSpec(wpg.shape, lambda i: (0, 0)),
                  pl.BlockSpec(wpp.shape, lambda i: (0, 0))],
        out_specs=pl.BlockSpec((tm, D), lambda i: (i, 0)),
        out_shape=jax.ShapeDtypeStruct((T, D), F32),
        compiler_params=_params("arbitrary"),
        name="moe_combine",
    )(x, info, y, y, lnp, p, wpg, wpp)


def _route_plan(info, counts, tm):
    T = info.shape[0]
    n_tiles = (2 * T + N_EXPERTS * (tm - 1)) // tm + 1
    n_rows = n_tiles * tm
    as_int = lambda col: info[:, col].astype(jnp.int32)
    idx1, idx2 = as_int(ROUTE_IDX1), as_int(ROUTE_IDX2)
    counts = counts[0, :N_EXPERTS].astype(jnp.int32)
    padded = ((counts + tm - 1) // tm) * tm
    ends = jnp.cumsum(padded)
    starts = ends - padded
    grouped_row = jnp.concatenate([starts[idx1] + as_int(ROUTE_RANK1),
                                   starts[idx2] + as_int(ROUTE_RANK2)])
    out_row = jnp.arange(2 * T, dtype=jnp.int32)
    owner = jnp.full((n_rows,), -1, jnp.int32).at[grouped_row].set(out_row)
    real = owner >= 0
    dst = jnp.maximum(owner, 0)
    src = jnp.where(real, jnp.where(owner >= T, owner - T, owner), 0)
    n_real = jnp.sum(real.reshape(n_tiles, tm), axis=1).astype(jnp.int32)
    tile_start = jnp.arange(n_tiles, dtype=jnp.int32) * tm
    tile_expert = jnp.minimum(jnp.sum(tile_start[:, None] >= ends[None, :], axis=1),
                              N_EXPERTS - 1).astype(jnp.int32)
    n_valid = (ends[-1] // tm).astype(jnp.int32).reshape(1)
    shape = (n_tiles, 1, tm)
    n_blk = D_MODEL // LANES
    return (src * n_blk).reshape(shape), (dst * n_blk).reshape(shape), tile_expert, n_valid, n_real


def _c_front_kernel(x_ref, wm_ref, ws_ref, gb_ref, q_ref, k_ref, v_ref, op_ref, sc_ref, sct_ref):
    xb = x_ref[0].astype(BF16)
    q_ref[0] = _dot(xb, wm_ref[:, 0:C_QK]) * (C_QK_HEAD ** -0.5)
    k_ref[0] = _dot(xb, wm_ref[:, C_QK:2 * C_QK])
    v_ref[0] = _dot(xb, wm_ref[:, 2 * C_QK:2 * C_QK + C_V])
    op_ref[0] = _dot(xb, wm_ref[:, 2 * C_QK + C_V:2 * C_QK + 2 * C_V])
    hs = _dot(xb, ws_ref[...]) + gb_ref[0:1, :]
    capped = GATE_CAP * jnp.tanh(hs / GATE_CAP)
    lane = lax.broadcasted_iota(jnp.int32, hs.shape, 1)
    log_f = jnp.minimum(capped, 0.0) - jnp.log1p(jnp.exp(-jnp.abs(capped)))
    sc = jnp.where(lane < N_HEADS, capped, _chunk_cumsum(log_f))
    sc_ref[0] = sc
    sct_ref[0] = jnp.transpose(sc)[0:8, :]


def _c_front(x, wm, ws, gb, ts):
    B, S, D = x.shape
    tok = lambda w: pl.BlockSpec((1, ts, w), lambda b, s: (b, s, 0))
    full = lambda a: pl.BlockSpec(a.shape, lambda b, s: (0,) * a.ndim)
    out_shape = (
        jax.ShapeDtypeStruct((B, S, C_QK), F32),
        jax.ShapeDtypeStruct((B, S, C_QK), F32),
        jax.ShapeDtypeStruct((B, S, C_V), F32),
        jax.ShapeDtypeStruct((B, S, C_V), F32),
        jax.ShapeDtypeStruct((B, S, LANES), F32),
        jax.ShapeDtypeStruct((B, 8, S), F32),
    )
    out_specs = (tok(C_QK), tok(C_QK), tok(C_V), tok(C_V), tok(LANES),
                 pl.BlockSpec((1, 8, ts), lambda b, s: (b, 0, s)))
    return pl.pallas_call(
        _c_front_kernel,
        grid=(B, S // ts),
        in_specs=[tok(D), full(wm), full(ws), full(gb)],
        out_specs=out_specs,
        out_shape=out_shape,
        compiler_params=_params("arbitrary", "arbitrary"),
        name="c_front",
    )(x, wm, ws, gb)


def _mlstm_kernel(q_ref, k_ref, v_ref, op_ref, sc_ref, sct_ref, ng_ref, o_ref,
                  c_ref, n_ref, m_ref):
    s = pl.program_id(1)
    ts = q_ref.shape[1]

    @pl.when(s == 0)
    def _():
        c_ref[...] = jnp.zeros(c_ref.shape, F32)
        n_ref[...] = jnp.zeros(n_ref.shape, F32)
        m_ref[...] = jnp.zeros(m_ref.shape, F32)

    row = lax.broadcasted_iota(jnp.int32, (CHUNK, CHUNK), 0)
    col = lax.broadcasted_iota(jnp.int32, (CHUNK, CHUNK), 1)
    causal = row >= col
    neg = jnp.float32(-jnp.inf)

    n_chunks = ts // CHUNK
    pairs = [(c, h) for c in range(n_chunks) for h in range(N_HEADS)]
    n = len(pairs)
    each = lambda f: [f(i) for i in range(n)]
    rows = lambda i: slice(pairs[i][0] * CHUNK, (pairs[i][0] + 1) * CHUNK)
    kcols = lambda i: slice(pairs[i][1] * C_QK_HEAD, (pairs[i][1] + 1) * C_QK_HEAD)
    vcols = lambda i: slice(pairs[i][1] * C_V_HEAD, (pairs[i][1] + 1) * C_V_HEAD)
    head = lambda i: pairs[i][1]

    qs = each(lambda i: q_ref[0, rows(i), kcols(i)])
    ks = each(lambda i: k_ref[0, rows(i), kcols(i)])
    qbs = each(lambda i: qs[i].astype(BF16))
    vbs = each(lambda i: v_ref[0, rows(i), vcols(i)].astype(BF16))
    qk = each(lambda i: _dot_nt(qbs[i], ks[i].astype(BF16)))
    lis = each(lambda i: sc_ref[0, rows(i), head(i):head(i) + 1])
    bcs = each(lambda i: sc_ref[0, rows(i), N_HEADS + head(i):N_HEADS + head(i) + 1])
    b_lasts = each(lambda i: bcs[i][CHUNK - 1:CHUNK, :])
    bc_wide = each(lambda i: jnp.broadcast_to(bcs[i], (CHUNK, CHUNK)))
    dmat = each(lambda i: jnp.where(
        causal,
        bc_wide[i] - sct_ref[0, N_HEADS + head(i):N_HEADS + head(i) + 1, rows(i)]
        + sct_ref[0, head(i):head(i) + 1, rows(i)], neg))
    m_intras = each(lambda i: jnp.max(dmat[i], axis=-1, keepdims=True))
    pmat = each(lambda i: jnp.exp(dmat[i] - m_intras[i]) * qk[i])
    dens = each(lambda i: jnp.sum(pmat[i], axis=-1, keepdims=True))
    num_intra = each(lambda i: _dot(pmat[i].astype(BF16), vbs[i]))
    g_kvs = each(lambda i: b_lasts[i] - bcs[i] + lis[i])
    m_kvs = each(lambda i: jnp.max(g_kvs[i], axis=0, keepdims=True))

    m_run = [m_ref[h, :, 0:1] for h in range(N_HEADS)]
    m_sts, m_news = [], []
    for i in range(n):
        m_sts.append(m_run[head(i)])
        m_run[head(i)] = jnp.maximum(b_lasts[i] + m_run[head(i)], m_kvs[i])
        m_news.append(m_run[head(i)])
    decs = each(lambda i: jnp.exp(b_lasts[i] + m_sts[i] - m_news[i]))
    k_scale = each(lambda i: jnp.exp(g_kvs[i] - m_news[i]))
    kws = each(lambda i: ks[i] * k_scale[i])
    kv = each(lambda i: _dot_tn(kws[i].astype(BF16), vbs[i]))
    k_sum = each(lambda i: jnp.sum(kws[i], axis=0, keepdims=True))
    dec_c = each(lambda i: jnp.broadcast_to(decs[i], (C_QK_HEAD, C_V_HEAD)))
    c_run = [c_ref[h] for h in range(N_HEADS)]
    n_run = [n_ref[h] for h in range(N_HEADS)]
    c_sts, n_sts = [], []
    for i in range(n):
        c_sts.append(c_run[head(i)].astype(BF16))
        n_sts.append(n_run[head(i)])
        c_run[head(i)] = dec_c[i] * c_run[head(i)] + kv[i]
        n_run[head(i)] = decs[i] * n_run[head(i)] + k_sum[i]
    for h in range(N_HEADS):
        c_ref[h] = c_run[h]
        n_ref[h] = n_run[h]
        m_ref[h] = jnp.broadcast_to(m_run[h], (1, LANES))

    q_c = each(lambda i: _dot(qbs[i], c_sts[i]))
    q_n = each(lambda i: jnp.sum(qs[i] * n_sts[i], axis=-1, keepdims=True))
    inter = each(lambda i: bcs[i] + m_sts[i])
    m_t = each(lambda i: jnp.maximum(inter[i], m_intras[i]))
    s_inter = each(lambda i: jnp.exp(inter[i] - m_t[i]))
    s_intra = each(lambda i: jnp.exp(m_intras[i] - m_t[i]))
    den = each(lambda i: jnp.maximum(jnp.abs(s_inter[i] * q_n[i] + s_intra[i] * dens[i]),
                                     jnp.exp(-m_t[i])))
    hh = each(lambda i: (s_inter[i] * q_c[i] + s_intra[i] * num_intra[i]) / den[i])
    ms = each(lambda i: jnp.mean(hh[i] * hh[i], axis=-1, keepdims=True))
    scale = each(lambda i: lax.rsqrt(ms[i] + 1e-6))
    for i in range(n):
        gate = ng_ref[0:1, vcols(i)] * _sigmoid(op_ref[0, rows(i), vcols(i)])
        o_ref[0, rows(i), vcols(i)] = (hh[i] * scale[i] * gate).astype(BF16)


def _mlstm(q, k, v, op, sc, sct, ng, ts):
    B, S, _ = q.shape
    tok = lambda w: pl.BlockSpec((1, ts, w), lambda b, s: (b, s, 0))
    return pl.pallas_call(
        _mlstm_kernel,
        grid=(B, S // ts),
        in_specs=[tok(C_QK), tok(C_QK), tok(C_V), tok(C_V), tok(LANES),
                  pl.BlockSpec((1, 8, ts), lambda b, s: (b, 0, s)),
                  pl.BlockSpec(ng.shape, lambda b, s: (0, 0))],
        out_specs=tok(C_V),
        out_shape=jax.ShapeDtypeStruct((B, S, C_V), BF16),
        scratch_shapes=[pltpu.VMEM((N_HEADS, C_QK_HEAD, C_V_HEAD), F32),
                        pltpu.VMEM((N_HEADS, 1, C_QK_HEAD), F32),
                        pltpu.VMEM((N_HEADS, 1, LANES), F32)],
        compiler_params=_params("arbitrary", "arbitrary"),
        name="mlstm_scan",
    )(q, k, v, op, sc, sct, ng)


def _pad_lanes(a, offset=0):
    return jnp.pad(a, ((0, 0), (offset, LANES - offset - a.shape[1])))


def kernel(x, p, ab_w_in, ab_conv_qkv, ab_a_log, ab_dt_bias, ab_o_norm_g, ab_dw_w, ab_dw_b,
           ab_cn_g, ab_cn_b, ab_w_out, ffn_w_gate, ffn_w_up, ffn_w_down, c_w_in, c_b_i, c_b_f,
           c_norm_g, c_w_out, moe_w_router, moe_b_router, moe_w_gate, moe_w_up, moe_w_down,
           ln_mix_g, ln_mix_b, ln_ffn_g, ln_ffn_b, ple_w_proj, ple_w_gate):
    B, S, D = x.shape
    T = B * S
    ts_front, ts_scan, tm, tm_big, tf = 512, 256, 512, 1024, 512
    tm_moe = 7 * LANES

    w_in = ab_w_in[0]
    n_main = 4 * A_DIM
    wm = jnp.concatenate([w_in[:, :n_main], w_in[:, n_main + 2 * N_HEADS:]], axis=1).astype(BF16)
    ws = _pad_lanes(w_in[:, n_main:n_main + 2 * N_HEADS]).astype(BF16)
    gp = jnp.concatenate([_pad_lanes(ab_a_log[0][None, :], N_HEADS),
                          _pad_lanes(ab_dt_bias[0][None, :], N_HEADS)], axis=0)
    dwp = jnp.stack([ab_dw_b[0], ab_cn_g[0], ab_cn_b[0]], axis=0)
    q, k, v, z, sc, sct, u = _ab_front(x, wm, ws, ab_conv_qkv[0], gp, ab_dw_w[0], dwp, ts_front)
    o_a = _gdn(q, k, v, z, sc, sct, ab_o_norm_g[0][None, :], ts_scan)
    w_out = ab_w_out[0].astype(BF16)
    lnp = jnp.stack([ln_mix_g[0], ln_mix_b[0]], axis=0)
    xf = x.reshape(T, D)
    x1 = _proj_ln([o_a.reshape(T, A_DIM), u.reshape(T, B_CH)], [w_out[:A_DIM], w_out[A_DIM:]],
                  xf, lnp, tm)
    lnp = jnp.stack([ln_ffn_g[0], ln_ffn_b[0]], axis=0)
    x2 = _ffn(x1, ffn_w_gate[0].astype(BF16), ffn_w_up[0].astype(BF16),
              ffn_w_down[0].astype(BF16), lnp, p[0].reshape(T, PLE_DIM),
              ple_w_gate[0].astype(BF16), ple_w_proj[0].astype(BF16), tm_big, tf)

    w_in = c_w_in[0]
    n_main = 2 * C_QK + 2 * C_V
    wm = w_in[:, :n_main].astype(BF16)
    ws = _pad_lanes(w_in[:, n_main:]).astype(BF16)
    gb = _pad_lanes(jnp.concatenate([c_b_i[0], c_b_f[0]])[None, :])
    q, k, v, op, sc, sct = _c_front(x2.reshape(B, S, D), wm, ws, gb, ts_front)
    hh = _mlstm(q, k, v, op, sc, sct, c_norm_g[0][None, :], ts_scan)
    lnp = jnp.stack([ln_mix_g[1], ln_mix_b[1]], axis=0)
    x3, x3_tok = _proj_ln([hh.reshape(T, C_V)], [c_w_out[0].astype(BF16)], x2, lnp, tm,
                          token_major_copy=True)

    wr = _pad_lanes(moe_w_router[0])
    wr_hi = wr.astype(BF16)
    wr_lo = (wr - wr_hi.astype(F32)).astype(BF16)
    info, counts = _router(x3, wr_hi, wr_lo, _pad_lanes(moe_b_router[0][None, :]), tm)
    src, dst, tile_expert, n_valid, n_real = _route_plan(info, counts, tm_moe)
    y = _moe_experts(tile_expert, n_valid, n_real, src, dst, x3_tok, moe_w_gate[0].astype(BF16),
                     moe_w_up[0].astype(BF16), moe_w_down[0].astype(BF16), 2 * T, tm_moe, tf)
    lnp = jnp.stack([ln_ffn_g[1], ln_ffn_b[1]], axis=0)
    x4 = _moe_combine(x3, info, y, lnp, p[1].reshape(T, PLE_DIM),
                      ple_w_gate[1].astype(BF16), ple_w_proj[1].astype(BF16), tm)
    return x4.reshape(B, S, D)
```

```python
import functools

import jax
import jax.numpy as jnp
from jax import lax
from jax.experimental import pallas as pl
from jax.experimental.pallas import tpu as pltpu

F32 = jnp.float32
BF16 = jnp.bfloat16

D_MODEL = 1024
N_HEADS = 4
CHUNK = 64
A_HEAD = 128
A_DIM = N_HEADS * A_HEAD
SHORT_CONV = 4
B_CH = 512
B_CONV = 31
C_QK_HEAD = 128
C_V_HEAD = 256
C_QK = N_HEADS * C_QK_HEAD
C_V = N_HEADS * C_V_HEAD
GATE_CAP = 15.0
D_FF = 3584
N_EXPERTS = 8
PLE_DIM = 256
DEPTH = 2
DN_ALPHA = (2 * DEPTH) ** 0.25
LANES = 128
SUBLANES = 8
QKV_HALO = 8
DW_HALO = 32
VMEM_LIMIT = 56 * 1024 * 1024


def _dot(a, b):
    return jnp.dot(a, b, preferred_element_type=F32)


def _dot_nt(a, b):
    return lax.dot_general(a, b, (((1,), (1,)), ((), ())), preferred_element_type=F32)


def _dot_tn(a, b):
    return lax.dot_general(a, b, (((0,), (0,)), ((), ())), preferred_element_type=F32)


def _sigmoid(t):
    return 1.0 / (1.0 + jnp.exp(-t))


def _silu(t):
    return t * _sigmoid(t)


def _softplus(t):
    return jnp.maximum(t, 0.0) + jnp.log1p(jnp.exp(-jnp.abs(t)))


def _layer_norm(t, g, b):
    mu = jnp.mean(t, axis=-1, keepdims=True)
    c = t - mu
    var = jnp.mean(c * c, axis=-1, keepdims=True)
    return c * lax.rsqrt(var + 1e-5) * g + b


def _chunk_cumsum(val):
    row = lax.broadcasted_iota(jnp.int32, val.shape, 0) & (CHUNK - 1)
    sh = 1
    while sh < CHUNK:
        val = val + jnp.where(row >= sh, pltpu.roll(val, sh, axis=0), 0.0)
        sh *= 2
    return val


def _params(*sem):
    return pltpu.CompilerParams(dimension_semantics=sem, vmem_limit_bytes=VMEM_LIMIT)


def _ab_front_kernel(x_ref, wm_ref, ws_ref, cw_ref, gp_ref, dww_ref, dwp_ref,
                     q_ref, k_ref, v_ref, z_ref, sc_ref, sct_ref, u_ref,
                     qkv_ext, u_ext, u_shift):
    s = pl.program_id(1)
    ts = x_ref.shape[1]
    xb = x_ref[0].astype(BF16)

    @pl.when(s == 0)
    def _():
        qkv_ext[0:QKV_HALO, :] = jnp.zeros((QKV_HALO, 3 * A_DIM), F32)
        u_ext[0:DW_HALO, :] = jnp.zeros((DW_HALO, B_CH), F32)

    @pl.when(s > 0)
    def _():
        qkv_ext[0:QKV_HALO, :] = qkv_ext[ts:ts + QKV_HALO, :]
        u_ext[0:DW_HALO, :] = u_ext[ts:ts + DW_HALO, :]

    qkv_ext[QKV_HALO:QKV_HALO + ts, :] = _dot(xb, wm_ref[:, 0:3 * A_DIM])
    acc = None
    for j in range(SHORT_CONV):
        off = QKV_HALO - (SHORT_CONV - 1) + j
        term = cw_ref[j:j + 1, :] * qkv_ext[off:off + ts, :]
        acc = term if acc is None else acc + term
    qkv = _silu(acc)
    for h in range(N_HEADS):
        lo, hi = h * A_HEAD, (h + 1) * A_HEAD
        qh = qkv[:, lo:hi]
        kh = qkv[:, A_DIM + lo:A_DIM + hi]
        qn = lax.rsqrt(jnp.sum(qh * qh, axis=-1, keepdims=True) + 1e-6)
        kn = lax.rsqrt(jnp.sum(kh * kh, axis=-1, keepdims=True) + 1e-6)
        q_ref[0, :, lo:hi] = qh * qn * (A_HEAD ** -0.5)
        k_ref[0, :, lo:hi] = kh * kn
    v_ref[0] = qkv[:, 2 * A_DIM:3 * A_DIM]
    z_ref[0] = _dot(xb, wm_ref[:, 3 * A_DIM:4 * A_DIM])

    hs = _dot(xb, ws_ref[...])
    lane = lax.broadcasted_iota(jnp.int32, hs.shape, 1)
    beta = _sigmoid(hs)
    g = -jnp.exp(gp_ref[0:1, :]) * _softplus(hs + gp_ref[1:2, :])
    sc = jnp.where(lane < N_HEADS, beta, _chunk_cumsum(g))
    sc_ref[0] = sc
    sct_ref[0] = jnp.transpose(sc)[0:8, :]

    glu = _dot(xb, wm_ref[:, 4 * A_DIM:4 * A_DIM + 2 * B_CH])
    u_ext[DW_HALO:DW_HALO + ts, :] = glu[:, 0:B_CH] * _sigmoid(glu[:, B_CH:2 * B_CH])
    first_off = DW_HALO - (B_CONV - 1)
    span = ts + DW_HALO - 8
    acc = None
    for res in range(8):
        offs = [o for o in range(first_off, DW_HALO + 1) if o % 8 == res]
        if res:
            u_shift[...] = u_ext[res:res + span, :]
        src = u_shift if res else u_ext
        for off in offs:
            lo = off - res if res else off
            term = dww_ref[off - first_off:off - first_off + 1, :] * src[lo:lo + ts, :]
            acc = term if acc is None else acc + term
    u = acc + dwp_ref[0:1, :]
    u = _layer_norm(u, dwp_ref[1:2, :], dwp_ref[2:3, :])
    u_ref[0] = _silu(u).astype(BF16)


def _ab_front(x, wm, ws, cw, gp, dww, dwp, ts):
    B, S, D = x.shape
    grid = (B, S // ts)
    tok = lambda w: pl.BlockSpec((1, ts, w), lambda b, s: (b, s, 0))
    full = lambda a: pl.BlockSpec(a.shape, lambda b, s: (0,) * a.ndim)
    out_shape = (
        jax.ShapeDtypeStruct((B, S, A_DIM), F32),
        jax.ShapeDtypeStruct((B, S, A_DIM), F32),
        jax.ShapeDtypeStruct((B, S, A_DIM), F32),
        jax.ShapeDtypeStruct((B, S, A_DIM), F32),
        jax.ShapeDtypeStruct((B, S, LANES), F32),
        jax.ShapeDtypeStruct((B, 8, S), F32),
        jax.ShapeDtypeStruct((B, S, B_CH), BF16),
    )
    out_specs = (tok(A_DIM), tok(A_DIM), tok(A_DIM), tok(A_DIM), tok(LANES),
                 pl.BlockSpec((1, 8, ts), lambda b, s: (b, 0, s)), tok(B_CH))
    return pl.pallas_call(
        _ab_front_kernel,
        grid=grid,
        in_specs=[tok(D), full(wm), full(ws), full(cw), full(gp), full(dww), full(dwp)],
        out_specs=out_specs,
        out_shape=out_shape,
        scratch_shapes=[pltpu.VMEM((ts + QKV_HALO, 3 * A_DIM), F32),
                        pltpu.VMEM((ts + DW_HALO, B_CH), F32),
                        pltpu.VMEM((ts + DW_HALO - 8, B_CH), F32)],
        compiler_params=_params("arbitrary", "arbitrary"),
        name="ab_front",
    )(x, wm, ws, cw, gp, dww, dwp)


def _gdn_kernel(q_ref, k_ref, v_ref, z_ref, sc_ref, sct_ref, ng_ref, o_ref, st_ref):
    s = pl.program_id(1)
    ts = q_ref.shape[1]

    @pl.when(s == 0)
    def _():
        st_ref[...] = jnp.zeros(st_ref.shape, F32)

    row = lax.broadcasted_iota(jnp.int32, (CHUNK, CHUNK), 0)
    col = lax.broadcasted_iota(jnp.int32, (CHUNK, CHUNK), 1)
    causal = row >= col
    strict = row > col
    eye = jnp.where(row == col, 1.0, 0.0).astype(F32)

    n_chunks = ts // CHUNK
    pairs = [(c, h) for c in range(n_chunks) for h in range(N_HEADS)]
    pw, tinv, attn, rhs, k_dec, q_dec, d_last = [], [], [], [], [], [], []
    for c, h in pairs:
        r0, r1 = c * CHUNK, (c + 1) * CHUNK
        lo, hi = h * A_HEAD, (h + 1) * A_HEAD
        qh = q_ref[0, r0:r1, lo:hi]
        kh = k_ref[0, r0:r1, lo:hi]
        vh = v_ref[0, r0:r1, lo:hi]
        beta = sc_ref[0, r0:r1, h:h + 1]
        gc = sc_ref[0, r0:r1, N_HEADS + h:N_HEADS + h + 1]
        gc_row = sct_ref[0, N_HEADS + h:N_HEADS + h + 1, r0:r1]
        gc_last = gc[CHUNK - 1:CHUNK, :]
        decay = jnp.where(causal, jnp.exp(jnp.where(causal, gc - gc_row, 0.0)), 0.0)
        kb = kh * beta
        egc = jnp.exp(gc)
        kq = _dot_nt(jnp.concatenate([kb, qh], axis=0).astype(BF16), kh.astype(BF16))
        a_neg = jnp.where(strict, -kq[0:CHUNK] * decay, 0.0)
        pw.append(a_neg)
        tinv.append(eye + a_neg)
        attn.append((kq[CHUNK:2 * CHUNK] * decay).astype(BF16))
        rhs.append(jnp.concatenate([vh * beta, kb * egc], axis=1).astype(BF16))
        k_dec.append((kh * jnp.exp(gc_last - gc)).astype(BF16))
        q_dec.append(qh * egc)
        d_last.append(jnp.exp(gc_last))
    head_blk = lax.broadcasted_iota(jnp.int32, (CHUNK, N_HEADS * CHUNK), 1) // CHUNK

    def block_diag(m):
        return jnp.concatenate([jnp.where(head_blk == h, m, 0.0) for h in range(N_HEADS)],
                               axis=0).astype(BF16)

    side_by_side = lambda mats, c: jnp.concatenate(mats[c * N_HEADS:(c + 1) * N_HEADS], axis=1)
    qc = [side_by_side(pw, c) for c in range(n_chunks)]
    yc = [side_by_side(tinv, c) for c in range(n_chunks)]
    qc = [_dot(qc[c].astype(BF16), block_diag(qc[c])) for c in range(n_chunks)]
    for _ in range(4):
        both = [_dot(jnp.concatenate([yc[c], qc[c]], axis=0).astype(BF16), block_diag(qc[c]))
                for c in range(n_chunks)]
        yc = [yc[c] + both[c][0:CHUNK] for c in range(n_chunks)]
        qc = [both[c][CHUNK:2 * CHUNK] for c in range(n_chunks)]
    last = [_dot(yc[c].astype(BF16), block_diag(qc[c])) for c in range(n_chunks)]
    yc = [yc[c] + last[c] for c in range(n_chunks)]
    tinv = [yc[c][:, h * CHUNK:(h + 1) * CHUNK] for c, h in pairs]
    uw = [_dot(tinv[i].astype(BF16), rhs[i]).astype(BF16) for i in range(len(pairs))]
    auw = [_dot(attn[i], uw[i]) for i in range(len(pairs))]
    kuw = [_dot_tn(k_dec[i], uw[i]) for i in range(len(pairs))]
    st = [st_ref[h] for h in range(N_HEADS)]
    for i, (c, h) in enumerate(pairs):
        r0, r1 = c * CHUNK, (c + 1) * CHUNK
        lo, hi = h * A_HEAD, (h + 1) * A_HEAD
        stb = st[h].astype(BF16)
        q_eff = (q_dec[i] - auw[i][:, A_HEAD:]).astype(BF16)
        o = _dot(q_eff, stb) + auw[i][:, 0:A_HEAD]
        st[h] = (d_last[i] * st[h] + kuw[i][:, 0:A_HEAD]
                 - _dot(kuw[i][:, A_HEAD:].astype(BF16), stb))
        o = o * lax.rsqrt(jnp.mean(o * o, axis=-1, keepdims=True) + 1e-6) * ng_ref[0:1, :]
        o_ref[0, r0:r1, lo:hi] = (o * _silu(z_ref[0, r0:r1, lo:hi])).astype(BF16)
    for h in range(N_HEADS):
        st_ref[h] = st[h]


def _gdn(q, k, v, z, sc, sct, ng, ts):
    B, S, _ = q.shape
    tok = lambda w: pl.BlockSpec((1, ts, w), lambda b, s: (b, s, 0))
    return pl.pallas_call(
        _gdn_kernel,
        grid=(B, S // ts),
        in_specs=[tok(A_DIM), tok(A_DIM), tok(A_DIM), tok(A_DIM), tok(LANES),
                  pl.BlockSpec((1, 8, ts), lambda b, s: (b, 0, s)),
                  pl.BlockSpec(ng.shape, lambda b, s: (0, 0))],
        out_specs=tok(A_DIM),
        out_shape=jax.ShapeDtypeStruct((B, S, A_DIM), BF16),
        scratch_shapes=[pltpu.VMEM((N_HEADS, A_HEAD, A_HEAD), F32)],
        compiler_params=_params("arbitrary", "arbitrary"),
        name="gdn_scan",
    )(q, k, v, z, sc, sct, ng)


def _token_rows(s, n_tokens):
    return pl.ds(s, n_tokens, stride=D_MODEL // LANES)


def _proj_ln_kernel(*refs, n_in, token_major_copy):
    a_refs = refs[0:n_in]
    w_refs = refs[n_in:2 * n_in]
    x_ref, lnp_ref, o_ref = refs[2 * n_in:2 * n_in + 3]
    mix = None
    for a_ref, w_ref in zip(a_refs, w_refs):
        t = _dot(a_ref[...], w_ref[...])
        mix = t if mix is None else mix + t
    y = _layer_norm(DN_ALPHA * x_ref[...] + mix, lnp_ref[0:1, :], lnp_ref[1:2, :])
    o_ref[...] = y
    if token_major_copy:
        ot_ref = refs[2 * n_in + 3]
        for s in range(D_MODEL // LANES):
            ot_ref[_token_rows(s, y.shape[0]), :] = y[:, s * LANES:(s + 1) * LANES]


def _proj_ln(acts, weights, x, lnp, tm, token_major_copy=False):
    T, D = x.shape
    n_in = len(acts)
    in_specs = ([pl.BlockSpec((tm, a.shape[1]), lambda i: (i, 0)) for a in acts]
                + [pl.BlockSpec(w.shape, lambda i: (0, 0)) for w in weights]
                + [pl.BlockSpec((tm, D), lambda i: (i, 0)),
                   pl.BlockSpec(lnp.shape, lambda i: (0, 0))])
    out_specs = [pl.BlockSpec((tm, D), lambda i: (i, 0))]
    out_shape = [jax.ShapeDtypeStruct((T, D), F32)]
    if token_major_copy:
        out_specs.append(pl.BlockSpec((tm * D // LANES, LANES), lambda i: (i, 0)))
        out_shape.append(jax.ShapeDtypeStruct((T * D // LANES, LANES), F32))
    outs = pl.pallas_call(
        functools.partial(_proj_ln_kernel, n_in=n_in, token_major_copy=token_major_copy),
        grid=(T // tm,),
        in_specs=in_specs,
        out_specs=tuple(out_specs),
        out_shape=tuple(out_shape),
        compiler_params=_params("arbitrary"),
        name="proj_ln",
    )(*acts, *weights, x, lnp)
    return outs if token_major_copy else outs[0]


def _ffn_epilogue(x, acc, lnp_ref, p_ref, wpg_ref, wpp_ref, o_ref):
    y = _layer_norm(DN_ALPHA * x + acc, lnp_ref[0:1, :], lnp_ref[1:2, :])
    gate = _sigmoid(_dot(y.astype(BF16), wpg_ref[...]))
    o_ref[...] = y + gate * _dot(p_ref[...].astype(BF16), wpp_ref[...])


def _ffn_kernel(x_ref, wg_ref, wu_ref, wd_ref, lnp_ref, p_ref, wpg_ref, wpp_ref,
                o_ref, xb_ref):
    j = pl.program_id(1)

    @pl.when(j == 0)
    def _():
        xb_ref[...] = x_ref[...].astype(BF16)
        o_ref[...] = jnp.zeros(o_ref.shape, F32)

    xb = xb_ref[...]
    hid = _silu(_dot(xb, wg_ref[...])) * _dot(xb, wu_ref[...])
    o_ref[...] += _dot(hid.astype(BF16), wd_ref[...])

    @pl.when(j == pl.num_programs(1) - 1)
    def _():
        _ffn_epilogue(x_ref[...], o_ref[...], lnp_ref, p_ref, wpg_ref, wpp_ref, o_ref)


def _ffn(x, wg, wu, wd, lnp, p, wpg, wpp, tm, tf):
    T, D = x.shape
    F = wg.shape[1]
    return pl.pallas_call(
        _ffn_kernel,
        grid=(T // tm, F // tf),
        in_specs=[pl.BlockSpec((tm, D), lambda i, j: (i, 0)),
                  pl.BlockSpec((D, tf), lambda i, j: (0, j)),
                  pl.BlockSpec((D, tf), lambda i, j: (0, j)),
                  pl.BlockSpec((tf, D), lambda i, j: (j, 0)),
                  pl.BlockSpec(lnp.shape, lambda i, j: (0, 0)),
                  pl.BlockSpec((tm, PLE_DIM), lambda i, j: (i, 0)),
                  pl.BlockSpec(wpg.shape, lambda i, j: (0, 0)),
                  pl.BlockSpec(wpp.shape, lambda i, j: (0, 0))],
        out_specs=pl.BlockSpec((tm, D), lambda i, j: (i, 0)),
        out_shape=jax.ShapeDtypeStruct((T, D), F32),
        scratch_shapes=[pltpu.VMEM((tm, D), BF16)],
        compiler_params=_params("arbitrary", "arbitrary"),
        name="ffn",
    )(x, wg, wu, wd, lnp, p, wpg, wpp)


ROUTE_IDX1, ROUTE_IDX2, ROUTE_W1, ROUTE_W2, ROUTE_RANK1, ROUTE_RANK2 = range(6)


def _router_kernel(x_ref, wh_ref, wl_ref, br_ref, info_ref, info_t_ref, cnt_ref, carry_ref):
    i = pl.program_id(0)

    @pl.when(i == 0)
    def _():
        carry_ref[...] = jnp.zeros(carry_ref.shape, F32)

    x = x_ref[...]
    tm = x.shape[0]
    xh = x.astype(BF16)
    xl = (x - xh.astype(F32)).astype(BF16)
    logits = (_dot(xh, wh_ref[...]) + _dot(xl, wh_ref[...]) + _dot(xh, wl_ref[...])
              + br_ref[0:1, :])
    lane = lax.broadcasted_iota(jnp.int32, logits.shape, 1)
    neg = jnp.float32(-jnp.inf)
    logits = jnp.where(lane < N_EXPERTS, logits, neg)
    m1 = jnp.max(logits, axis=-1, keepdims=True)
    i1 = jnp.min(jnp.where(logits == m1, lane, LANES), axis=-1, keepdims=True)
    sel1 = lane == i1
    rest = jnp.where(sel1, neg, logits)
    m2 = jnp.max(rest, axis=-1, keepdims=True)
    i2 = jnp.min(jnp.where(rest == m2, lane, LANES), axis=-1, keepdims=True)
    sel2 = lane == i2
    e2 = jnp.exp(m2 - m1)
    den = 1.0 + e2

    sel = jnp.where(jnp.logical_or(sel1, sel2), 1.0, 0.0)
    row = lax.broadcasted_iota(jnp.int32, (tm, tm), 0)
    col = lax.broadcasted_iota(jnp.int32, (tm, tm), 1)
    earlier = jnp.where(row > col, 1.0, 0.0).astype(BF16)
    rank = _dot(earlier, sel.astype(BF16)) + carry_ref[0:1, :]
    rank1 = jnp.sum(jnp.where(sel1, rank, 0.0), axis=-1, keepdims=True)
    rank2 = jnp.sum(jnp.where(sel2, rank, 0.0), axis=-1, keepdims=True)

    info = jnp.zeros(logits.shape, F32)
    for slot, val in ((ROUTE_IDX1, i1.astype(F32)), (ROUTE_IDX2, i2.astype(F32)),
                      (ROUTE_W1, 1.0 / den), (ROUTE_W2, e2 / den),
                      (ROUTE_RANK1, rank1), (ROUTE_RANK2, rank2)):
        info = jnp.where(lane == slot, val, info)
    info_ref[...] = info
    info_t_ref[...] = jnp.transpose(info)[0:8, :]
    carry_ref[0:1, :] = carry_ref[0:1, :] + jnp.sum(sel, axis=0, keepdims=True)
    cnt_ref[...] = jnp.broadcast_to(carry_ref[0:1, :], cnt_ref.shape)


def _router(x, wh, wl, br, tm):
    T, D = x.shape
    return pl.pallas_call(
        _router_kernel,
        grid=(T // tm,),
        in_specs=[pl.BlockSpec((tm, D), lambda i: (i, 0)),
                  pl.BlockSpec(wh.shape, lambda i: (0, 0)),
                  pl.BlockSpec(wl.shape, lambda i: (0, 0)),
                  pl.BlockSpec(br.shape, lambda i: (0, 0))],
        out_specs=(pl.BlockSpec((tm, LANES), lambda i: (i, 0)),
                   pl.BlockSpec((8, tm), lambda i: (0, i)),
                   pl.BlockSpec((8, LANES), lambda i: (0, 0))),
        out_shape=(jax.ShapeDtypeStruct((T, LANES), F32),
                   jax.ShapeDtypeStruct((8, T), F32),
                   jax.ShapeDtypeStruct((8, LANES), F32)),
        scratch_shapes=[pltpu.VMEM((8, LANES), F32)],
        compiler_params=_params("arbitrary"),
        name="router",
    )(x, wh, wl, br)


def _stage_indices(idx_vmem, idx_smem, sem):
    cp = pltpu.make_async_copy(idx_vmem, idx_smem, sem)
    cp.start()
    cp.wait()


def _moe_expert_kernel(te_ref, nv_ref, nr_ref, src_ref, dst_ref, x_hbm, wg_ref, wu_ref, wd_ref, y_hbm,
                       xg_ref, xb_ref, acc_ref, yo_ref, gidx_ref, sidx_ref,
                       sem_idx, sem_g, sem_s, *, n_j):
    s = pl.program_id(0)
    j = pl.program_id(1)
    tm = xb_ref.shape[0]
    rows_per_step = tm // n_j
    n_blk = D_MODEL // LANES
    nv = nv_ref[0]
    gather_on = s < nv
    compute_on = jnp.logical_and(s >= 1, s - 1 < nv)
    scatter_on = jnp.logical_and(s >= 2, s - 2 < nv)
    slot_cur = lax.rem(s, 2)
    slot_prev = 1 - slot_cur

    def token_tile(first_row):
        return pl.ds(pl.multiple_of(first_row, n_blk), n_blk)

    def gather_copy(slot, r, src_row):
        return pltpu.make_async_copy(x_hbm.at[token_tile(src_row), :],
                                     xg_ref.at[slot, token_tile(r * n_blk), :], sem_g)

    def scatter_copy(slot, r, dst_row):
        return pltpu.make_async_copy(yo_ref.at[slot, token_tile(r * n_blk), :],
                                     y_hbm.at[token_tile(dst_row), :], sem_s)

    def real_rows(tile):
        return nr_ref[jnp.clip(tile, 0, nr_ref.shape[0] - 1)]

    def drain(copy_of_row, n_rows=None):
        def body(g, carry):
            r0 = pl.multiple_of(g * SUBLANES, SUBLANES)
            for u in range(SUBLANES):
                copy_of_row(r0 + u).wait()
            return carry

        def one(r, carry):
            copy_of_row(r).wait()
            return carry

        def all_rows():
            lax.fori_loop(0, tm // SUBLANES, body, 0)

        def some_rows():
            lax.fori_loop(0, n_rows, one, 0)

        if n_rows is None:
            all_rows()
        else:
            pl.when(n_rows == tm)(all_rows)
            pl.when(n_rows < tm)(some_rows)

    @pl.when(j == 0)
    def _():
        @pl.when(compute_on)
        def _():
            drain(lambda r: gather_copy(slot_prev, r, 0))
            for blk in range(n_blk):
                xb_ref[:, blk * LANES:(blk + 1) * LANES] = (
                    xg_ref[slot_prev, _token_rows(blk, tm), :].astype(BF16))
            acc_ref[...] = jnp.zeros(acc_ref.shape, F32)

        @pl.when(jnp.logical_and(s >= 3, s - 3 < nv))
        def _():
            drain(lambda r: scatter_copy(slot_prev, r, 0), real_rows(s - 3))

        @pl.when(gather_on)
        def _():
            _stage_indices(src_ref.at[0], gidx_ref, sem_idx)

        @pl.when(scatter_on)
        def _():
            _stage_indices(dst_ref.at[0], sidx_ref, sem_idx)

    base = pl.multiple_of(j * rows_per_step, rows_per_step)

    def issue_gathers():
        for u in range(rows_per_step):
            gather_copy(slot_cur, base + u, gidx_ref[0, base + u]).start()

    def issue_scatters():
        n_real = real_rows(s - 2)

        @pl.when(base + rows_per_step <= n_real)
        def _():
            for u in range(rows_per_step):
                scatter_copy(slot_cur, base + u, sidx_ref[0, base + u]).start()

        @pl.when(base + rows_per_step > n_real)
        def _():
            def one(r, carry):
                scatter_copy(slot_cur, r, sidx_ref[0, r]).start()
                return carry
            lax.fori_loop(base, jnp.maximum(base, jnp.minimum(base + rows_per_step, n_real)),
                          one, 0)

    def compute():
        xb = xb_ref[...]
        hid = _silu(_dot(xb, wg_ref[0])) * _dot(xb, wu_ref[0])
        acc_ref[...] += _dot(hid.astype(BF16), wd_ref[0])

    pl.when(gather_on)(issue_gathers)
    pl.when(compute_on)(compute)
    pl.when(scatter_on)(issue_scatters)

    last_j = j == n_j - 1

    @pl.when(jnp.logical_and(last_j, compute_on))
    def _():
        for blk in range(n_blk):
            yo_ref[slot_prev, _token_rows(blk, tm), :] = acc_ref[:, blk * LANES:(blk + 1) * LANES]

    @pl.when(jnp.logical_and(jnp.logical_and(last_j, s == pl.num_programs(0) - 1), scatter_on))
    def _():
        drain(lambda r: scatter_copy(slot_cur, r, 0), real_rows(s - 2))


def _moe_experts(tile_expert, n_valid, n_real, src_rows, dst_rows, x, wg, wu, wd, n_out_rows,
                 tm, tf):
    E, D, F = wg.shape
    n_tiles = src_rows.shape[0]
    n_j = F // tf
    assert tm % (n_j * SUBLANES) == 0

    def computing(s, nv):
        return jnp.logical_and(s >= 1, s - 1 < nv[0])

    def tile_of(s):
        return jnp.clip(s - 1, 0, n_tiles - 1)

    def w_map(s, j, te, nv, nr):
        return (te[tile_of(s)], 0, jnp.where(computing(s, nv), j, 0))

    def wd_map(s, j, te, nv, nr):
        return (te[tile_of(s)], jnp.where(computing(s, nv), j, 0), 0)

    grid_spec = pltpu.PrefetchScalarGridSpec(
        num_scalar_prefetch=3,
        grid=(n_tiles + 2, n_j),
        in_specs=[pl.BlockSpec((1, 1, tm),
                               lambda s, j, te, nv, nr: (jnp.minimum(s, n_tiles - 1), 0, 0)),
                  pl.BlockSpec((1, 1, tm),
                               lambda s, j, te, nv, nr: (jnp.clip(s - 2, 0, n_tiles - 1), 0, 0)),
                  pl.BlockSpec(memory_space=pl.ANY),
                  pl.BlockSpec((1, D, tf), w_map),
                  pl.BlockSpec((1, D, tf), w_map),
                  pl.BlockSpec((1, tf, D), wd_map)],
        out_specs=pl.BlockSpec(memory_space=pl.ANY),
        scratch_shapes=[pltpu.VMEM((2, tm * D // LANES, LANES), F32),
                        pltpu.VMEM((tm, D), BF16),
                        pltpu.VMEM((tm, D), F32),
                        pltpu.VMEM((2, tm * D // LANES, LANES), F32),
                        pltpu.SMEM((1, tm), jnp.int32), pltpu.SMEM((1, tm), jnp.int32),
                        pltpu.SemaphoreType.DMA(()), pltpu.SemaphoreType.DMA(()),
                        pltpu.SemaphoreType.DMA(())],
    )
    return pl.pallas_call(
        functools.partial(_moe_expert_kernel, n_j=n_j),
        grid_spec=grid_spec,
        out_shape=jax.ShapeDtypeStruct((n_out_rows * D // LANES, LANES), F32),
        compiler_params=_params("arbitrary", "arbitrary"),
        name="moe_experts",
    )(tile_expert, n_valid, n_real, src_rows, dst_rows, x, wg, wu, wd)


def _moe_combine_kernel(x_ref, info_ref, ya_ref, yb_ref, lnp_ref, p_ref, wpg_ref, wpp_ref, o_ref):
    info = info_ref[...]
    tm = x_ref.shape[0]
    w1 = info[:, ROUTE_W1:ROUTE_W1 + 1]
    w2 = info[:, ROUTE_W2:ROUTE_W2 + 1]
    mix = jnp.concatenate(
        [w1 * ya_ref[_token_rows(blk, tm), :] + w2 * yb_ref[_token_rows(blk, tm), :]
         for blk in range(D_MODEL // LANES)], axis=1)
    _ffn_epilogue(x_ref[...], mix, lnp_ref, p_ref, wpg_ref, wpp_ref, o_ref)


def _moe_combine(x, info, y, lnp, p, wpg, wpp, tm):
    T, D = x.shape
    y_block = (tm * D // LANES, LANES)
    return pl.pallas_call(
        _moe_combine_kernel,
        grid=(T // tm,),
        in_specs=[pl.BlockSpec((tm, D), lambda i: (i, 0)),
                  pl.BlockSpec((tm, LANES), lambda i: (i, 0)),
                  pl.BlockSpec(y_block, lambda i: (i, 0)),
                  pl.BlockSpec(y_block, lambda i: (T // tm + i, 0)),
                  pl.BlockSpec(lnp.shape, lambda i: (0, 0)),
                  pl.BlockSpec((tm, PLE_DIM), lambda i: (i, 0)),
                  pl.BlockSpec(wpg.shape, lambda i: (0, 0)),
                  pl.BlockSpec(wpp.shape, lambda i: (0, 0))],
        out_specs=pl.BlockSpec((tm, D), lambda i: (i, 0)),
        out_shape=jax.ShapeDtypeStruct((T, D), F32),
        compiler_params=_params("arbitrary"),
        name="moe_combine",
    )(x, info, y, y, lnp, p, wpg, wpp)


def _route_plan(info_t, counts, tm):
    T = info_t.shape[1]
    n_tiles = (2 * T + N_EXPERTS * (tm - 1)) // tm + 1
    n_rows = n_tiles * tm
    as_int = lambda col: info_t[col].astype(jnp.int32)
    idx1, idx2 = as_int(ROUTE_IDX1), as_int(ROUTE_IDX2)
    counts = counts[0, :N_EXPERTS].astype(jnp.int32)
    padded = ((counts + tm - 1) // tm) * tm
    ends = jnp.cumsum(padded)
    starts = ends - padded
    grouped_row = jnp.concatenate([starts[idx1] + as_int(ROUTE_RANK1),
                                   starts[idx2] + as_int(ROUTE_RANK2)])
    out_row = jnp.arange(2 * T, dtype=jnp.int32)
    owner = jnp.full((n_rows,), -1, jnp.int32).at[grouped_row].set(out_row)
    real = owner >= 0
    dst = jnp.maximum(owner, 0)
    src = jnp.where(real, jnp.where(owner >= T, owner - T, owner), 0)
    n_real = jnp.sum(real.reshape(n_tiles, tm), axis=1).astype(jnp.int32)
    tile_start = jnp.arange(n_tiles, dtype=jnp.int32) * tm
    tile_expert = jnp.minimum(jnp.sum(tile_start[:, None] >= ends[None, :], axis=1),
                              N_EXPERTS - 1).astype(jnp.int32)
    n_valid = (ends[-1] // tm).astype(jnp.int32).reshape(1)
    shape = (n_tiles, 1, tm)
    n_blk = D_MODEL // LANES
    return (src * n_blk).reshape(shape), (dst * n_blk).reshape(shape), tile_expert, n_valid, n_real


def _c_front_kernel(x_ref, wm_ref, ws_ref, gb_ref, q_ref, k_ref, v_ref, op_ref, sc_ref, sct_ref):
    xb = x_ref[0].astype(BF16)
    q_ref[0] = _dot(xb, wm_ref[:, 0:C_QK]) * (C_QK_HEAD ** -0.5)
    k_ref[0] = _dot(xb, wm_ref[:, C_QK:2 * C_QK])
    v_ref[0] = _dot(xb, wm_ref[:, 2 * C_QK:2 * C_QK + C_V])
    op_ref[0] = _dot(xb, wm_ref[:, 2 * C_QK + C_V:2 * C_QK + 2 * C_V])
    hs = _dot(xb, ws_ref[...]) + gb_ref[0:1, :]
    capped = GATE_CAP * jnp.tanh(hs / GATE_CAP)
    lane = lax.broadcasted_iota(jnp.int32, hs.shape, 1)
    log_f = jnp.minimum(capped, 0.0) - jnp.log1p(jnp.exp(-jnp.abs(capped)))
    sc = jnp.where(lane < N_HEADS, capped, _chunk_cumsum(log_f))
    sc_ref[0] = sc
    sct_ref[0] = jnp.transpose(sc)[0:8, :]


def _c_front(x, wm, ws, gb, ts):
    B, S, D = x.shape
    tok = lambda w: pl.BlockSpec((1, ts, w), lambda b, s: (b, s, 0))
    full = lambda a: pl.BlockSpec(a.shape, lambda b, s: (0,) * a.ndim)
    out_shape = (
        jax.ShapeDtypeStruct((B, S, C_QK), F32),
        jax.ShapeDtypeStruct((B, S, C_QK), F32),
        jax.ShapeDtypeStruct((B, S, C_V), F32),
        jax.ShapeDtypeStruct((B, S, C_V), F32),
        jax.ShapeDtypeStruct((B, S, LANES), F32),
        jax.ShapeDtypeStruct((B, 8, S), F32),
    )
    out_specs = (tok(C_QK), tok(C_QK), tok(C_V), tok(C_V), tok(LANES),
                 pl.BlockSpec((1, 8, ts), lambda b, s: (b, 0, s)))
    return pl.pallas_call(
        _c_front_kernel,
        grid=(B, S // ts),
        in_specs=[tok(D), full(wm), full(ws), full(gb)],
        out_specs=out_specs,
        out_shape=out_shape,
        compiler_params=_params("arbitrary", "arbitrary"),
        name="c_front",
    )(x, wm, ws, gb)


def _mlstm_kernel(q_ref, k_ref, v_ref, op_ref, sc_ref, sct_ref, ng_ref, o_ref,
                  c_ref, n_ref, m_ref):
    s = pl.program_id(1)
    ts = q_ref.shape[1]

    @pl.when(s == 0)
    def _():
        c_ref[...] = jnp.zeros(c_ref.shape, F32)
        n_ref[...] = jnp.zeros(n_ref.shape, F32)
        m_ref[...] = jnp.zeros(m_ref.shape, F32)

    row = lax.broadcasted_iota(jnp.int32, (CHUNK, CHUNK), 0)
    col = lax.broadcasted_iota(jnp.int32, (CHUNK, CHUNK), 1)
    causal = row >= col
    neg = jnp.float32(-jnp.inf)

    n_chunks = ts // CHUNK
    pairs = [(c, h) for c in range(n_chunks) for h in range(N_HEADS)]
    n = len(pairs)
    each = lambda f: [f(i) for i in range(n)]
    rows = lambda i: slice(pairs[i][0] * CHUNK, (pairs[i][0] + 1) * CHUNK)
    kcols = lambda i: slice(pairs[i][1] * C_QK_HEAD, (pairs[i][1] + 1) * C_QK_HEAD)
    vcols = lambda i: slice(pairs[i][1] * C_V_HEAD, (pairs[i][1] + 1) * C_V_HEAD)
    head = lambda i: pairs[i][1]

    qs = each(lambda i: q_ref[0, rows(i), kcols(i)])
    ks = each(lambda i: k_ref[0, rows(i), kcols(i)])
    qbs = each(lambda i: qs[i].astype(BF16))
    vbs = each(lambda i: v_ref[0, rows(i), vcols(i)].astype(BF16))
    qk = each(lambda i: _dot_nt(qbs[i], ks[i].astype(BF16)))
    lis = each(lambda i: sc_ref[0, rows(i), head(i):head(i) + 1])
    bcs = each(lambda i: sc_ref[0, rows(i), N_HEADS + head(i):N_HEADS + head(i) + 1])
    b_lasts = each(lambda i: bcs[i][CHUNK - 1:CHUNK, :])
    bc_wide = each(lambda i: jnp.broadcast_to(bcs[i], (CHUNK, CHUNK)))
    dmat = each(lambda i: jnp.where(
        causal,
        bc_wide[i] - sct_ref[0, N_HEADS + head(i):N_HEADS + head(i) + 1, rows(i)]
        + sct_ref[0, head(i):head(i) + 1, rows(i)], neg))
    m_intras = each(lambda i: jnp.max(dmat[i], axis=-1, keepdims=True))
    pmat = each(lambda i: jnp.exp(dmat[i] - m_intras[i]) * qk[i])
    dens = each(lambda i: jnp.sum(pmat[i], axis=-1, keepdims=True))
    num_intra = each(lambda i: _dot(pmat[i].astype(BF16), vbs[i]))
    g_kvs = each(lambda i: b_lasts[i] - bcs[i] + lis[i])
    m_kvs = each(lambda i: jnp.max(g_kvs[i], axis=0, keepdims=True))

    m_run = [m_ref[h, :, 0:1] for h in range(N_HEADS)]
    m_sts, m_news = [], []
    for i in range(n):
        m_sts.append(m_run[head(i)])
        m_run[head(i)] = jnp.maximum(b_lasts[i] + m_run[head(i)], m_kvs[i])
        m_news.append(m_run[head(i)])
    decs = each(lambda i: jnp.exp(b_lasts[i] + m_sts[i] - m_news[i]))
    k_scale = each(lambda i: jnp.exp(g_kvs[i] - m_news[i]))
    kws = each(lambda i: ks[i] * k_scale[i])
    kv = each(lambda i: _dot_tn(kws[i].astype(BF16), vbs[i]))
    k_sum = each(lambda i: jnp.sum(kws[i], axis=0, keepdims=True))
    dec_c = each(lambda i: jnp.broadcast_to(decs[i], (C_QK_HEAD, C_V_HEAD)))
    c_run = [c_ref[h] for h in range(N_HEADS)]
    n_run = [n_ref[h] for h in range(N_HEADS)]
    c_sts, n_sts = [], []
    for i in range(n):
        c_sts.append(c_run[head(i)].astype(BF16))
        n_sts.append(n_run[head(i)])
        c_run[head(i)] = dec_c[i] * c_run[head(i)] + kv[i]
        n_run[head(i)] = decs[i] * n_run[head(i)] + k_sum[i]
    for h in range(N_HEADS):
        c_ref[h] = c_run[h]
        n_ref[h] = n_run[h]
        m_ref[h] = jnp.broadcast_to(m_run[h], (1, LANES))

    q_c = each(lambda i: _dot(qbs[i], c_sts[i]))
    q_n = each(lambda i: jnp.sum(qs[i] * n_sts[i], axis=-1, keepdims=True))
    inter = each(lambda i: bcs[i] + m_sts[i])
    m_t = each(lambda i: jnp.maximum(inter[i], m_intras[i]))
    s_inter = each(lambda i: jnp.exp(inter[i] - m_t[i]))
    s_intra = each(lambda i: jnp.exp(m_intras[i] - m_t[i]))
    den = each(lambda i: jnp.maximum(jnp.abs(s_inter[i] * q_n[i] + s_intra[i] * dens[i]),
                                     jnp.exp(-m_t[i])))
    hh = each(lambda i: (s_inter[i] * q_c[i] + s_intra[i] * num_intra[i]) / den[i])
    ms = each(lambda i: jnp.mean(hh[i] * hh[i], axis=-1, keepdims=True))
    scale = each(lambda i: lax.rsqrt(ms[i] + 1e-6))
    for i in range(n):
        gate = ng_ref[0:1, vcols(i)] * _sigmoid(op_ref[0, rows(i), vcols(i)])
        o_ref[0, rows(i), vcols(i)] = (hh[i] * scale[i] * gate).astype(BF16)


def _mlstm(q, k, v, op, sc, sct, ng, ts):
    B, S, _ = q.shape
    tok = lambda w: pl.BlockSpec((1, ts, w), lambda b, s: (b, s, 0))
    return pl.pallas_call(
        _mlstm_kernel,
        grid=(B, S // ts),
        in_specs=[tok(C_QK), tok(C_QK), tok(C_V), tok(C_V), tok(LANES),
                  pl.BlockSpec((1, 8, ts), lambda b, s: (b, 0, s)),
                  pl.BlockSpec(ng.shape, lambda b, s: (0, 0))],
        out_specs=tok(C_V),
        out_shape=jax.ShapeDtypeStruct((B, S, C_V), BF16),
        scratch_shapes=[pltpu.VMEM((N_HEADS, C_QK_HEAD, C_V_HEAD), F32),
                        pltpu.VMEM((N_HEADS, 1, C_QK_HEAD), F32),
                        pltpu.VMEM((N_HEADS, 1, LANES), F32)],
        compiler_params=_params("arbitrary", "arbitrary"),
        name="mlstm_scan",
    )(q, k, v, op, sc, sct, ng)


def _pad_lanes(a, offset=0):
    return jnp.pad(a, ((0, 0), (offset, LANES - offset - a.shape[1])))


def kernel(x, p, ab_w_in, ab_conv_qkv, ab_a_log, ab_dt_bias, ab_o_norm_g, ab_dw_w, ab_dw_b,
           ab_cn_g, ab_cn_b, ab_w_out, ffn_w_gate, ffn_w_up, ffn_w_down, c_w_in, c_b_i, c_b_f,
           c_norm_g, c_w_out, moe_w_router, moe_b_router, moe_w_gate, moe_w_up, moe_w_down,
           ln_mix_g, ln_mix_b, ln_ffn_g, ln_ffn_b, ple_w_proj, ple_w_gate):
    B, S, D = x.shape
    T = B * S
    ts_front, ts_scan, tm, tm_big, tf = 512, 512, 512, 1024, 7 * LANES
    tm_moe = 7 * LANES

    w_in = ab_w_in[0]
    n_main = 4 * A_DIM
    wm = jnp.concatenate([w_in[:, :n_main], w_in[:, n_main + 2 * N_HEADS:]], axis=1).astype(BF16)
    ws = _pad_lanes(w_in[:, n_main:n_main + 2 * N_HEADS]).astype(BF16)
    gp = jnp.concatenate([_pad_lanes(ab_a_log[0][None, :], N_HEADS),
                          _pad_lanes(ab_dt_bias[0][None, :], N_HEADS)], axis=0)
    dwp = jnp.stack([ab_dw_b[0], ab_cn_g[0], ab_cn_b[0]], axis=0)
    q, k, v, z, sc, sct, u = _ab_front(x, wm, ws, ab_conv_qkv[0], gp, ab_dw_w[0], dwp, ts_front)
    o_a = _gdn(q, k, v, z, sc, sct, ab_o_norm_g[0][None, :], ts_scan)
    w_out = ab_w_out[0].astype(BF16)
    lnp = jnp.stack([ln_mix_g[0], ln_mix_b[0]], axis=0)
    xf = x.reshape(T, D)
    x1 = _proj_ln([o_a.reshape(T, A_DIM), u.reshape(T, B_CH)], [w_out[:A_DIM], w_out[A_DIM:]],
                  xf, lnp, tm)
    lnp = jnp.stack([ln_ffn_g[0], ln_ffn_b[0]], axis=0)
    x2 = _ffn(x1, ffn_w_gate[0].astype(BF16), ffn_w_up[0].astype(BF16),
              ffn_w_down[0].astype(BF16), lnp, p[0].reshape(T, PLE_DIM),
              ple_w_gate[0].astype(BF16), ple_w_proj[0].astype(BF16), tm_big, tf)

    w_in = c_w_in[0]
    n_main = 2 * C_QK + 2 * C_V
    wm = w_in[:, :n_main].astype(BF16)
    ws = _pad_lanes(w_in[:, n_main:]).astype(BF16)
    gb = _pad_lanes(jnp.concatenate([c_b_i[0], c_b_f[0]])[None, :])
    q, k, v, op, sc, sct = _c_front(x2.reshape(B, S, D), wm, ws, gb, ts_front)
    hh = _mlstm(q, k, v, op, sc, sct, c_norm_g[0][None, :], ts_scan)
    lnp = jnp.stack([ln_mix_g[1], ln_mix_b[1]], axis=0)
    x3, x3_tok = _proj_ln([hh.reshape(T, C_V)], [c_w_out[0].astype(BF16)], x2, lnp, tm,
                          token_major_copy=True)

    wr = _pad_lanes(moe_w_router[0])
    wr_hi = wr.astype(BF16)
    wr_lo = (wr - wr_hi.astype(F32)).astype(BF16)
    info, info_t, counts = _router(x3, wr_hi, wr_lo, _pad_lanes(moe_b_router[0][None, :]), tm)
    src, dst, tile_expert, n_valid, n_real = _route_plan(info_t, counts, tm_moe)
    y = _moe_experts(tile_expert, n_valid, n_real, src, dst, x3_tok, moe_w_gate[0].astype(BF16),
                     moe_w_up[0].astype(BF16), moe_w_down[0].astype(BF16), 2 * T, tm_moe, tf)
    lnp = jnp.stack([ln_ffn_g[1], ln_ffn_b[1]], axis=0)
    x4 = _moe_combine(x3, info, y, lnp, p[1].reshape(T, PLE_DIM),
                      ple_w_gate[1].astype(BF16), ple_w_proj[1].astype(BF16), tm)
    return x4.reshape(B, S, D)
```

```python
import functools

import jax
import jax.numpy as jnp
from jax import lax
from jax.experimental import pallas as pl
from jax.experimental.pallas import tpu as pltpu

F32 = jnp.float32
BF16 = jnp.bfloat16

D_MODEL = 1024
N_HEADS = 4
CHUNK = 64
A_HEAD = 128
A_DIM = N_HEADS * A_HEAD
SHORT_CONV = 4
B_CH = 512
B_CONV = 31
C_QK_HEAD = 128
C_V_HEAD = 256
C_QK = N_HEADS * C_QK_HEAD
C_V = N_HEADS * C_V_HEAD
GATE_CAP = 15.0
D_FF = 3584
N_EXPERTS = 8
PLE_DIM = 256
DEPTH = 2
DN_ALPHA = (2 * DEPTH) ** 0.25
LANES = 128
SUBLANES = 8
QKV_HALO = 8
DW_HALO = 32
VMEM_LIMIT = 56 * 1024 * 1024


def _dot(a, b):
    return jnp.dot(a, b, preferred_element_type=F32)


def _dot_nt(a, b):
    return lax.dot_general(a, b, (((1,), (1,)), ((), ())), preferred_element_type=F32)


def _dot_tn(a, b):
    return lax.dot_general(a, b, (((0,), (0,)), ((), ())), preferred_element_type=F32)


def _sigmoid(t):
    return 1.0 / (1.0 + jnp.exp(-t))


def _silu(t):
    return t * _sigmoid(t)


def _softplus(t):
    return jnp.maximum(t, 0.0) + jnp.log1p(jnp.exp(-jnp.abs(t)))


def _layer_norm(t, g, b):
    mu = jnp.mean(t, axis=-1, keepdims=True)
    c = t - mu
    var = jnp.mean(c * c, axis=-1, keepdims=True)
    return c * lax.rsqrt(var + 1e-5) * g + b


def _chunk_cumsum(val):
    row = lax.broadcasted_iota(jnp.int32, val.shape, 0) & (CHUNK - 1)
    sh = 1
    while sh < CHUNK:
        val = val + jnp.where(row >= sh, pltpu.roll(val, sh, axis=0), 0.0)
        sh *= 2
    return val


def _params(*sem):
    return pltpu.CompilerParams(dimension_semantics=sem, vmem_limit_bytes=VMEM_LIMIT)


def _ab_front_kernel(x_ref, wm_ref, ws_ref, cw_ref, gp_ref, dww_ref, dwp_ref,
                     q_ref, k_ref, v_ref, z_ref, sc_ref, sct_ref, u_ref,
                     qkv_ext, u_ext, u_shift):
    s = pl.program_id(1)
    ts = x_ref.shape[1]
    xb = x_ref[0].astype(BF16)

    @pl.when(s == 0)
    def _():
        qkv_ext[0:QKV_HALO, :] = jnp.zeros((QKV_HALO, 3 * A_DIM), F32)
        u_ext[0:DW_HALO, :] = jnp.zeros((DW_HALO, B_CH), F32)

    @pl.when(s > 0)
    def _():
        qkv_ext[0:QKV_HALO, :] = qkv_ext[ts:ts + QKV_HALO, :]
        u_ext[0:DW_HALO, :] = u_ext[ts:ts + DW_HALO, :]

    qkv_ext[QKV_HALO:QKV_HALO + ts, :] = _dot(xb, wm_ref[:, 0:3 * A_DIM])
    acc = None
    for j in range(SHORT_CONV):
        off = QKV_HALO - (SHORT_CONV - 1) + j
        term = cw_ref[j:j + 1, :] * qkv_ext[off:off + ts, :]
        acc = term if acc is None else acc + term
    qkv = _silu(acc)
    for h in range(N_HEADS):
        lo, hi = h * A_HEAD, (h + 1) * A_HEAD
        qh = qkv[:, lo:hi]
        kh = qkv[:, A_DIM + lo:A_DIM + hi]
        qn = lax.rsqrt(jnp.sum(qh * qh, axis=-1, keepdims=True) + 1e-6)
        kn = lax.rsqrt(jnp.sum(kh * kh, axis=-1, keepdims=True) + 1e-6)
        q_ref[0, :, lo:hi] = qh * qn * (A_HEAD ** -0.5)
        k_ref[0, :, lo:hi] = kh * kn
    v_ref[0] = qkv[:, 2 * A_DIM:3 * A_DIM]
    z_ref[0] = _dot(xb, wm_ref[:, 3 * A_DIM:4 * A_DIM])

    hs = _dot(xb, ws_ref[...])
    lane = lax.broadcasted_iota(jnp.int32, hs.shape, 1)
    beta = _sigmoid(hs)
    g = -jnp.exp(gp_ref[0:1, :]) * _softplus(hs + gp_ref[1:2, :])
    sc = jnp.where(lane < N_HEADS, beta, _chunk_cumsum(g))
    sc_ref[0] = sc
    sct_ref[0] = jnp.transpose(sc)[0:8, :]

    glu = _dot(xb, wm_ref[:, 4 * A_DIM:4 * A_DIM + 2 * B_CH])
    u_ext[DW_HALO:DW_HALO + ts, :] = glu[:, 0:B_CH] * _sigmoid(glu[:, B_CH:2 * B_CH])
    first_off = DW_HALO - (B_CONV - 1)
    span = ts + DW_HALO - 8
    acc = None
    for res in range(8):
        offs = [o for o in range(first_off, DW_HALO + 1) if o % 8 == res]
        if res:
            u_shift[...] = u_ext[res:res + span, :]
        src = u_shift if res else u_ext
        for off in offs:
            lo = off - res if res else off
            term = dww_ref[off - first_off:off - first_off + 1, :] * src[lo:lo + ts, :]
            acc = term if acc is None else acc + term
    u = acc + dwp_ref[0:1, :]
    u = _layer_norm(u, dwp_ref[1:2, :], dwp_ref[2:3, :])
    u_ref[0] = _silu(u).astype(BF16)


def _ab_front(x, wm, ws, cw, gp, dww, dwp, ts):
    B, S, D = x.shape
    grid = (B, S // ts)
    tok = lambda w: pl.BlockSpec((1, ts, w), lambda b, s: (b, s, 0))
    full = lambda a: pl.BlockSpec(a.shape, lambda b, s: (0,) * a.ndim)
    out_shape = (
        jax.ShapeDtypeStruct((B, S, A_DIM), F32),
        jax.ShapeDtypeStruct((B, S, A_DIM), F32),
        jax.ShapeDtypeStruct((B, S, A_DIM), F32),
        jax.ShapeDtypeStruct((B, S, A_DIM), F32),
        jax.ShapeDtypeStruct((B, S, LANES), F32),
        jax.ShapeDtypeStruct((B, 8, S), F32),
        jax.ShapeDtypeStruct((B, S, B_CH), BF16),
    )
    out_specs = (tok(A_DIM), tok(A_DIM), tok(A_DIM), tok(A_DIM), tok(LANES),
                 pl.BlockSpec((1, 8, ts), lambda b, s: (b, 0, s)), tok(B_CH))
    return pl.pallas_call(
        _ab_front_kernel,
        grid=grid,
        in_specs=[tok(D), full(wm), full(ws), full(cw), full(gp), full(dww), full(dwp)],
        out_specs=out_specs,
        out_shape=out_shape,
        scratch_shapes=[pltpu.VMEM((ts + QKV_HALO, 3 * A_DIM), F32),
                        pltpu.VMEM((ts + DW_HALO, B_CH), F32),
                        pltpu.VMEM((ts + DW_HALO - 8, B_CH), F32)],
        compiler_params=_params("arbitrary", "arbitrary"),
        name="ab_front",
    )(x, wm, ws, cw, gp, dww, dwp)


def _gdn_kernel(q_ref, k_ref, v_ref, z_ref, sc_ref, sct_ref, ng_ref, o_ref, st_ref):
    s = pl.program_id(1)
    ts = q_ref.shape[1]

    @pl.when(s == 0)
    def _():
        st_ref[...] = jnp.zeros(st_ref.shape, F32)

    row = lax.broadcasted_iota(jnp.int32, (CHUNK, CHUNK), 0)
    col = lax.broadcasted_iota(jnp.int32, (CHUNK, CHUNK), 1)
    causal = row >= col
    strict = row > col
    eye = jnp.where(row == col, 1.0, 0.0).astype(F32)

    n_chunks = ts // CHUNK
    pairs = [(c, h) for c in range(n_chunks) for h in range(N_HEADS)]
    pw, tinv, attn, rhs, k_dec, q_dec, d_last = [], [], [], [], [], [], []
    for c, h in pairs:
        r0, r1 = c * CHUNK, (c + 1) * CHUNK
        lo, hi = h * A_HEAD, (h + 1) * A_HEAD
        qh = q_ref[0, r0:r1, lo:hi]
        kh = k_ref[0, r0:r1, lo:hi]
        vh = v_ref[0, r0:r1, lo:hi]
        beta = sc_ref[0, r0:r1, h:h + 1]
        gc = sc_ref[0, r0:r1, N_HEADS + h:N_HEADS + h + 1]
        gc_row = sct_ref[0, N_HEADS + h:N_HEADS + h + 1, r0:r1]
        gc_last = gc[CHUNK - 1:CHUNK, :]
        decay = jnp.where(causal, jnp.exp(jnp.where(causal, gc - gc_row, 0.0)), 0.0)
        kb = kh * beta
        egc = jnp.exp(gc)
        kq = _dot_nt(jnp.concatenate([kb, qh], axis=0).astype(BF16), kh.astype(BF16))
        a_neg = jnp.where(strict, -kq[0:CHUNK] * decay, 0.0)
        pw.append(a_neg)
        tinv.append(eye + a_neg)
        attn.append((kq[CHUNK:2 * CHUNK] * decay).astype(BF16))
        rhs.append(jnp.concatenate([vh * beta, kb * egc], axis=1).astype(BF16))
        k_dec.append((kh * jnp.exp(gc_last - gc)).astype(BF16))
        q_dec.append(qh * egc)
        d_last.append(jnp.exp(gc_last))
    head_blk = lax.broadcasted_iota(jnp.int32, (CHUNK, N_HEADS * CHUNK), 1) // CHUNK

    def block_diag(m):
        return jnp.concatenate([jnp.where(head_blk == h, m, 0.0) for h in range(N_HEADS)],
                               axis=0).astype(BF16)

    side_by_side = lambda mats, c: jnp.concatenate(mats[c * N_HEADS:(c + 1) * N_HEADS], axis=1)
    qc = [side_by_side(pw, c) for c in range(n_chunks)]
    yc = [side_by_side(tinv, c) for c in range(n_chunks)]
    qc = [_dot(qc[c].astype(BF16), block_diag(qc[c])) for c in range(n_chunks)]
    for _ in range(4):
        both = [_dot(jnp.concatenate([yc[c], qc[c]], axis=0).astype(BF16), block_diag(qc[c]))
                for c in range(n_chunks)]
        yc = [yc[c] + both[c][0:CHUNK] for c in range(n_chunks)]
        qc = [both[c][CHUNK:2 * CHUNK] for c in range(n_chunks)]
    last = [_dot(yc[c].astype(BF16), block_diag(qc[c])) for c in range(n_chunks)]
    yc = [yc[c] + last[c] for c in range(n_chunks)]
    tinv = [yc[c][:, h * CHUNK:(h + 1) * CHUNK] for c, h in pairs]
    uw = [_dot(tinv[i].astype(BF16), rhs[i]).astype(BF16) for i in range(len(pairs))]
    auw = [_dot(attn[i], uw[i]) for i in range(len(pairs))]
    kuw = [_dot_tn(k_dec[i], uw[i]) for i in range(len(pairs))]
    st = [st_ref[h] for h in range(N_HEADS)]
    for i, (c, h) in enumerate(pairs):
        r0, r1 = c * CHUNK, (c + 1) * CHUNK
        lo, hi = h * A_HEAD, (h + 1) * A_HEAD
        stb = st[h].astype(BF16)
        q_eff = (q_dec[i] - auw[i][:, A_HEAD:]).astype(BF16)
        o = _dot(q_eff, stb) + auw[i][:, 0:A_HEAD]
        st[h] = (d_last[i] * st[h] + kuw[i][:, 0:A_HEAD]
                 - _dot(kuw[i][:, A_HEAD:].astype(BF16), stb))
        o = o * lax.rsqrt(jnp.mean(o * o, axis=-1, keepdims=True) + 1e-6) * ng_ref[0:1, :]
        o_ref[0, r0:r1, lo:hi] = (o * _silu(z_ref[0, r0:r1, lo:hi])).astype(BF16)
    for h in range(N_HEADS):
        st_ref[h] = st[h]


def _gdn(q, k, v, z, sc, sct, ng, ts):
    B, S, _ = q.shape
    tok = lambda w: pl.BlockSpec((1, ts, w), lambda b, s: (b, s, 0))
    return pl.pallas_call(
        _gdn_kernel,
        grid=(B, S // ts),
        in_specs=[tok(A_DIM), tok(A_DIM), tok(A_DIM), tok(A_DIM), tok(LANES),
                  pl.BlockSpec((1, 8, ts), lambda b, s: (b, 0, s)),
                  pl.BlockSpec(ng.shape, lambda b, s: (0, 0))],
        out_specs=tok(A_DIM),
        out_shape=jax.ShapeDtypeStruct((B, S, A_DIM), BF16),
        scratch_shapes=[pltpu.VMEM((N_HEADS, A_HEAD, A_HEAD), F32)],
        compiler_params=_params("arbitrary", "arbitrary"),
        name="gdn_scan",
    )(q, k, v, z, sc, sct, ng)


def _token_rows(s, n_tokens):
    return pl.ds(s, n_tokens, stride=D_MODEL // LANES)


def _proj_ln_kernel(*refs, n_in, token_major_copy):
    a_refs = refs[0:n_in]
    w_refs = refs[n_in:2 * n_in]
    x_ref, lnp_ref, o_ref = refs[2 * n_in:2 * n_in + 3]
    mix = None
    for a_ref, w_ref in zip(a_refs, w_refs):
        t = _dot(a_ref[...], w_ref[...])
        mix = t if mix is None else mix + t
    y = _layer_norm(DN_ALPHA * x_ref[...] + mix, lnp_ref[0:1, :], lnp_ref[1:2, :])
    o_ref[...] = y
    if token_major_copy:
        ot_ref = refs[2 * n_in + 3]
        for s in range(D_MODEL // LANES):
            ot_ref[_token_rows(s, y.shape[0]), :] = y[:, s * LANES:(s + 1) * LANES]


def _proj_ln(acts, weights, x, lnp, tm, token_major_copy=False):
    T, D = x.shape
    n_in = len(acts)
    in_specs = ([pl.BlockSpec((tm, a.shape[1]), lambda i: (i, 0)) for a in acts]
                + [pl.BlockSpec(w.shape, lambda i: (0, 0)) for w in weights]
                + [pl.BlockSpec((tm, D), lambda i: (i, 0)),
                   pl.BlockSpec(lnp.shape, lambda i: (0, 0))])
    out_specs = [pl.BlockSpec((tm, D), lambda i: (i, 0))]
    out_shape = [jax.ShapeDtypeStruct((T, D), F32)]
    if token_major_copy:
        out_specs.append(pl.BlockSpec((tm * D // LANES, LANES), lambda i: (i, 0)))
        out_shape.append(jax.ShapeDtypeStruct((T * D // LANES, LANES), F32))
    outs = pl.pallas_call(
        functools.partial(_proj_ln_kernel, n_in=n_in, token_major_copy=token_major_copy),
        grid=(T // tm,),
        in_specs=in_specs,
        out_specs=tuple(out_specs),
        out_shape=tuple(out_shape),
        compiler_params=_params("arbitrary"),
        name="proj_ln",
    )(*acts, *weights, x, lnp)
    return outs if token_major_copy else outs[0]


def _ffn_epilogue(x, acc, lnp_ref, p_ref, wpg_ref, wpp_ref, o_ref):
    y = _layer_norm(DN_ALPHA * x + acc, lnp_ref[0:1, :], lnp_ref[1:2, :])
    gate = _sigmoid(_dot(y.astype(BF16), wpg_ref[...]))
    o_ref[...] = y + gate * _dot(p_ref[...].astype(BF16), wpp_ref[...])


def _ffn_kernel(x_ref, wg_ref, wu_ref, wd_ref, lnp_ref, p_ref, wpg_ref, wpp_ref,
                o_ref, xb_ref):
    j = pl.program_id(1)

    @pl.when(j == 0)
    def _():
        xb_ref[...] = x_ref[...].astype(BF16)
        o_ref[...] = jnp.zeros(o_ref.shape, F32)

    xb = xb_ref[...]
    hid = _silu(_dot(xb, wg_ref[...])) * _dot(xb, wu_ref[...])
    o_ref[...] += _dot(hid.astype(BF16), wd_ref[...])

    @pl.when(j == pl.num_programs(1) - 1)
    def _():
        _ffn_epilogue(x_ref[...], o_ref[...], lnp_ref, p_ref, wpg_ref, wpp_ref, o_ref)


def _ffn(x, wg, wu, wd, lnp, p, wpg, wpp, tm, tf):
    T, D = x.shape
    F = wg.shape[1]
    return pl.pallas_call(
        _ffn_kernel,
        grid=(T // tm, F // tf),
        in_specs=[pl.BlockSpec((tm, D), lambda i, j: (i, 0)),
                  pl.BlockSpec((D, tf), lambda i, j: (0, j)),
                  pl.BlockSpec((D, tf), lambda i, j: (0, j)),
                  pl.BlockSpec((tf, D), lambda i, j: (j, 0)),
                  pl.BlockSpec(lnp.shape, lambda i, j: (0, 0)),
                  pl.BlockSpec((tm, PLE_DIM), lambda i, j: (i, 0)),
                  pl.BlockSpec(wpg.shape, lambda i, j: (0, 0)),
                  pl.BlockSpec(wpp.shape, lambda i, j: (0, 0))],
        out_specs=pl.BlockSpec((tm, D), lambda i, j: (i, 0)),
        out_shape=jax.ShapeDtypeStruct((T, D), F32),
        scratch_shapes=[pltpu.VMEM((tm, D), BF16)],
        compiler_params=_params("arbitrary", "arbitrary"),
        name="ffn",
    )(x, wg, wu, wd, lnp, p, wpg, wpp)


ROUTE_IDX1, ROUTE_IDX2, ROUTE_W1, ROUTE_W2, ROUTE_RANK1, ROUTE_RANK2 = range(6)


def _router_kernel(x_ref, wh_ref, wl_ref, br_ref, info_ref, info_t_ref, cnt_ref, carry_ref):
    i = pl.program_id(0)

    @pl.when(i == 0)
    def _():
        carry_ref[...] = jnp.zeros(carry_ref.shape, F32)

    x = x_ref[...]
    tm = x.shape[0]
    xh = x.astype(BF16)
    xl = (x - xh.astype(F32)).astype(BF16)
    logits = (_dot(xh, wh_ref[...]) + _dot(xl, wh_ref[...]) + _dot(xh, wl_ref[...])
              + br_ref[0:1, :])
    lane = lax.broadcasted_iota(jnp.int32, logits.shape, 1)
    neg = jnp.float32(-jnp.inf)
    logits = jnp.where(lane < N_EXPERTS, logits, neg)
    m1 = jnp.max(logits, axis=-1, keepdims=True)
    i1 = jnp.min(jnp.where(logits == m1, lane, LANES), axis=-1, keepdims=True)
    sel1 = lane == i1
    rest = jnp.where(sel1, neg, logits)
    m2 = jnp.max(rest, axis=-1, keepdims=True)
    i2 = jnp.min(jnp.where(rest == m2, lane, LANES), axis=-1, keepdims=True)
    sel2 = lane == i2
    e2 = jnp.exp(m2 - m1)
    den = 1.0 + e2

    sel = jnp.where(jnp.logical_or(sel1, sel2), 1.0, 0.0)
    row = lax.broadcasted_iota(jnp.int32, (tm, tm), 0)
    col = lax.broadcasted_iota(jnp.int32, (tm, tm), 1)
    earlier = jnp.where(row > col, 1.0, 0.0).astype(BF16)
    rank = _dot(earlier, sel.astype(BF16)) + carry_ref[0:1, :]
    rank1 = jnp.sum(jnp.where(sel1, rank, 0.0), axis=-1, keepdims=True)
    rank2 = jnp.sum(jnp.where(sel2, rank, 0.0), axis=-1, keepdims=True)

    info = jnp.zeros(logits.shape, F32)
    for slot, val in ((ROUTE_IDX1, i1.astype(F32)), (ROUTE_IDX2, i2.astype(F32)),
                      (ROUTE_W1, 1.0 / den), (ROUTE_W2, e2 / den),
                      (ROUTE_RANK1, rank1), (ROUTE_RANK2, rank2)):
        info = jnp.where(lane == slot, val, info)
    info_ref[...] = info
    info_t_ref[...] = jnp.transpose(info)[0:8, :]
    carry_ref[0:1, :] = carry_ref[0:1, :] + jnp.sum(sel, axis=0, keepdims=True)
    cnt_ref[...] = jnp.broadcast_to(carry_ref[0:1, :], cnt_ref.shape)


def _router(x, wh, wl, br, tm):
    T, D = x.shape
    return pl.pallas_call(
        _router_kernel,
        grid=(T // tm,),
        in_specs=[pl.BlockSpec((tm, D), lambda i: (i, 0)),
                  pl.BlockSpec(wh.shape, lambda i: (0, 0)),
                  pl.BlockSpec(wl.shape, lambda i: (0, 0)),
                  pl.BlockSpec(br.shape, lambda i: (0, 0))],
        out_specs=(pl.BlockSpec((tm, LANES), lambda i: (i, 0)),
                   pl.BlockSpec((8, tm), lambda i: (0, i)),
                   pl.BlockSpec((8, LANES), lambda i: (0, 0))),
        out_shape=(jax.ShapeDtypeStruct((T, LANES), F32),
                   jax.ShapeDtypeStruct((8, T), F32),
                   jax.ShapeDtypeStruct((8, LANES), F32)),
        scratch_shapes=[pltpu.VMEM((8, LANES), F32)],
        compiler_params=_params("arbitrary"),
        name="router",
    )(x, wh, wl, br)


def _moe_expert_kernel(te_ref, nv_ref, nr_ref, src_ref, dst_ref, x_hbm, wg_ref, wu_ref, wd_ref, y_hbm,
                       xg_ref, xb_ref, acc_ref, yo_ref, gidx_ref, sidx_ref,
                       sem_idx, sem_g, sem_s, *, n_j):
    s = pl.program_id(0)
    j = pl.program_id(1)
    tm = xb_ref.shape[0]
    rows_per_step = tm // n_j
    n_blk = D_MODEL // LANES
    nv = nv_ref[0]
    gather_on = s < nv
    compute_on = jnp.logical_and(s >= 1, s - 1 < nv)
    scatter_on = jnp.logical_and(s >= 2, s - 2 < nv)
    slot_cur = lax.rem(s, 2)
    slot_prev = 1 - slot_cur

    def token_tile(first_row):
        return pl.ds(pl.multiple_of(first_row, n_blk), n_blk)

    def gather_copy(slot, r, src_row):
        return pltpu.make_async_copy(x_hbm.at[token_tile(src_row), :],
                                     xg_ref.at[slot, token_tile(r * n_blk), :], sem_g)

    def scatter_copy(slot, r, dst_row):
        return pltpu.make_async_copy(yo_ref.at[slot, token_tile(r * n_blk), :],
                                     y_hbm.at[token_tile(dst_row), :], sem_s)

    def real_rows(tile):
        return nr_ref[jnp.clip(tile, 0, nr_ref.shape[0] - 1)]

    def drain(copy_of_row, n_rows=None):
        def body(g, carry):
            r0 = pl.multiple_of(g * SUBLANES, SUBLANES)
            for u in range(SUBLANES):
                copy_of_row(r0 + u).wait()
            return carry

        def one(r, carry):
            copy_of_row(r).wait()
            return carry

        def all_rows():
            lax.fori_loop(0, tm // SUBLANES, body, 0)

        def some_rows():
            lax.fori_loop(0, n_rows, one, 0)

        if n_rows is None:
            all_rows()
        else:
            pl.when(n_rows == tm)(all_rows)
            pl.when(n_rows < tm)(some_rows)

    stage_g = pltpu.make_async_copy(src_ref.at[0], gidx_ref, sem_idx.at[0])
    stage_s = pltpu.make_async_copy(dst_ref.at[0], sidx_ref, sem_idx.at[1])

    @pl.when(j == 0)
    def _():
        pl.when(gather_on)(stage_g.start)
        pl.when(scatter_on)(stage_s.start)

        @pl.when(compute_on)
        def _():
            drain(lambda r: gather_copy(slot_prev, r, 0))
            for blk in range(n_blk):
                xb_ref[:, blk * LANES:(blk + 1) * LANES] = (
                    xg_ref[slot_prev, _token_rows(blk, tm), :].astype(BF16))
            acc_ref[...] = jnp.zeros(acc_ref.shape, F32)

        @pl.when(jnp.logical_and(s >= 3, s - 3 < nv))
        def _():
            drain(lambda r: scatter_copy(slot_prev, r, 0), real_rows(s - 3))

        pl.when(gather_on)(stage_g.wait)
        pl.when(scatter_on)(stage_s.wait)

    base = pl.multiple_of(j * rows_per_step, rows_per_step)

    def issue_gathers():
        for u in range(rows_per_step):
            gather_copy(slot_cur, base + u, gidx_ref[0, base + u]).start(priority=1)

    def issue_scatters():
        n_real = real_rows(s - 2)

        @pl.when(base + rows_per_step <= n_real)
        def _():
            for u in range(rows_per_step):
                scatter_copy(slot_cur, base + u, sidx_ref[0, base + u]).start()

        @pl.when(base + rows_per_step > n_real)
        def _():
            def one(r, carry):
                scatter_copy(slot_cur, r, sidx_ref[0, r]).start()
                return carry
            lax.fori_loop(base, jnp.maximum(base, jnp.minimum(base + rows_per_step, n_real)),
                          one, 0)

    def compute():
        xb = xb_ref[...]
        hid = _silu(_dot(xb, wg_ref[0])) * _dot(xb, wu_ref[0])
        acc_ref[...] += _dot(hid.astype(BF16), wd_ref[0])

    pl.when(gather_on)(issue_gathers)
    pl.when(compute_on)(compute)
    pl.when(scatter_on)(issue_scatters)

    last_j = j == n_j - 1

    @pl.when(jnp.logical_and(last_j, compute_on))
    def _():
        for blk in range(n_blk):
            yo_ref[slot_prev, _token_rows(blk, tm), :] = acc_ref[:, blk * LANES:(blk + 1) * LANES]

    @pl.when(jnp.logical_and(jnp.logical_and(last_j, s == pl.num_programs(0) - 1), scatter_on))
    def _():
        drain(lambda r: scatter_copy(slot_cur, r, 0), real_rows(s - 2))


def _moe_experts(tile_expert, n_valid, n_real, src_rows, dst_rows, x, wg, wu, wd, n_out_rows,
                 tm, tf):
    E, D, F = wg.shape
    n_tiles = src_rows.shape[0]
    n_j = F // tf
    assert tm % (n_j * SUBLANES) == 0

    def computing(s, nv):
        return jnp.logical_and(s >= 1, s - 1 < nv[0])

    def tile_of(s):
        return jnp.clip(s - 1, 0, n_tiles - 1)

    def w_map(s, j, te, nv, nr):
        return (te[tile_of(s)], 0, jnp.where(computing(s, nv), j, 0))

    def wd_map(s, j, te, nv, nr):
        return (te[tile_of(s)], jnp.where(computing(s, nv), j, 0), 0)

    grid_spec = pltpu.PrefetchScalarGridSpec(
        num_scalar_prefetch=3,
        grid=(n_tiles + 2, n_j),
        in_specs=[pl.BlockSpec((1, 1, tm),
                               lambda s, j, te, nv, nr: (jnp.minimum(s, n_tiles - 1), 0, 0)),
                  pl.BlockSpec((1, 1, tm),
                               lambda s, j, te, nv, nr: (jnp.clip(s - 2, 0, n_tiles - 1), 0, 0)),
                  pl.BlockSpec(memory_space=pl.ANY),
                  pl.BlockSpec((1, D, tf), w_map),
                  pl.BlockSpec((1, D, tf), w_map),
                  pl.BlockSpec((1, tf, D), wd_map)],
        out_specs=pl.BlockSpec(memory_space=pl.ANY),
        scratch_shapes=[pltpu.VMEM((2, tm * D // LANES, LANES), F32),
                        pltpu.VMEM((tm, D), BF16),
                        pltpu.VMEM((tm, D), F32),
                        pltpu.VMEM((2, tm * D // LANES, LANES), F32),
                        pltpu.SMEM((1, tm), jnp.int32), pltpu.SMEM((1, tm), jnp.int32),
                        pltpu.SemaphoreType.DMA((2,)), pltpu.SemaphoreType.DMA(()),
                        pltpu.SemaphoreType.DMA(())],
    )
    return pl.pallas_call(
        functools.partial(_moe_expert_kernel, n_j=n_j),
        grid_spec=grid_spec,
        out_shape=jax.ShapeDtypeStruct((n_out_rows * D // LANES, LANES), F32),
        compiler_params=_params("arbitrary", "arbitrary"),
        name="moe_experts",
    )(tile_expert, n_valid, n_real, src_rows, dst_rows, x, wg, wu, wd)


def _moe_combine_kernel(x_ref, info_ref, ya_ref, yb_ref, lnp_ref, p_ref, wpg_ref, wpp_ref, o_ref):
    info = info_ref[...]
    tm = x_ref.shape[0]
    w1 = info[:, ROUTE_W1:ROUTE_W1 + 1]
    w2 = info[:, ROUTE_W2:ROUTE_W2 + 1]
    mix = jnp.concatenate(
        [w1 * ya_ref[_token_rows(blk, tm), :] + w2 * yb_ref[_token_rows(blk, tm), :]
         for blk in range(D_MODEL // LANES)], axis=1)
    _ffn_epilogue(x_ref[...], mix, lnp_ref, p_ref, wpg_ref, wpp_ref, o_ref)


def _moe_combine(x, info, y, lnp, p, wpg, wpp, tm):
    T, D = x.shape
    y_block = (tm * D // LANES, LANES)
    return pl.pallas_call(
        _moe_combine_kernel,
        grid=(T // tm,),
        in_specs=[pl.BlockSpec((tm, D), lambda i: (i, 0)),
                  pl.BlockSpec((tm, LANES), lambda i: (i, 0)),
                  pl.BlockSpec(y_block, lambda i: (i, 0)),
                  pl.BlockSpec(y_block, lambda i: (T // tm + i, 0)),
                  pl.BlockSpec(lnp.shape, lambda i: (0, 0)),
                  pl.BlockSpec((tm, PLE_DIM), lambda i: (i, 0)),
                  pl.BlockSpec(wpg.shape, lambda i: (0, 0)),
                  pl.BlockSpec(wpp.shape, lambda i: (0, 0))],
        out_specs=pl.BlockSpec((tm, D), lambda i: (i, 0)),
        out_shape=jax.ShapeDtypeStruct((T, D), F32),
        compiler_params=_params("arbitrary"),
        name="moe_combine",
    )(x, info, y, y, lnp, p, wpg, wpp)


def _route_plan(info_t, counts, tm):
    T = info_t.shape[1]
    n_tiles = (2 * T + N_EXPERTS * (tm - 1)) // tm + 1
    n_rows = n_tiles * tm
    as_int = lambda col: info_t[col].astype(jnp.int32)
    idx1, idx2 = as_int(ROUTE_IDX1), as_int(ROUTE_IDX2)
    counts = counts[0, :N_EXPERTS].astype(jnp.int32)
    padded = ((counts + tm - 1) // tm) * tm
    ends = jnp.cumsum(padded)
    starts = ends - padded
    grouped_row = jnp.concatenate([starts[idx1] + as_int(ROUTE_RANK1),
                                   starts[idx2] + as_int(ROUTE_RANK2)])
    out_row = jnp.arange(2 * T, dtype=jnp.int32)
    owner = jnp.full((n_rows,), -1, jnp.int32).at[grouped_row].set(out_row)
    real = owner >= 0
    dst = jnp.maximum(owner, 0)
    src = jnp.where(real, jnp.where(owner >= T, owner - T, owner), 0)
    n_real = jnp.sum(real.reshape(n_tiles, tm), axis=1).astype(jnp.int32)
    tile_start = jnp.arange(n_tiles, dtype=jnp.int32) * tm
    tile_expert = jnp.minimum(jnp.sum(tile_start[:, None] >= ends[None, :], axis=1),
                              N_EXPERTS - 1).astype(jnp.int32)
    n_valid = (ends[-1] // tm).astype(jnp.int32).reshape(1)
    shape = (n_tiles, 1, tm)
    n_blk = D_MODEL // LANES
    return (src * n_blk).reshape(shape), (dst * n_blk).reshape(shape), tile_expert, n_valid, n_real


def _c_front_kernel(x_ref, wm_ref, ws_ref, gb_ref, q_ref, k_ref, v_ref, op_ref, sc_ref, sct_ref):
    xb = x_ref[0].astype(BF16)
    q_ref[0] = _dot(xb, wm_ref[:, 0:C_QK]) * (C_QK_HEAD ** -0.5)
    k_ref[0] = _dot(xb, wm_ref[:, C_QK:2 * C_QK])
    v_ref[0] = _dot(xb, wm_ref[:, 2 * C_QK:2 * C_QK + C_V])
    op_ref[0] = _dot(xb, wm_ref[:, 2 * C_QK + C_V:2 * C_QK + 2 * C_V])
    hs = _dot(xb, ws_ref[...]) + gb_ref[0:1, :]
    capped = GATE_CAP * jnp.tanh(hs / GATE_CAP)
    lane = lax.broadcasted_iota(jnp.int32, hs.shape, 1)
    log_f = jnp.minimum(capped, 0.0) - jnp.log1p(jnp.exp(-jnp.abs(capped)))
    sc = jnp.where(lane < N_HEADS, capped, _chunk_cumsum(log_f))
    sc_ref[0] = sc
    sct_ref[0] = jnp.transpose(sc)[0:8, :]


def _c_front(x, wm, ws, gb, ts):
    B, S, D = x.shape
    tok = lambda w: pl.BlockSpec((1, ts, w), lambda b, s: (b, s, 0))
    full = lambda a: pl.BlockSpec(a.shape, lambda b, s: (0,) * a.ndim)
    out_shape = (
        jax.ShapeDtypeStruct((B, S, C_QK), F32),
        jax.ShapeDtypeStruct((B, S, C_QK), F32),
        jax.ShapeDtypeStruct((B, S, C_V), F32),
        jax.ShapeDtypeStruct((B, S, C_V), F32),
        jax.ShapeDtypeStruct((B, S, LANES), F32),
        jax.ShapeDtypeStruct((B, 8, S), F32),
    )
    out_specs = (tok(C_QK), tok(C_QK), tok(C_V), tok(C_V), tok(LANES),
                 pl.BlockSpec((1, 8, ts), lambda b, s: (b, 0, s)))
    return pl.pallas_call(
        _c_front_kernel,
        grid=(B, S // ts),
        in_specs=[tok(D), full(wm), full(ws), full(gb)],
        out_specs=out_specs,
        out_shape=out_shape,
        compiler_params=_params("arbitrary", "arbitrary"),
        name="c_front",
    )(x, wm, ws, gb)


def _mlstm_kernel(q_ref, k_ref, v_ref, op_ref, sc_ref, sct_ref, ng_ref, o_ref,
                  c_ref, n_ref, m_ref):
    s = pl.program_id(1)
    ts = q_ref.shape[1]

    @pl.when(s == 0)
    def _():
        c_ref[...] = jnp.zeros(c_ref.shape, F32)
        n_ref[...] = jnp.zeros(n_ref.shape, F32)
        m_ref[...] = jnp.zeros(m_ref.shape, F32)

    row = lax.broadcasted_iota(jnp.int32, (CHUNK, CHUNK), 0)
    col = lax.broadcasted_iota(jnp.int32, (CHUNK, CHUNK), 1)
    causal = row >= col
    neg = jnp.float32(-jnp.inf)

    n_chunks = ts // CHUNK
    pairs = [(c, h) for c in range(n_chunks) for h in range(N_HEADS)]
    n = len(pairs)
    each = lambda f: [f(i) for i in range(n)]
    rows = lambda i: slice(pairs[i][0] * CHUNK, (pairs[i][0] + 1) * CHUNK)
    kcols = lambda i: slice(pairs[i][1] * C_QK_HEAD, (pairs[i][1] + 1) * C_QK_HEAD)
    vcols = lambda i: slice(pairs[i][1] * C_V_HEAD, (pairs[i][1] + 1) * C_V_HEAD)
    head = lambda i: pairs[i][1]

    qs = each(lambda i: q_ref[0, rows(i), kcols(i)])
    ks = each(lambda i: k_ref[0, rows(i), kcols(i)])
    qbs = each(lambda i: qs[i].astype(BF16))
    vbs = each(lambda i: v_ref[0, rows(i), vcols(i)].astype(BF16))
    qk = each(lambda i: _dot_nt(qbs[i], ks[i].astype(BF16)))
    lis = each(lambda i: sc_ref[0, rows(i), head(i):head(i) + 1])
    bcs = each(lambda i: sc_ref[0, rows(i), N_HEADS + head(i):N_HEADS + head(i) + 1])
    b_lasts = each(lambda i: bcs[i][CHUNK - 1:CHUNK, :])
    bc_wide = each(lambda i: jnp.broadcast_to(bcs[i], (CHUNK, CHUNK)))
    dmat = each(lambda i: jnp.where(
        causal,
        bc_wide[i] - sct_ref[0, N_HEADS + head(i):N_HEADS + head(i) + 1, rows(i)]
        + sct_ref[0, head(i):head(i) + 1, rows(i)], neg))
    m_intras = each(lambda i: jnp.max(dmat[i], axis=-1, keepdims=True))
    pmat = each(lambda i: jnp.exp(dmat[i] - m_intras[i]) * qk[i])
    dens = each(lambda i: jnp.sum(pmat[i], axis=-1, keepdims=True))
    num_intra = each(lambda i: _dot(pmat[i].astype(BF16), vbs[i]))
    g_kvs = each(lambda i: b_lasts[i] - bcs[i] + lis[i])
    m_kvs = each(lambda i: jnp.max(g_kvs[i], axis=0, keepdims=True))

    m_run = [m_ref[h, :, 0:1] for h in range(N_HEADS)]
    m_sts, m_news = [], []
    for i in range(n):
        m_sts.append(m_run[head(i)])
        m_run[head(i)] = jnp.maximum(b_lasts[i] + m_run[head(i)], m_kvs[i])
        m_news.append(m_run[head(i)])
    decs = each(lambda i: jnp.exp(b_lasts[i] + m_sts[i] - m_news[i]))
    k_scale = each(lambda i: jnp.exp(g_kvs[i] - m_news[i]))
    kws = each(lambda i: ks[i] * k_scale[i])
    kv = each(lambda i: _dot_tn(kws[i].astype(BF16), vbs[i]))
    k_sum = each(lambda i: jnp.sum(kws[i], axis=0, keepdims=True))
    dec_c = each(lambda i: jnp.broadcast_to(decs[i], (C_QK_HEAD, C_V_HEAD)))
    c_run = [c_ref[h] for h in range(N_HEADS)]
    n_run = [n_ref[h] for h in range(N_HEADS)]
    c_sts, n_sts = [], []
    for i in range(n):
        c_sts.append(c_run[head(i)].astype(BF16))
        n_sts.append(n_run[head(i)])
        c_run[head(i)] = dec_c[i] * c_run[head(i)] + kv[i]
        n_run[head(i)] = decs[i] * n_run[head(i)] + k_sum[i]
    for h in range(N_HEADS):
        c_ref[h] = c_run[h]
        n_ref[h] = n_run[h]
        m_ref[h] = jnp.broadcast_to(m_run[h], (1, LANES))

    q_c = each(lambda i: _dot(qbs[i], c_sts[i]))
    q_n = each(lambda i: jnp.sum(qs[i] * n_sts[i], axis=-1, keepdims=True))
    inter = each(lambda i: bcs[i] + m_sts[i])
    m_t = each(lambda i: jnp.maximum(inter[i], m_intras[i]))
    s_inter = each(lambda i: jnp.exp(inter[i] - m_t[i]))
    s_intra = each(lambda i: jnp.exp(m_intras[i] - m_t[i]))
    den = each(lambda i: jnp.maximum(jnp.abs(s_inter[i] * q_n[i] + s_intra[i] * dens[i]),
                                     jnp.exp(-m_t[i])))
    hh = each(lambda i: (s_inter[i] * q_c[i] + s_intra[i] * num_intra[i]) / den[i])
    ms = each(lambda i: jnp.mean(hh[i] * hh[i], axis=-1, keepdims=True))
    scale = each(lambda i: lax.rsqrt(ms[i] + 1e-6))
    for i in range(n):
        gate = ng_ref[0:1, vcols(i)] * _sigmoid(op_ref[0, rows(i), vcols(i)])
        o_ref[0, rows(i), vcols(i)] = (hh[i] * scale[i] * gate).astype(BF16)


def _mlstm(q, k, v, op, sc, sct, ng, ts):
    B, S, _ = q.shape
    tok = lambda w: pl.BlockSpec((1, ts, w), lambda b, s: (b, s, 0))
    return pl.pallas_call(
        _mlstm_kernel,
        grid=(B, S // ts),
        in_specs=[tok(C_QK), tok(C_QK), tok(C_V), tok(C_V), tok(LANES),
                  pl.BlockSpec((1, 8, ts), lambda b, s: (b, 0, s)),
                  pl.BlockSpec(ng.shape, lambda b, s: (0, 0))],
        out_specs=tok(C_V),
        out_shape=jax.ShapeDtypeStruct((B, S, C_V), BF16),
        scratch_shapes=[pltpu.VMEM((N_HEADS, C_QK_HEAD, C_V_HEAD), F32),
                        pltpu.VMEM((N_HEADS, 1, C_QK_HEAD), F32),
                        pltpu.VMEM((N_HEADS, 1, LANES), F32)],
        compiler_params=_params("arbitrary", "arbitrary"),
        name="mlstm_scan",
    )(q, k, v, op, sc, sct, ng)


def _pad_lanes(a, offset=0):
    return jnp.pad(a, ((0, 0), (offset, LANES - offset - a.shape[1])))


def kernel(x, p, ab_w_in, ab_conv_qkv, ab_a_log, ab_dt_bias, ab_o_norm_g, ab_dw_w, ab_dw_b,
           ab_cn_g, ab_cn_b, ab_w_out, ffn_w_gate, ffn_w_up, ffn_w_down, c_w_in, c_b_i, c_b_f,
           c_norm_g, c_w_out, moe_w_router, moe_b_router, moe_w_gate, moe_w_up, moe_w_down,
           ln_mix_g, ln_mix_b, ln_ffn_g, ln_ffn_b, ple_w_proj, ple_w_gate):
    B, S, D = x.shape
    T = B * S
    ts_front, ts_scan, tm, tm_big, tf = 512, 512, 512, 1024, 512
    tm_moe = 7 * LANES

    w_in = ab_w_in[0]
    n_main = 4 * A_DIM
    wm = jnp.concatenate([w_in[:, :n_main], w_in[:, n_main + 2 * N_HEADS:]], axis=1).astype(BF16)
    ws = _pad_lanes(w_in[:, n_main:n_main + 2 * N_HEADS]).astype(BF16)
    gp = jnp.concatenate([_pad_lanes(ab_a_log[0][None, :], N_HEADS),
                          _pad_lanes(ab_dt_bias[0][None, :], N_HEADS)], axis=0)
    dwp = jnp.stack([ab_dw_b[0], ab_cn_g[0], ab_cn_b[0]], axis=0)
    q, k, v, z, sc, sct, u = _ab_front(x, wm, ws, ab_conv_qkv[0], gp, ab_dw_w[0], dwp, ts_front)
    o_a = _gdn(q, k, v, z, sc, sct, ab_o_norm_g[0][None, :], ts_scan)
    w_out = ab_w_out[0].astype(BF16)
    lnp = jnp.stack([ln_mix_g[0], ln_mix_b[0]], axis=0)
    xf = x.reshape(T, D)
    x1 = _proj_ln([o_a.reshape(T, A_DIM), u.reshape(T, B_CH)], [w_out[:A_DIM], w_out[A_DIM:]],
                  xf, lnp, tm)
    lnp = jnp.stack([ln_ffn_g[0], ln_ffn_b[0]], axis=0)
    x2 = _ffn(x1, ffn_w_gate[0].astype(BF16), ffn_w_up[0].astype(BF16),
              ffn_w_down[0].astype(BF16), lnp, p[0].reshape(T, PLE_DIM),
              ple_w_gate[0].astype(BF16), ple_w_proj[0].astype(BF16), tm_big, tf)

    w_in = c_w_in[0]
    n_main = 2 * C_QK + 2 * C_V
    wm = w_in[:, :n_main].astype(BF16)
    ws = _pad_lanes(w_in[:, n_main:]).astype(BF16)
    gb = _pad_lanes(jnp.concatenate([c_b_i[0], c_b_f[0]])[None, :])
    q, k, v, op, sc, sct = _c_front(x2.reshape(B, S, D), wm, ws, gb, ts_front)
    hh = _mlstm(q, k, v, op, sc, sct, c_norm_g[0][None, :], ts_scan)
    lnp = jnp.stack([ln_mix_g[1], ln_mix_b[1]], axis=0)
    x3, x3_tok = _proj_ln([hh.reshape(T, C_V)], [c_w_out[0].astype(BF16)], x2, lnp, tm,
                          token_major_copy=True)

    wr = _pad_lanes(moe_w_router[0])
    wr_hi = wr.astype(BF16)
    wr_lo = (wr - wr_hi.astype(F32)).astype(BF16)
    info, info_t, counts = _router(x3, wr_hi, wr_lo, _pad_lanes(moe_b_router[0][None, :]), tm)
    src, dst, tile_expert, n_valid, n_real = _route_plan(info_t, counts, tm_moe)
    y = _moe_experts(tile_expert, n_valid, n_real, src, dst, x3_tok, moe_w_gate[0].astype(BF16),
                     moe_w_up[0].astype(BF16), moe_w_down[0].astype(BF16), 2 * T, tm_moe, tf)
    lnp = jnp.stack([ln_ffn_g[1], ln_ffn_b[1]], axis=0)
    x4 = _moe_combine(x3, info, y, lnp, p[1].reshape(T, PLE_DIM),
                      ple_w_gate[1].astype(BF16), ple_w_proj[1].astype(BF16), tm)
    return x4.reshape(B, S, D)
```

```python
import functools

import jax
import jax.numpy as jnp
from jax import lax
from jax.experimental import pallas as pl
from jax.experimental.pallas import tpu as pltpu

F32 = jnp.float32
BF16 = jnp.bfloat16

D_MODEL = 1024
N_HEADS = 4
CHUNK = 64
A_HEAD = 128
A_DIM = N_HEADS * A_HEAD
SHORT_CONV = 4
B_CH = 512
B_CONV = 31
C_QK_HEAD = 128
C_V_HEAD = 256
C_QK = N_HEADS * C_QK_HEAD
C_V = N_HEADS * C_V_HEAD
GATE_CAP = 15.0
D_FF = 3584
N_EXPERTS = 8
PLE_DIM = 256
DEPTH = 2
DN_ALPHA = (2 * DEPTH) ** 0.25
LANES = 128
SUBLANES = 8
QKV_HALO = 8
DW_HALO = 32
VMEM_LIMIT = 56 * 1024 * 1024


def _dot(a, b):
    return jnp.dot(a, b, preferred_element_type=F32)


def _dot_nt(a, b):
    return lax.dot_general(a, b, (((1,), (1,)), ((), ())), preferred_element_type=F32)


def _dot_tn(a, b):
    return lax.dot_general(a, b, (((0,), (0,)), ((), ())), preferred_element_type=F32)


def _sigmoid(t):
    return 1.0 / (1.0 + jnp.exp(-t))


def _silu(t):
    return t * _sigmoid(t)


def _softplus(t):
    return jnp.maximum(t, 0.0) + jnp.log1p(jnp.exp(-jnp.abs(t)))


def _layer_norm(t, g, b):
    mu = jnp.mean(t, axis=-1, keepdims=True)
    c = t - mu
    var = jnp.mean(c * c, axis=-1, keepdims=True)
    return c * lax.rsqrt(var + 1e-5) * g + b


def _chunk_cumsum(val):
    row = lax.broadcasted_iota(jnp.int32, val.shape, 0) & (CHUNK - 1)
    sh = 1
    while sh < CHUNK:
        val = val + jnp.where(row >= sh, pltpu.roll(val, sh, axis=0), 0.0)
        sh *= 2
    return val


def _params(*sem):
    return pltpu.CompilerParams(dimension_semantics=sem, vmem_limit_bytes=VMEM_LIMIT)


def _ab_front_kernel(x_ref, wm_ref, ws_ref, cw_ref, gp_ref, dww_ref, dwp_ref,
                     q_ref, k_ref, v_ref, z_ref, sc_ref, sct_ref, u_ref,
                     qkv_ext, u_ext, u_shift):
    s = pl.program_id(1)
    ts = x_ref.shape[1]
    xb = x_ref[0].astype(BF16)

    @pl.when(s == 0)
    def _():
        qkv_ext[0:QKV_HALO, :] = jnp.zeros((QKV_HALO, 3 * A_DIM), F32)
        u_ext[0:DW_HALO, :] = jnp.zeros((DW_HALO, B_CH), F32)

    @pl.when(s > 0)
    def _():
        qkv_ext[0:QKV_HALO, :] = qkv_ext[ts:ts + QKV_HALO, :]
        u_ext[0:DW_HALO, :] = u_ext[ts:ts + DW_HALO, :]

    qkv_ext[QKV_HALO:QKV_HALO + ts, :] = _dot(xb, wm_ref[:, 0:3 * A_DIM])
    acc = None
    for j in range(SHORT_CONV):
        off = QKV_HALO - (SHORT_CONV - 1) + j
        term = cw_ref[j:j + 1, :] * qkv_ext[off:off + ts, :]
        acc = term if acc is None else acc + term
    qkv = _silu(acc)
    for h in range(N_HEADS):
        lo, hi = h * A_HEAD, (h + 1) * A_HEAD
        qh = qkv[:, lo:hi]
        kh = qkv[:, A_DIM + lo:A_DIM + hi]
        qn = lax.rsqrt(jnp.sum(qh * qh, axis=-1, keepdims=True) + 1e-6)
        kn = lax.rsqrt(jnp.sum(kh * kh, axis=-1, keepdims=True) + 1e-6)
        q_ref[0, :, lo:hi] = qh * qn * (A_HEAD ** -0.5)
        k_ref[0, :, lo:hi] = kh * kn
    v_ref[0] = qkv[:, 2 * A_DIM:3 * A_DIM]
    z_ref[0] = _dot(xb, wm_ref[:, 3 * A_DIM:4 * A_DIM])

    hs = _dot(xb, ws_ref[...])
    lane = lax.broadcasted_iota(jnp.int32, hs.shape, 1)
    beta = _sigmoid(hs)
    g = -jnp.exp(gp_ref[0:1, :]) * _softplus(hs + gp_ref[1:2, :])
    sc = jnp.where(lane < N_HEADS, beta, _chunk_cumsum(g))
    sc_ref[0] = sc
    sct_ref[0] = jnp.transpose(sc)[0:8, :]

    glu = _dot(xb, wm_ref[:, 4 * A_DIM:4 * A_DIM + 2 * B_CH])
    u_ext[DW_HALO:DW_HALO + ts, :] = glu[:, 0:B_CH] * _sigmoid(glu[:, B_CH:2 * B_CH])
    first_off = DW_HALO - (B_CONV - 1)
    span = ts + DW_HALO - 8
    acc = None
    for res in range(8):
        offs = [o for o in range(first_off, DW_HALO + 1) if o % 8 == res]
        if res:
            u_shift[...] = u_ext[res:res + span, :]
        src = u_shift if res else u_ext
        for off in offs:
            lo = off - res if res else off
            term = dww_ref[off - first_off:off - first_off + 1, :] * src[lo:lo + ts, :]
            acc = term if acc is None else acc + term
    u = acc + dwp_ref[0:1, :]
    u = _layer_norm(u, dwp_ref[1:2, :], dwp_ref[2:3, :])
    u_ref[0] = _silu(u).astype(BF16)


def _ab_front(x, wm, ws, cw, gp, dww, dwp, ts):
    B, S, D = x.shape
    grid = (B, S // ts)
    tok = lambda w: pl.BlockSpec((1, ts, w), lambda b, s: (b, s, 0))
    full = lambda a: pl.BlockSpec(a.shape, lambda b, s: (0,) * a.ndim)
    out_shape = (
        jax.ShapeDtypeStruct((B, S, A_DIM), F32),
        jax.ShapeDtypeStruct((B, S, A_DIM), F32),
        jax.ShapeDtypeStruct((B, S, A_DIM), F32),
        jax.ShapeDtypeStruct((B, S, A_DIM), F32),
        jax.ShapeDtypeStruct((B, S, LANES), F32),
        jax.ShapeDtypeStruct((B, 8, S), F32),
        jax.ShapeDtypeStruct((B, S, B_CH), BF16),
    )
    out_specs = (tok(A_DIM), tok(A_DIM), tok(A_DIM), tok(A_DIM), tok(LANES),
                 pl.BlockSpec((1, 8, ts), lambda b, s: (b, 0, s)), tok(B_CH))
    return pl.pallas_call(
        _ab_front_kernel,
        grid=grid,
        in_specs=[tok(D), full(wm), full(ws), full(cw), full(gp), full(dww), full(dwp)],
        out_specs=out_specs,
        out_shape=out_shape,
        scratch_shapes=[pltpu.VMEM((ts + QKV_HALO, 3 * A_DIM), F32),
                        pltpu.VMEM((ts + DW_HALO, B_CH), F32),
                        pltpu.VMEM((ts + DW_HALO - 8, B_CH), F32)],
        compiler_params=_params("arbitrary", "arbitrary"),
        name="ab_front",
    )(x, wm, ws, cw, gp, dww, dwp)


def _gdn_kernel(q_ref, k_ref, v_ref, z_ref, sc_ref, sct_ref, ng_ref, o_ref, st_ref):
    s = pl.program_id(1)
    ts = q_ref.shape[1]

    @pl.when(s == 0)
    def _():
        st_ref[...] = jnp.zeros(st_ref.shape, F32)

    row = lax.broadcasted_iota(jnp.int32, (CHUNK, CHUNK), 0)
    col = lax.broadcasted_iota(jnp.int32, (CHUNK, CHUNK), 1)
    causal = row >= col
    strict = row > col
    eye = jnp.where(row == col, 1.0, 0.0).astype(F32)

    n_chunks = ts // CHUNK
    pairs = [(c, h) for c in range(n_chunks) for h in range(N_HEADS)]
    pw, tinv, attn, rhs, k_dec, q_dec, d_last = [], [], [], [], [], [], []
    for c, h in pairs:
        r0, r1 = c * CHUNK, (c + 1) * CHUNK
        lo, hi = h * A_HEAD, (h + 1) * A_HEAD
        qh = q_ref[0, r0:r1, lo:hi]
        kh = k_ref[0, r0:r1, lo:hi]
        vh = v_ref[0, r0:r1, lo:hi]
        beta = sc_ref[0, r0:r1, h:h + 1]
        gc = sc_ref[0, r0:r1, N_HEADS + h:N_HEADS + h + 1]
        gc_row = sct_ref[0, N_HEADS + h:N_HEADS + h + 1, r0:r1]
        gc_last = gc[CHUNK - 1:CHUNK, :]
        decay = jnp.where(causal, jnp.exp(jnp.where(causal, gc - gc_row, 0.0)), 0.0)
        kb = kh * beta
        egc = jnp.exp(gc)
        kq = _dot_nt(jnp.concatenate([kb, qh], axis=0).astype(BF16), kh.astype(BF16))
        a_neg = jnp.where(strict, -kq[0:CHUNK] * decay, 0.0)
        pw.append(a_neg)
        tinv.append(eye + a_neg)
        attn.append((kq[CHUNK:2 * CHUNK] * decay).astype(BF16))
        rhs.append(jnp.concatenate([vh * beta, kb * egc], axis=1).astype(BF16))
        k_dec.append((kh * jnp.exp(gc_last - gc)).astype(BF16))
        q_dec.append(qh * egc)
        d_last.append(jnp.exp(gc_last))
    head_blk = lax.broadcasted_iota(jnp.int32, (CHUNK, N_HEADS * CHUNK), 1) // CHUNK

    def block_diag(m):
        return jnp.concatenate([jnp.where(head_blk == h, m, 0.0) for h in range(N_HEADS)],
                               axis=0).astype(BF16)

    side_by_side = lambda mats, c: jnp.concatenate(mats[c * N_HEADS:(c + 1) * N_HEADS], axis=1)
    qc = [side_by_side(pw, c) for c in range(n_chunks)]
    yc = [side_by_side(tinv, c) for c in range(n_chunks)]
    qc = [_dot(qc[c].astype(BF16), block_diag(qc[c])) for c in range(n_chunks)]
    for _ in range(4):
        both = [_dot(jnp.concatenate([yc[c], qc[c]], axis=0).astype(BF16), block_diag(qc[c]))
                for c in range(n_chunks)]
        yc = [yc[c] + both[c][0:CHUNK] for c in range(n_chunks)]
        qc = [both[c][CHUNK:2 * CHUNK] for c in range(n_chunks)]
    last = [_dot(yc[c].astype(BF16), block_diag(qc[c])) for c in range(n_chunks)]
    yc = [yc[c] + last[c] for c in range(n_chunks)]
    tinv = [yc[c][:, h * CHUNK:(h + 1) * CHUNK] for c, h in pairs]
    uw = [_dot(tinv[i].astype(BF16), rhs[i]).astype(BF16) for i in range(len(pairs))]
    auw = [_dot(attn[i], uw[i]) for i in range(len(pairs))]
    kuw = [_dot_tn(k_dec[i], uw[i]) for i in range(len(pairs))]
    st = [st_ref[h] for h in range(N_HEADS)]
    for i, (c, h) in enumerate(pairs):
        r0, r1 = c * CHUNK, (c + 1) * CHUNK
        lo, hi = h * A_HEAD, (h + 1) * A_HEAD
        stb = st[h].astype(BF16)
        q_eff = (q_dec[i] - auw[i][:, A_HEAD:]).astype(BF16)
        o = _dot(q_eff, stb) + auw[i][:, 0:A_HEAD]
        st[h] = (d_last[i] * st[h] + kuw[i][:, 0:A_HEAD]
                 - _dot(kuw[i][:, A_HEAD:].astype(BF16), stb))
        o = o * lax.rsqrt(jnp.mean(o * o, axis=-1, keepdims=True) + 1e-6) * ng_ref[0:1, :]
        o_ref[0, r0:r1, lo:hi] = (o * _silu(z_ref[0, r0:r1, lo:hi])).astype(BF16)
    for h in range(N_HEADS):
        st_ref[h] = st[h]


def _gdn(q, k, v, z, sc, sct, ng, ts):
    B, S, _ = q.shape
    tok = lambda w: pl.BlockSpec((1, ts, w), lambda b, s: (b, s, 0))
    return pl.pallas_call(
        _gdn_kernel,
        grid=(B, S // ts),
        in_specs=[tok(A_DIM), tok(A_DIM), tok(A_DIM), tok(A_DIM), tok(LANES),
                  pl.BlockSpec((1, 8, ts), lambda b, s: (b, 0, s)),
                  pl.BlockSpec(ng.shape, lambda b, s: (0, 0))],
        out_specs=tok(A_DIM),
        out_shape=jax.ShapeDtypeStruct((B, S, A_DIM), BF16),
        scratch_shapes=[pltpu.VMEM((N_HEADS, A_HEAD, A_HEAD), F32)],
        compiler_params=_params("arbitrary", "arbitrary"),
        name="gdn_scan",
    )(q, k, v, z, sc, sct, ng)


def _token_rows(s, n_tokens):
    return pl.ds(s, n_tokens, stride=D_MODEL // LANES)


def _proj_ln_kernel(*refs, n_in, token_major_copy):
    a_refs = refs[0:n_in]
    w_refs = refs[n_in:2 * n_in]
    x_ref, lnp_ref, o_ref = refs[2 * n_in:2 * n_in + 3]
    mix = None
    for a_ref, w_ref in zip(a_refs, w_refs):
        t = _dot(a_ref[...], w_ref[...])
        mix = t if mix is None else mix + t
    y = _layer_norm(DN_ALPHA * x_ref[...] + mix, lnp_ref[0:1, :], lnp_ref[1:2, :])
    o_ref[...] = y
    if token_major_copy:
        ot_ref = refs[2 * n_in + 3]
        for s in range(D_MODEL // LANES):
            ot_ref[_token_rows(s, y.shape[0]), :] = y[:, s * LANES:(s + 1) * LANES]


def _proj_ln(acts, weights, x, lnp, tm, token_major_copy=False):
    T, D = x.shape
    n_in = len(acts)
    in_specs = ([pl.BlockSpec((tm, a.shape[1]), lambda i: (i, 0)) for a in acts]
                + [pl.BlockSpec(w.shape, lambda i: (0, 0)) for w in weights]
                + [pl.BlockSpec((tm, D), lambda i: (i, 0)),
                   pl.BlockSpec(lnp.shape, lambda i: (0, 0))])
    out_specs = [pl.BlockSpec((tm, D), lambda i: (i, 0))]
    out_shape = [jax.ShapeDtypeStruct((T, D), F32)]
    if token_major_copy:
        out_specs.append(pl.BlockSpec((tm * D // LANES, LANES), lambda i: (i, 0)))
        out_shape.append(jax.ShapeDtypeStruct((T * D // LANES, LANES), F32))
    outs = pl.pallas_call(
        functools.partial(_proj_ln_kernel, n_in=n_in, token_major_copy=token_major_copy),
        grid=(T // tm,),
        in_specs=in_specs,
        out_specs=tuple(out_specs),
        out_shape=tuple(out_shape),
        compiler_params=_params("arbitrary"),
        name="proj_ln",
    )(*acts, *weights, x, lnp)
    return outs if token_major_copy else outs[0]


def _ffn_epilogue(x, acc, lnp_ref, p_ref, wpg_ref, wpp_ref, o_ref):
    y = _layer_norm(DN_ALPHA * x + acc, lnp_ref[0:1, :], lnp_ref[1:2, :])
    gate = _sigmoid(_dot(y.astype(BF16), wpg_ref[...]))
    o_ref[...] = y + gate * _dot(p_ref[...].astype(BF16), wpp_ref[...])


def _ffn_kernel(a1_ref, a2_ref, w1_ref, w2_ref, x_ref, lnm_ref,
                wg_ref, wu_ref, wd_ref, lnp_ref, p_ref, wpg_ref, wpp_ref,
                o_ref, xb_ref, x1_ref):
    j = pl.program_id(1)

    @pl.when(j == 0)
    def _():
        mix = _dot(a1_ref[...], w1_ref[...]) + _dot(a2_ref[...], w2_ref[...])
        x1 = _layer_norm(DN_ALPHA * x_ref[...] + mix, lnm_ref[0:1, :], lnm_ref[1:2, :])
        x1_ref[...] = x1
        xb_ref[...] = x1.astype(BF16)
        o_ref[...] = jnp.zeros(o_ref.shape, F32)

    xb = xb_ref[...]
    hid = _silu(_dot(xb, wg_ref[...])) * _dot(xb, wu_ref[...])
    o_ref[...] += _dot(hid.astype(BF16), wd_ref[...])

    @pl.when(j == pl.num_programs(1) - 1)
    def _():
        _ffn_epilogue(x1_ref[...], o_ref[...], lnp_ref, p_ref, wpg_ref, wpp_ref, o_ref)


def _ffn(a1, a2, w1, w2, x, lnm, wg, wu, wd, lnp, p, wpg, wpp, tm, tf):
    T, D = x.shape
    F = wg.shape[1]
    return pl.pallas_call(
        _ffn_kernel,
        grid=(T // tm, F // tf),
        in_specs=[pl.BlockSpec((tm, a1.shape[1]), lambda i, j: (i, 0)),
                  pl.BlockSpec((tm, a2.shape[1]), lambda i, j: (i, 0)),
                  pl.BlockSpec(w1.shape, lambda i, j: (0, 0)),
                  pl.BlockSpec(w2.shape, lambda i, j: (0, 0)),
                  pl.BlockSpec((tm, D), lambda i, j: (i, 0)),
                  pl.BlockSpec(lnm.shape, lambda i, j: (0, 0)),
                  pl.BlockSpec((D, tf), lambda i, j: (0, j)),
                  pl.BlockSpec((D, tf), lambda i, j: (0, j)),
                  pl.BlockSpec((tf, D), lambda i, j: (j, 0)),
                  pl.BlockSpec(lnp.shape, lambda i, j: (0, 0)),
                  pl.BlockSpec((tm, PLE_DIM), lambda i, j: (i, 0)),
                  pl.BlockSpec(wpg.shape, lambda i, j: (0, 0)),
                  pl.BlockSpec(wpp.shape, lambda i, j: (0, 0))],
        out_specs=pl.BlockSpec((tm, D), lambda i, j: (i, 0)),
        out_shape=jax.ShapeDtypeStruct((T, D), F32),
        scratch_shapes=[pltpu.VMEM((tm, D), BF16), pltpu.VMEM((tm, D), F32)],
        compiler_params=_params("arbitrary", "arbitrary"),
        name="ffn",
    )(a1, a2, w1, w2, x, lnm, wg, wu, wd, lnp, p, wpg, wpp)


ROUTE_IDX1, ROUTE_IDX2, ROUTE_W1, ROUTE_W2, ROUTE_RANK1, ROUTE_RANK2 = range(6)


def _router_kernel(x_ref, wh_ref, wl_ref, br_ref, info_ref, info_t_ref, cnt_ref, carry_ref):
    i = pl.program_id(0)

    @pl.when(i == 0)
    def _():
        carry_ref[...] = jnp.zeros(carry_ref.shape, F32)

    x = x_ref[...]
    tm = x.shape[0]
    xh = x.astype(BF16)
    xl = (x - xh.astype(F32)).astype(BF16)
    logits = (_dot(xh, wh_ref[...]) + _dot(xl, wh_ref[...]) + _dot(xh, wl_ref[...])
              + br_ref[0:1, :])
    lane = lax.broadcasted_iota(jnp.int32, logits.shape, 1)
    neg = jnp.float32(-jnp.inf)
    logits = jnp.where(lane < N_EXPERTS, logits, neg)
    m1 = jnp.max(logits, axis=-1, keepdims=True)
    i1 = jnp.min(jnp.where(logits == m1, lane, LANES), axis=-1, keepdims=True)
    sel1 = lane == i1
    rest = jnp.where(sel1, neg, logits)
    m2 = jnp.max(rest, axis=-1, keepdims=True)
    i2 = jnp.min(jnp.where(rest == m2, lane, LANES), axis=-1, keepdims=True)
    sel2 = lane == i2
    e2 = jnp.exp(m2 - m1)
    den = 1.0 + e2

    sel = jnp.where(jnp.logical_or(sel1, sel2), 1.0, 0.0)
    row = lax.broadcasted_iota(jnp.int32, (tm, tm), 0)
    col = lax.broadcasted_iota(jnp.int32, (tm, tm), 1)
    earlier = jnp.where(row > col, 1.0, 0.0).astype(BF16)
    rank = _dot(earlier, sel.astype(BF16)) + carry_ref[0:1, :]
    rank1 = jnp.sum(jnp.where(sel1, rank, 0.0), axis=-1, keepdims=True)
    rank2 = jnp.sum(jnp.where(sel2, rank, 0.0), axis=-1, keepdims=True)

    info = jnp.zeros(logits.shape, F32)
    for slot, val in ((ROUTE_IDX1, i1.astype(F32)), (ROUTE_IDX2, i2.astype(F32)),
                      (ROUTE_W1, 1.0 / den), (ROUTE_W2, e2 / den),
                      (ROUTE_RANK1, rank1), (ROUTE_RANK2, rank2)):
        info = jnp.where(lane == slot, val, info)
    info_ref[...] = info
    info_t_ref[...] = jnp.transpose(info)[0:8, :]
    carry_ref[0:1, :] = carry_ref[0:1, :] + jnp.sum(sel, axis=0, keepdims=True)
    cnt_ref[...] = jnp.broadcast_to(carry_ref[0:1, :], cnt_ref.shape)


def _router(x, wh, wl, br, tm):
    T, D = x.shape
    return pl.pallas_call(
        _router_kernel,
        grid=(T // tm,),
        in_specs=[pl.BlockSpec((tm, D), lambda i: (i, 0)),
                  pl.BlockSpec(wh.shape, lambda i: (0, 0)),
                  pl.BlockSpec(wl.shape, lambda i: (0, 0)),
                  pl.BlockSpec(br.shape, lambda i: (0, 0))],
        out_specs=(pl.BlockSpec((tm, LANES), lambda i: (i, 0)),
                   pl.BlockSpec((8, tm), lambda i: (0, i)),
                   pl.BlockSpec((8, LANES), lambda i: (0, 0))),
        out_shape=(jax.ShapeDtypeStruct((T, LANES), F32),
                   jax.ShapeDtypeStruct((8, T), F32),
                   jax.ShapeDtypeStruct((8, LANES), F32)),
        scratch_shapes=[pltpu.VMEM((8, LANES), F32)],
        compiler_params=_params("arbitrary"),
        name="router",
    )(x, wh, wl, br)


def _moe_expert_kernel(te_ref, nv_ref, nr_ref, src_ref, dst_ref, x_hbm, wg_ref, wu_ref, wd_ref, y_hbm,
                       xg_ref, xb_ref, acc_ref, yo_ref, gidx_ref, sidx_ref,
                       sem_idx, sem_g, sem_s, *, n_j):
    s = pl.program_id(0)
    j = pl.program_id(1)
    tm = xb_ref.shape[0]
    rows_per_step = tm // n_j
    n_blk = D_MODEL // LANES
    nv = nv_ref[0]
    gather_on = s < nv
    compute_on = jnp.logical_and(s >= 1, s - 1 < nv)
    scatter_on = jnp.logical_and(s >= 2, s - 2 < nv)
    slot_cur = lax.rem(s, 2)
    slot_prev = 1 - slot_cur

    def token_tile(first_row):
        return pl.ds(pl.multiple_of(first_row, n_blk), n_blk)

    def gather_copy(slot, r, src_row):
        return pltpu.make_async_copy(x_hbm.at[token_tile(src_row), :],
                                     xg_ref.at[slot, token_tile(r * n_blk), :], sem_g)

    def scatter_copy(slot, r, dst_row):
        return pltpu.make_async_copy(yo_ref.at[slot, token_tile(r * n_blk), :],
                                     y_hbm.at[token_tile(dst_row), :], sem_s)

    def real_rows(tile):
        return nr_ref[jnp.clip(tile, 0, nr_ref.shape[0] - 1)]

    def drain(copy_of_row, n_rows=None):
        def body(g, carry):
            r0 = pl.multiple_of(g * SUBLANES, SUBLANES)
            for u in range(SUBLANES):
                copy_of_row(r0 + u).wait()
            return carry

        def one(r, carry):
            copy_of_row(r).wait()
            return carry

        def all_rows():
            lax.fori_loop(0, tm // SUBLANES, body, 0)

        def some_rows():
            lax.fori_loop(0, n_rows, one, 0)

        if n_rows is None:
            all_rows()
        else:
            pl.when(n_rows == tm)(all_rows)
            pl.when(n_rows < tm)(some_rows)

    stage_g = pltpu.make_async_copy(src_ref.at[0], gidx_ref, sem_idx.at[0])
    stage_s = pltpu.make_async_copy(dst_ref.at[0], sidx_ref, sem_idx.at[1])

    @pl.when(j == 0)
    def _():
        pl.when(gather_on)(stage_g.start)
        pl.when(scatter_on)(stage_s.start)

        @pl.when(compute_on)
        def _():
            drain(lambda r: gather_copy(slot_prev, r, 0))
            for blk in range(n_blk):
                xb_ref[:, blk * LANES:(blk + 1) * LANES] = (
                    xg_ref[slot_prev, _token_rows(blk, tm), :].astype(BF16))
            acc_ref[...] = jnp.zeros(acc_ref.shape, F32)

        @pl.when(jnp.logical_and(s >= 3, s - 3 < nv))
        def _():
            drain(lambda r: scatter_copy(slot_prev, r, 0), real_rows(s - 3))

        pl.when(gather_on)(stage_g.wait)
        pl.when(scatter_on)(stage_s.wait)

    base = pl.multiple_of(j * rows_per_step, rows_per_step)

    def issue_gathers():
        for u in range(rows_per_step):
            gather_copy(slot_cur, base + u, gidx_ref[0, base + u]).start(priority=1)

    def issue_scatters():
        n_real = real_rows(s - 2)

        @pl.when(base + rows_per_step <= n_real)
        def _():
            for u in range(rows_per_step):
                scatter_copy(slot_cur, base + u, sidx_ref[0, base + u]).start()

        @pl.when(base + rows_per_step > n_real)
        def _():
            def one(r, carry):
                scatter_copy(slot_cur, r, sidx_ref[0, r]).start()
                return carry
            lax.fori_loop(base, jnp.maximum(base, jnp.minimum(base + rows_per_step, n_real)),
                          one, 0)

    def compute():
        xb = xb_ref[...]
        hid = _silu(_dot(xb, wg_ref[0])) * _dot(xb, wu_ref[0])
        acc_ref[...] += _dot(hid.astype(BF16), wd_ref[0])

    pl.when(gather_on)(issue_gathers)
    pl.when(compute_on)(compute)
    pl.when(scatter_on)(issue_scatters)

    last_j = j == n_j - 1

    @pl.when(jnp.logical_and(last_j, compute_on))
    def _():
        for blk in range(n_blk):
            yo_ref[slot_prev, _token_rows(blk, tm), :] = acc_ref[:, blk * LANES:(blk + 1) * LANES]

    @pl.when(jnp.logical_and(jnp.logical_and(last_j, s == pl.num_programs(0) - 1), scatter_on))
    def _():
        drain(lambda r: scatter_copy(slot_cur, r, 0), real_rows(s - 2))


def _moe_experts(tile_expert, n_valid, n_real, src_rows, dst_rows, x, wg, wu, wd, n_out_rows,
                 tm, tf):
    E, D, F = wg.shape
    n_tiles = src_rows.shape[0]
    n_j = F // tf
    assert tm % (n_j * SUBLANES) == 0

    def computing(s, nv):
        return jnp.logical_and(s >= 1, s - 1 < nv[0])

    def tile_of(s):
        return jnp.clip(s - 1, 0, n_tiles - 1)

    def w_map(s, j, te, nv, nr):
        return (te[tile_of(s)], 0, jnp.where(computing(s, nv), j, 0))

    def wd_map(s, j, te, nv, nr):
        return (te[tile_of(s)], jnp.where(computing(s, nv), j, 0), 0)

    grid_spec = pltpu.PrefetchScalarGridSpec(
        num_scalar_prefetch=3,
        grid=(n_tiles + 2, n_j),
        in_specs=[pl.BlockSpec((1, 1, tm),
                               lambda s, j, te, nv, nr: (jnp.minimum(s, n_tiles - 1), 0, 0)),
                  pl.BlockSpec((1, 1, tm),
                               lambda s, j, te, nv, nr: (jnp.clip(s - 2, 0, n_tiles - 1), 0, 0)),
                  pl.BlockSpec(memory_space=pl.ANY),
                  pl.BlockSpec((1, D, tf), w_map),
                  pl.BlockSpec((1, D, tf), w_map),
                  pl.BlockSpec((1, tf, D), wd_map)],
        out_specs=pl.BlockSpec(memory_space=pl.ANY),
        scratch_shapes=[pltpu.VMEM((2, tm * D // LANES, LANES), F32),
                        pltpu.VMEM((tm, D), BF16),
                        pltpu.VMEM((tm, D), F32),
                        pltpu.VMEM((2, tm * D // LANES, LANES), F32),
                        pltpu.SMEM((1, tm), jnp.int32), pltpu.SMEM((1, tm), jnp.int32),
                        pltpu.SemaphoreType.DMA((2,)), pltpu.SemaphoreType.DMA(()),
                        pltpu.SemaphoreType.DMA(())],
    )
    return pl.pallas_call(
        functools.partial(_moe_expert_kernel, n_j=n_j),
        grid_spec=grid_spec,
        out_shape=jax.ShapeDtypeStruct((n_out_rows * D // LANES, LANES), F32),
        compiler_params=_params("arbitrary", "arbitrary"),
        name="moe_experts",
    )(tile_expert, n_valid, n_real, src_rows, dst_rows, x, wg, wu, wd)


def _moe_combine_kernel(x_ref, info_ref, ya_ref, yb_ref, lnp_ref, p_ref, wpg_ref, wpp_ref, o_ref):
    info = info_ref[...]
    tm = x_ref.shape[0]
    w1 = info[:, ROUTE_W1:ROUTE_W1 + 1]
    w2 = info[:, ROUTE_W2:ROUTE_W2 + 1]
    mix = jnp.concatenate(
        [w1 * ya_ref[_token_rows(blk, tm), :] + w2 * yb_ref[_token_rows(blk, tm), :]
         for blk in range(D_MODEL // LANES)], axis=1)
    _ffn_epilogue(x_ref[...], mix, lnp_ref, p_ref, wpg_ref, wpp_ref, o_ref)


def _moe_combine(x, info, y, lnp, p, wpg, wpp, tm):
    T, D = x.shape
    y_block = (tm * D // LANES, LANES)
    return pl.pallas_call(
        _moe_combine_kernel,
        grid=(T // tm,),
        in_specs=[pl.BlockSpec((tm, D), lambda i: (i, 0)),
                  pl.BlockSpec((tm, LANES), lambda i: (i, 0)),
                  pl.BlockSpec(y_block, lambda i: (i, 0)),
                  pl.BlockSpec(y_block, lambda i: (T // tm + i, 0)),
                  pl.BlockSpec(lnp.shape, lambda i: (0, 0)),
                  pl.BlockSpec((tm, PLE_DIM), lambda i: (i, 0)),
                  pl.BlockSpec(wpg.shape, lambda i: (0, 0)),
                  pl.BlockSpec(wpp.shape, lambda i: (0, 0))],
        out_specs=pl.BlockSpec((tm, D), lambda i: (i, 0)),
        out_shape=jax.ShapeDtypeStruct((T, D), F32),
        compiler_params=_params("arbitrary"),
        name="moe_combine",
    )(x, info, y, y, lnp, p, wpg, wpp)


def _route_plan(info_t, counts, tm):
    T = info_t.shape[1]
    n_tiles = (2 * T + N_EXPERTS * (tm - 1)) // tm + 1
    n_rows = n_tiles * tm
    as_int = lambda col: info_t[col].astype(jnp.int32)
    idx1, idx2 = as_int(ROUTE_IDX1), as_int(ROUTE_IDX2)
    counts = counts[0, :N_EXPERTS].astype(jnp.int32)
    padded = ((counts + tm - 1) // tm) * tm
    ends = jnp.cumsum(padded)
    starts = ends - padded
    grouped_row = jnp.concatenate([starts[idx1] + as_int(ROUTE_RANK1),
                                   starts[idx2] + as_int(ROUTE_RANK2)])
    out_row = jnp.arange(2 * T, dtype=jnp.int32)
    owner = jnp.full((n_rows,), -1, jnp.int32).at[grouped_row].set(out_row)
    real = owner >= 0
    dst = jnp.maximum(owner, 0)
    src = jnp.where(real, jnp.where(owner >= T, owner - T, owner), 0)
    n_real = jnp.sum(real.reshape(n_tiles, tm), axis=1).astype(jnp.int32)
    tile_start = jnp.arange(n_tiles, dtype=jnp.int32) * tm
    tile_expert = jnp.minimum(jnp.sum(tile_start[:, None] >= ends[None, :], axis=1),
                              N_EXPERTS - 1).astype(jnp.int32)
    n_valid = (ends[-1] // tm).astype(jnp.int32).reshape(1)
    shape = (n_tiles, 1, tm)
    n_blk = D_MODEL // LANES
    return (src * n_blk).reshape(shape), (dst * n_blk).reshape(shape), tile_expert, n_valid, n_real


def _c_front_kernel(x_ref, wm_ref, ws_ref, gb_ref, q_ref, k_ref, v_ref, op_ref, sc_ref, sct_ref):
    xb = x_ref[0].astype(BF16)
    q_ref[0] = _dot(xb, wm_ref[:, 0:C_QK]) * (C_QK_HEAD ** -0.5)
    k_ref[0] = _dot(xb, wm_ref[:, C_QK:2 * C_QK])
    v_ref[0] = _dot(xb, wm_ref[:, 2 * C_QK:2 * C_QK + C_V])
    op_ref[0] = _dot(xb, wm_ref[:, 2 * C_QK + C_V:2 * C_QK + 2 * C_V])
    hs = _dot(xb, ws_ref[...]) + gb_ref[0:1, :]
    capped = GATE_CAP * jnp.tanh(hs / GATE_CAP)
    lane = lax.broadcasted_iota(jnp.int32, hs.shape, 1)
    log_f = jnp.minimum(capped, 0.0) - jnp.log1p(jnp.exp(-jnp.abs(capped)))
    sc = jnp.where(lane < N_HEADS, capped, _chunk_cumsum(log_f))
    sc_ref[0] = sc
    sct_ref[0] = jnp.transpose(sc)[0:8, :]


def _c_front(x, wm, ws, gb, ts):
    B, S, D = x.shape
    tok = lambda w: pl.BlockSpec((1, ts, w), lambda b, s: (b, s, 0))
    full = lambda a: pl.BlockSpec(a.shape, lambda b, s: (0,) * a.ndim)
    out_shape = (
        jax.ShapeDtypeStruct((B, S, C_QK), F32),
        jax.ShapeDtypeStruct((B, S, C_QK), F32),
        jax.ShapeDtypeStruct((B, S, C_V), F32),
        jax.ShapeDtypeStruct((B, S, C_V), F32),
        jax.ShapeDtypeStruct((B, S, LANES), F32),
        jax.ShapeDtypeStruct((B, 8, S), F32),
    )
    out_specs = (tok(C_QK), tok(C_QK), tok(C_V), tok(C_V), tok(LANES),
                 pl.BlockSpec((1, 8, ts), lambda b, s: (b, 0, s)))
    return pl.pallas_call(
        _c_front_kernel,
        grid=(B, S // ts),
        in_specs=[tok(D), full(wm), full(ws), full(gb)],
        out_specs=out_specs,
        out_shape=out_shape,
        compiler_params=_params("arbitrary", "arbitrary"),
        name="c_front",
    )(x, wm, ws, gb)


def _mlstm_kernel(q_ref, k_ref, v_ref, op_ref, sc_ref, sct_ref, ng_ref, o_ref,
                  c_ref, n_ref, m_ref):
    s = pl.program_id(1)
    ts = q_ref.shape[1]

    @pl.when(s == 0)
    def _():
        c_ref[...] = jnp.zeros(c_ref.shape, F32)
        n_ref[...] = jnp.zeros(n_ref.shape, F32)
        m_ref[...] = jnp.zeros(m_ref.shape, F32)

    row = lax.broadcasted_iota(jnp.int32, (CHUNK, CHUNK), 0)
    col = lax.broadcasted_iota(jnp.int32, (CHUNK, CHUNK), 1)
    causal = row >= col
    neg = jnp.float32(-jnp.inf)

    n_chunks = ts // CHUNK
    pairs = [(c, h) for c in range(n_chunks) for h in range(N_HEADS)]
    n = len(pairs)
    each = lambda f: [f(i) for i in range(n)]
    rows = lambda i: slice(pairs[i][0] * CHUNK, (pairs[i][0] + 1) * CHUNK)
    kcols = lambda i: slice(pairs[i][1] * C_QK_HEAD, (pairs[i][1] + 1) * C_QK_HEAD)
    vcols = lambda i: slice(pairs[i][1] * C_V_HEAD, (pairs[i][1] + 1) * C_V_HEAD)
    head = lambda i: pairs[i][1]

    qs = each(lambda i: q_ref[0, rows(i), kcols(i)])
    ks = each(lambda i: k_ref[0, rows(i), kcols(i)])
    qbs = each(lambda i: qs[i].astype(BF16))
    vbs = each(lambda i: v_ref[0, rows(i), vcols(i)].astype(BF16))
    qk = each(lambda i: _dot_nt(qbs[i], ks[i].astype(BF16)))
    lis = each(lambda i: sc_ref[0, rows(i), head(i):head(i) + 1])
    bcs = each(lambda i: sc_ref[0, rows(i), N_HEADS + head(i):N_HEADS + head(i) + 1])
    b_lasts = each(lambda i: bcs[i][CHUNK - 1:CHUNK, :])
    bc_wide = each(lambda i: jnp.broadcast_to(bcs[i], (CHUNK, CHUNK)))
    dmat = each(lambda i: jnp.where(
        causal,
        bc_wide[i] - sct_ref[0, N_HEADS + head(i):N_HEADS + head(i) + 1, rows(i)]
        + sct_ref[0, head(i):head(i) + 1, rows(i)], neg))
    m_intras = each(lambda i: jnp.max(dmat[i], axis=-1, keepdims=True))
    pmat = each(lambda i: jnp.exp(dmat[i] - m_intras[i]) * qk[i])
    dens = each(lambda i: jnp.sum(pmat[i], axis=-1, keepdims=True))
    num_intra = each(lambda i: _dot(pmat[i].astype(BF16), vbs[i]))
    g_kvs = each(lambda i: b_lasts[i] - bcs[i] + lis[i])
    m_kvs = each(lambda i: jnp.max(g_kvs[i], axis=0, keepdims=True))

    m_run = [m_ref[h, :, 0:1] for h in range(N_HEADS)]
    m_sts, m_news = [], []
    for i in range(n):
        m_sts.append(m_run[head(i)])
        m_run[head(i)] = jnp.maximum(b_lasts[i] + m_run[head(i)], m_kvs[i])
        m_news.append(m_run[head(i)])
    decs = each(lambda i: jnp.exp(b_lasts[i] + m_sts[i] - m_news[i]))
    k_scale = each(lambda i: jnp.exp(g_kvs[i] - m_news[i]))
    kws = each(lambda i: ks[i] * k_scale[i])
    kv = each(lambda i: _dot_tn(kws[i].astype(BF16), vbs[i]))
    k_sum = each(lambda i: jnp.sum(kws[i], axis=0, keepdims=True))
    dec_c = each(lambda i: jnp.broadcast_to(decs[i], (C_QK_HEAD, C_V_HEAD)))
    c_run = [c_ref[h] for h in range(N_HEADS)]
    n_run = [n_ref[h] for h in range(N_HEADS)]
    c_sts, n_sts = [], []
    for i in range(n):
        c_sts.append(c_run[head(i)].astype(BF16))
        n_sts.append(n_run[head(i)])
        c_run[head(i)] = dec_c[i] * c_run[head(i)] + kv[i]
        n_run[head(i)] = decs[i] * n_run[head(i)] + k_sum[i]
    for h in range(N_HEADS):
        c_ref[h] = c_run[h]
        n_ref[h] = n_run[h]
        m_ref[h] = jnp.broadcast_to(m_run[h], (1, LANES))

    q_c = each(lambda i: _dot(qbs[i], c_sts[i]))
    q_n = each(lambda i: jnp.sum(qs[i] * n_sts[i], axis=-1, keepdims=True))
    inter = each(lambda i: bcs[i] + m_sts[i])
    m_t = each(lambda i: jnp.maximum(inter[i], m_intras[i]))
    s_inter = each(lambda i: jnp.exp(inter[i] - m_t[i]))
    s_intra = each(lambda i: jnp.exp(m_intras[i] - m_t[i]))
    den = each(lambda i: jnp.maximum(jnp.abs(s_inter[i] * q_n[i] + s_intra[i] * dens[i]),
                                     jnp.exp(-m_t[i])))
    hh = each(lambda i: (s_inter[i] * q_c[i] + s_intra[i] * num_intra[i]) / den[i])
    ms = each(lambda i: jnp.mean(hh[i] * hh[i], axis=-1, keepdims=True))
    scale = each(lambda i: lax.rsqrt(ms[i] + 1e-6))
    for i in range(n):
        gate = ng_ref[0:1, vcols(i)] * _sigmoid(op_ref[0, rows(i), vcols(i)])
        o_ref[0, rows(i), vcols(i)] = (hh[i] * scale[i] * gate).astype(BF16)


def _mlstm(q, k, v, op, sc, sct, ng, ts):
    B, S, _ = q.shape
    tok = lambda w: pl.BlockSpec((1, ts, w), lambda b, s: (b, s, 0))
    return pl.pallas_call(
        _mlstm_kernel,
        grid=(B, S // ts),
        in_specs=[tok(C_QK), tok(C_QK), tok(C_V), tok(C_V), tok(LANES),
                  pl.BlockSpec((1, 8, ts), lambda b, s: (b, 0, s)),
                  pl.BlockSpec(ng.shape, lambda b, s: (0, 0))],
        out_specs=tok(C_V),
        out_shape=jax.ShapeDtypeStruct((B, S, C_V), BF16),
        scratch_shapes=[pltpu.VMEM((N_HEADS, C_QK_HEAD, C_V_HEAD), F32),
                        pltpu.VMEM((N_HEADS, 1, C_QK_HEAD), F32),
                        pltpu.VMEM((N_HEADS, 1, LANES), F32)],
        compiler_params=_params("arbitrary", "arbitrary"),
        name="mlstm_scan",
    )(q, k, v, op, sc, sct, ng)


def _pad_lanes(a, offset=0):
    return jnp.pad(a, ((0, 0), (offset, LANES - offset - a.shape[1])))


def kernel(x, p, ab_w_in, ab_conv_qkv, ab_a_log, ab_dt_bias, ab_o_norm_g, ab_dw_w, ab_dw_b,
           ab_cn_g, ab_cn_b, ab_w_out, ffn_w_gate, ffn_w_up, ffn_w_down, c_w_in, c_b_i, c_b_f,
           c_norm_g, c_w_out, moe_w_router, moe_b_router, moe_w_gate, moe_w_up, moe_w_down,
           ln_mix_g, ln_mix_b, ln_ffn_g, ln_ffn_b, ple_w_proj, ple_w_gate):
    B, S, D = x.shape
    T = B * S
    ts_front, ts_scan, tm, tm_big, tf = 512, 512, 512, 1024, 512
    tm_moe = 7 * LANES

    w_in = ab_w_in[0]
    n_main = 4 * A_DIM
    wm = jnp.concatenate([w_in[:, :n_main], w_in[:, n_main + 2 * N_HEADS:]], axis=1).astype(BF16)
    ws = _pad_lanes(w_in[:, n_main:n_main + 2 * N_HEADS]).astype(BF16)
    gp = jnp.concatenate([_pad_lanes(ab_a_log[0][None, :], N_HEADS),
                          _pad_lanes(ab_dt_bias[0][None, :], N_HEADS)], axis=0)
    dwp = jnp.stack([ab_dw_b[0], ab_cn_g[0], ab_cn_b[0]], axis=0)
    q, k, v, z, sc, sct, u = _ab_front(x, wm, ws, ab_conv_qkv[0], gp, ab_dw_w[0], dwp, ts_front)
    o_a = _gdn(q, k, v, z, sc, sct, ab_o_norm_g[0][None, :], ts_scan)
    w_out = ab_w_out[0].astype(BF16)
    lnm = jnp.stack([ln_mix_g[0], ln_mix_b[0]], axis=0)
    lnp = jnp.stack([ln_ffn_g[0], ln_ffn_b[0]], axis=0)
    x2 = _ffn(o_a.reshape(T, A_DIM), u.reshape(T, B_CH), w_out[:A_DIM], w_out[A_DIM:],
              x.reshape(T, D), lnm,
              ffn_w_gate[0].astype(BF16), ffn_w_up[0].astype(BF16),
              ffn_w_down[0].astype(BF16), lnp, p[0].reshape(T, PLE_DIM),
              ple_w_gate[0].astype(BF16), ple_w_proj[0].astype(BF16), tm_big, tf)

    w_in = c_w_in[0]
    n_main = 2 * C_QK + 2 * C_V
    wm = w_in[:, :n_main].astype(BF16)
    ws = _pad_lanes(w_in[:, n_main:]).astype(BF16)
    gb = _pad_lanes(jnp.concatenate([c_b_i[0], c_b_f[0]])[None, :])
    q, k, v, op, sc, sct = _c_front(x2.reshape(B, S, D), wm, ws, gb, ts_front)
    hh = _mlstm(q, k, v, op, sc, sct, c_norm_g[0][None, :], ts_scan)
    lnp = jnp.stack([ln_mix_g[1], ln_mix_b[1]], axis=0)
    x3, x3_tok = _proj_ln([hh.reshape(T, C_V)], [c_w_out[0].astype(BF16)], x2, lnp, tm,
                          token_major_copy=True)

    wr = _pad_lanes(moe_w_router[0])
    wr_hi = wr.astype(BF16)
    wr_lo = (wr - wr_hi.astype(F32)).astype(BF16)
    info, info_t, counts = _router(x3, wr_hi, wr_lo, _pad_lanes(moe_b_router[0][None, :]), tm)
    src, dst, tile_expert, n_valid, n_real = _route_plan(info_t, counts, tm_moe)
    y = _moe_experts(tile_expert, n_valid, n_real, src, dst, x3_tok, moe_w_gate[0].astype(BF16),
                     moe_w_up[0].astype(BF16), moe_w_down[0].astype(BF16), 2 * T, tm_moe, tf)
    lnp = jnp.stack([ln_ffn_g[1], ln_ffn_b[1]], axis=0)
    x4 = _moe_combine(x3, info, y, lnp, p[1].reshape(T, PLE_DIM),
                      ple_w_gate[1].astype(BF16), ple_w_proj[1].astype(BF16), tm)
    return x4.reshape(B, S, D)
```

```python
import functools

import jax
import jax.numpy as jnp
from jax import lax
from jax.experimental import pallas as pl
from jax.experimental.pallas import tpu as pltpu

F32 = jnp.float32
BF16 = jnp.bfloat16

D_MODEL = 1024
N_HEADS = 4
CHUNK = 64
A_HEAD = 128
A_DIM = N_HEADS * A_HEAD
SHORT_CONV = 4
B_CH = 512
B_CONV = 31
C_QK_HEAD = 128
C_V_HEAD = 256
C_QK = N_HEADS * C_QK_HEAD
C_V = N_HEADS * C_V_HEAD
GATE_CAP = 15.0
D_FF = 3584
N_EXPERTS = 8
PLE_DIM = 256
DEPTH = 2
DN_ALPHA = (2 * DEPTH) ** 0.25
LANES = 128
SUBLANES = 8
QKV_HALO = 8
DW_HALO = 32
VMEM_LIMIT = 56 * 1024 * 1024


def _dot(a, b):
    return jnp.dot(a, b, preferred_element_type=F32)


def _dot_nt(a, b):
    return lax.dot_general(a, b, (((1,), (1,)), ((), ())), preferred_element_type=F32)


def _dot_tn(a, b):
    return lax.dot_general(a, b, (((0,), (0,)), ((), ())), preferred_element_type=F32)


def _sigmoid(t):
    return 1.0 / (1.0 + jnp.exp(-t))


def _silu(t):
    return t * _sigmoid(t)


def _softplus(t):
    return jnp.maximum(t, 0.0) + jnp.log1p(jnp.exp(-jnp.abs(t)))


def _layer_norm(t, g, b):
    mu = jnp.mean(t, axis=-1, keepdims=True)
    c = t - mu
    var = jnp.mean(c * c, axis=-1, keepdims=True)
    return c * lax.rsqrt(var + 1e-5) * g + b


def _chunk_cumsum(val):
    row = lax.broadcasted_iota(jnp.int32, val.shape, 0) & (CHUNK - 1)
    sh = 1
    while sh < CHUNK:
        val = val + jnp.where(row >= sh, pltpu.roll(val, sh, axis=0), 0.0)
        sh *= 2
    return val


def _params(*sem):
    return pltpu.CompilerParams(dimension_semantics=sem, vmem_limit_bytes=VMEM_LIMIT)


def _ab_front_kernel(x_ref, wm_ref, ws_ref, cw_ref, gp_ref, dww_ref, dwp_ref,
                     q_ref, k_ref, v_ref, z_ref, sc_ref, sct_ref, u_ref,
                     qkv_ext, u_ext, u_shift):
    s = pl.program_id(1)
    ts = x_ref.shape[1]
    xb = x_ref[0].astype(BF16)

    @pl.when(s == 0)
    def _():
        qkv_ext[0:QKV_HALO, :] = jnp.zeros((QKV_HALO, 3 * A_DIM), F32)
        u_ext[0:DW_HALO, :] = jnp.zeros((DW_HALO, B_CH), F32)

    @pl.when(s > 0)
    def _():
        qkv_ext[0:QKV_HALO, :] = qkv_ext[ts:ts + QKV_HALO, :]
        u_ext[0:DW_HALO, :] = u_ext[ts:ts + DW_HALO, :]

    qkv_ext[QKV_HALO:QKV_HALO + ts, :] = _dot(xb, wm_ref[:, 0:3 * A_DIM])
    acc = None
    for j in range(SHORT_CONV):
        off = QKV_HALO - (SHORT_CONV - 1) + j
        term = cw_ref[j:j + 1, :] * qkv_ext[off:off + ts, :]
        acc = term if acc is None else acc + term
    qkv = _silu(acc)
    for h in range(N_HEADS):
        lo, hi = h * A_HEAD, (h + 1) * A_HEAD
        qh = qkv[:, lo:hi]
        kh = qkv[:, A_DIM + lo:A_DIM + hi]
        qn = lax.rsqrt(jnp.sum(qh * qh, axis=-1, keepdims=True) + 1e-6)
        kn = lax.rsqrt(jnp.sum(kh * kh, axis=-1, keepdims=True) + 1e-6)
        q_ref[0, :, lo:hi] = qh * qn * (A_HEAD ** -0.5)
        k_ref[0, :, lo:hi] = kh * kn
    v_ref[0] = qkv[:, 2 * A_DIM:3 * A_DIM]
    z_ref[0] = _dot(xb, wm_ref[:, 3 * A_DIM:4 * A_DIM])

    hs = _dot(xb, ws_ref[...])
    lane = lax.broadcasted_iota(jnp.int32, hs.shape, 1)
    beta = _sigmoid(hs)
    g = -jnp.exp(gp_ref[0:1, :]) * _softplus(hs + gp_ref[1:2, :])
    sc = jnp.where(lane < N_HEADS, beta, _chunk_cumsum(g))
    sc_ref[0] = sc
    sct_ref[0] = jnp.transpose(sc)[0:8, :]

    glu = _dot(xb, wm_ref[:, 4 * A_DIM:4 * A_DIM + 2 * B_CH])
    u_ext[DW_HALO:DW_HALO + ts, :] = glu[:, 0:B_CH] * _sigmoid(glu[:, B_CH:2 * B_CH])
    first_off = DW_HALO - (B_CONV - 1)
    span = ts + DW_HALO - 8
    acc = None
    for res in range(8):
        offs = [o for o in range(first_off, DW_HALO + 1) if o % 8 == res]
        if res:
            u_shift[...] = u_ext[res:res + span, :]
        src = u_shift if res else u_ext
        for off in offs:
            lo = off - res if res else off
            term = dww_ref[off - first_off:off - first_off + 1, :] * src[lo:lo + ts, :]
            acc = term if acc is None else acc + term
    u = acc + dwp_ref[0:1, :]
    u = _layer_norm(u, dwp_ref[1:2, :], dwp_ref[2:3, :])
    u_ref[0] = _silu(u).astype(BF16)


def _ab_front(x, wm, ws, cw, gp, dww, dwp, ts):
    B, S, D = x.shape
    grid = (B, S // ts)
    tok = lambda w: pl.BlockSpec((1, ts, w), lambda b, s: (b, s, 0))
    full = lambda a: pl.BlockSpec(a.shape, lambda b, s: (0,) * a.ndim)
    out_shape = (
        jax.ShapeDtypeStruct((B, S, A_DIM), F32),
        jax.ShapeDtypeStruct((B, S, A_DIM), F32),
        jax.ShapeDtypeStruct((B, S, A_DIM), F32),
        jax.ShapeDtypeStruct((B, S, A_DIM), F32),
        jax.ShapeDtypeStruct((B, S, LANES), F32),
        jax.ShapeDtypeStruct((B, 8, S), F32),
        jax.ShapeDtypeStruct((B, S, B_CH), BF16),
    )
    out_specs = (tok(A_DIM), tok(A_DIM), tok(A_DIM), tok(A_DIM), tok(LANES),
                 pl.BlockSpec((1, 8, ts), lambda b, s: (b, 0, s)), tok(B_CH))
    return pl.pallas_call(
        _ab_front_kernel,
        grid=grid,
        in_specs=[tok(D), full(wm), full(ws), full(cw), full(gp), full(dww), full(dwp)],
        out_specs=out_specs,
        out_shape=out_shape,
        scratch_shapes=[pltpu.VMEM((ts + QKV_HALO, 3 * A_DIM), F32),
                        pltpu.VMEM((ts + DW_HALO, B_CH), F32),
                        pltpu.VMEM((ts + DW_HALO - 8, B_CH), F32)],
        compiler_params=_params("arbitrary", "arbitrary"),
        name="ab_front",
    )(x, wm, ws, cw, gp, dww, dwp)


def _gdn_kernel(q_ref, k_ref, v_ref, z_ref, sc_ref, sct_ref, ng_ref, o_ref, st_ref):
    s = pl.program_id(1)
    ts = q_ref.shape[1]

    @pl.when(s == 0)
    def _():
        st_ref[...] = jnp.zeros(st_ref.shape, F32)

    row = lax.broadcasted_iota(jnp.int32, (CHUNK, CHUNK), 0)
    col = lax.broadcasted_iota(jnp.int32, (CHUNK, CHUNK), 1)
    causal = row >= col
    strict = row > col
    eye = jnp.where(row == col, 1.0, 0.0).astype(F32)

    n_chunks = ts // CHUNK
    pairs = [(c, h) for c in range(n_chunks) for h in range(N_HEADS)]
    pw, tinv, attn, rhs, k_dec, q_dec, d_last = [], [], [], [], [], [], []
    for c, h in pairs:
        r0, r1 = c * CHUNK, (c + 1) * CHUNK
        lo, hi = h * A_HEAD, (h + 1) * A_HEAD
        qh = q_ref[0, r0:r1, lo:hi]
        kh = k_ref[0, r0:r1, lo:hi]
        vh = v_ref[0, r0:r1, lo:hi]
        beta = sc_ref[0, r0:r1, h:h + 1]
        gc = sc_ref[0, r0:r1, N_HEADS + h:N_HEADS + h + 1]
        gc_row = sct_ref[0, N_HEADS + h:N_HEADS + h + 1, r0:r1]
        gc_last = gc[CHUNK - 1:CHUNK, :]
        decay = jnp.where(causal, jnp.exp(jnp.where(causal, gc - gc_row, 0.0)), 0.0)
        kb = kh * beta
        egc = jnp.exp(gc)
        kq = _dot_nt(jnp.concatenate([kb, qh], axis=0).astype(BF16), kh.astype(BF16))
        a_neg = jnp.where(strict, -kq[0:CHUNK] * decay, 0.0)
        pw.append(a_neg)
        tinv.append(eye + a_neg)
        attn.append((kq[CHUNK:2 * CHUNK] * decay).astype(BF16))
        rhs.append(jnp.concatenate([vh * beta, kb * egc], axis=1).astype(BF16))
        k_dec.append((kh * jnp.exp(gc_last - gc)).astype(BF16))
        q_dec.append(qh * egc)
        d_last.append(jnp.exp(gc_last))
    head_blk = lax.broadcasted_iota(jnp.int32, (CHUNK, N_HEADS * CHUNK), 1) // CHUNK

    def block_diag(m):
        return jnp.concatenate([jnp.where(head_blk == h, m, 0.0) for h in range(N_HEADS)],
                               axis=0).astype(BF16)

    side_by_side = lambda mats, c: jnp.concatenate(mats[c * N_HEADS:(c + 1) * N_HEADS], axis=1)
    qc = [side_by_side(pw, c) for c in range(n_chunks)]
    yc = [side_by_side(tinv, c) for c in range(n_chunks)]
    qc = [_dot(qc[c].astype(BF16), block_diag(qc[c])) for c in range(n_chunks)]
    for _ in range(4):
        both = [_dot(jnp.concatenate([yc[c], qc[c]], axis=0).astype(BF16), block_diag(qc[c]))
                for c in range(n_chunks)]
        yc = [yc[c] + both[c][0:CHUNK] for c in range(n_chunks)]
        qc = [both[c][CHUNK:2 * CHUNK] for c in range(n_chunks)]
    last = [_dot(yc[c].astype(BF16), block_diag(qc[c])) for c in range(n_chunks)]
    yc = [yc[c] + last[c] for c in range(n_chunks)]
    tinv = [yc[c][:, h * CHUNK:(h + 1) * CHUNK] for c, h in pairs]
    uw = [_dot(tinv[i].astype(BF16), rhs[i]).astype(BF16) for i in range(len(pairs))]
    auw = [_dot(attn[i], uw[i]) for i in range(len(pairs))]
    kuw = [_dot_tn(k_dec[i], uw[i]) for i in range(len(pairs))]
    st = [st_ref[h] for h in range(N_HEADS)]
    for i, (c, h) in enumerate(pairs):
        r0, r1 = c * CHUNK, (c + 1) * CHUNK
        lo, hi = h * A_HEAD, (h + 1) * A_HEAD
        stb = st[h].astype(BF16)
        q_eff = (q_dec[i] - auw[i][:, A_HEAD:]).astype(BF16)
        o = _dot(q_eff, stb) + auw[i][:, 0:A_HEAD]
        st[h] = (d_last[i] * st[h] + kuw[i][:, 0:A_HEAD]
                 - _dot(kuw[i][:, A_HEAD:].astype(BF16), stb))
        o = o * lax.rsqrt(jnp.mean(o * o, axis=-1, keepdims=True) + 1e-6) * ng_ref[0:1, :]
        o_ref[0, r0:r1, lo:hi] = (o * _silu(z_ref[0, r0:r1, lo:hi])).astype(BF16)
    for h in range(N_HEADS):
        st_ref[h] = st[h]


def _gdn(q, k, v, z, sc, sct, ng, ts):
    B, S, _ = q.shape
    tok = lambda w: pl.BlockSpec((1, ts, w), lambda b, s: (b, s, 0))
    return pl.pallas_call(
        _gdn_kernel,
        grid=(B, S // ts),
        in_specs=[tok(A_DIM), tok(A_DIM), tok(A_DIM), tok(A_DIM), tok(LANES),
                  pl.BlockSpec((1, 8, ts), lambda b, s: (b, 0, s)),
                  pl.BlockSpec(ng.shape, lambda b, s: (0, 0))],
        out_specs=tok(A_DIM),
        out_shape=jax.ShapeDtypeStruct((B, S, A_DIM), BF16),
        scratch_shapes=[pltpu.VMEM((N_HEADS, A_HEAD, A_HEAD), F32)],
        compiler_params=_params("arbitrary", "arbitrary"),
        name="gdn_scan",
    )(q, k, v, z, sc, sct, ng)


def _token_rows(s, n_tokens):
    return pl.ds(s, n_tokens, stride=D_MODEL // LANES)


def _proj_ln_kernel(*refs, n_in, token_major_copy):
    a_refs = refs[0:n_in]
    w_refs = refs[n_in:2 * n_in]
    x_ref, lnp_ref, o_ref = refs[2 * n_in:2 * n_in + 3]
    mix = None
    for a_ref, w_ref in zip(a_refs, w_refs):
        t = _dot(a_ref[...], w_ref[...])
        mix = t if mix is None else mix + t
    y = _layer_norm(DN_ALPHA * x_ref[...] + mix, lnp_ref[0:1, :], lnp_ref[1:2, :])
    o_ref[...] = y
    if token_major_copy:
        ot_ref = refs[2 * n_in + 3]
        for s in range(D_MODEL // LANES):
            ot_ref[_token_rows(s, y.shape[0]), :] = y[:, s * LANES:(s + 1) * LANES]


def _proj_ln(acts, weights, x, lnp, tm, token_major_copy=False):
    T, D = x.shape
    n_in = len(acts)
    in_specs = ([pl.BlockSpec((tm, a.shape[1]), lambda i: (i, 0)) for a in acts]
                + [pl.BlockSpec(w.shape, lambda i: (0, 0)) for w in weights]
                + [pl.BlockSpec((tm, D), lambda i: (i, 0)),
                   pl.BlockSpec(lnp.shape, lambda i: (0, 0))])
    out_specs = [pl.BlockSpec((tm, D), lambda i: (i, 0))]
    out_shape = [jax.ShapeDtypeStruct((T, D), F32)]
    if token_major_copy:
        out_specs.append(pl.BlockSpec((tm * D // LANES, LANES), lambda i: (i, 0)))
        out_shape.append(jax.ShapeDtypeStruct((T * D // LANES, LANES), F32))
    outs = pl.pallas_call(
        functools.partial(_proj_ln_kernel, n_in=n_in, token_major_copy=token_major_copy),
        grid=(T // tm,),
        in_specs=in_specs,
        out_specs=tuple(out_specs),
        out_shape=tuple(out_shape),
        compiler_params=_params("arbitrary"),
        name="proj_ln",
    )(*acts, *weights, x, lnp)
    return outs if token_major_copy else outs[0]


def _ffn_epilogue(x, acc, lnp_ref, p_ref, wpg_ref, wpp_ref, o_ref):
    y = _layer_norm(DN_ALPHA * x + acc, lnp_ref[0:1, :], lnp_ref[1:2, :])
    gate = _sigmoid(_dot(y.astype(BF16), wpg_ref[...]))
    o_ref[...] = y + gate * _dot(p_ref[...].astype(BF16), wpp_ref[...])


def _ffn_kernel(a1_ref, a2_ref, w1_ref, w2_ref, x_ref, lnm_ref,
                wg_ref, wu_ref, wd_ref, lnp_ref, p_ref, wpg_ref, wpp_ref,
                o_ref, xb_ref, x1_ref):
    j = pl.program_id(1)

    @pl.when(j == 0)
    def _():
        mix = _dot(a1_ref[...], w1_ref[...]) + _dot(a2_ref[...], w2_ref[...])
        x1 = _layer_norm(DN_ALPHA * x_ref[...] + mix, lnm_ref[0:1, :], lnm_ref[1:2, :])
        x1_ref[...] = x1
        xb_ref[...] = x1.astype(BF16)
        o_ref[...] = jnp.zeros(o_ref.shape, F32)

    xb = xb_ref[...]
    hid = _silu(_dot(xb, wg_ref[...])) * _dot(xb, wu_ref[...])
    o_ref[...] += _dot(hid.astype(BF16), wd_ref[...])

    @pl.when(j == pl.num_programs(1) - 1)
    def _():
        _ffn_epilogue(x1_ref[...], o_ref[...], lnp_ref, p_ref, wpg_ref, wpp_ref, o_ref)


def _ffn(a1, a2, w1, w2, x, lnm, wg, wu, wd, lnp, p, wpg, wpp, tm, tf):
    T, D = x.shape
    F = wg.shape[1]
    return pl.pallas_call(
        _ffn_kernel,
        grid=(T // tm, F // tf),
        in_specs=[pl.BlockSpec((tm, a1.shape[1]), lambda i, j: (i, 0)),
                  pl.BlockSpec((tm, a2.shape[1]), lambda i, j: (i, 0)),
                  pl.BlockSpec(w1.shape, lambda i, j: (0, 0)),
                  pl.BlockSpec(w2.shape, lambda i, j: (0, 0)),
                  pl.BlockSpec((tm, D), lambda i, j: (i, 0)),
                  pl.BlockSpec(lnm.shape, lambda i, j: (0, 0)),
                  pl.BlockSpec((D, tf), lambda i, j: (0, j)),
                  pl.BlockSpec((D, tf), lambda i, j: (0, j)),
                  pl.BlockSpec((tf, D), lambda i, j: (j, 0)),
                  pl.BlockSpec(lnp.shape, lambda i, j: (0, 0)),
                  pl.BlockSpec((tm, PLE_DIM), lambda i, j: (i, 0)),
                  pl.BlockSpec(wpg.shape, lambda i, j: (0, 0)),
                  pl.BlockSpec(wpp.shape, lambda i, j: (0, 0))],
        out_specs=pl.BlockSpec((tm, D), lambda i, j: (i, 0)),
        out_shape=jax.ShapeDtypeStruct((T, D), F32),
        scratch_shapes=[pltpu.VMEM((tm, D), BF16), pltpu.VMEM((tm, D), F32)],
        compiler_params=_params("arbitrary", "arbitrary"),
        name="ffn",
    )(a1, a2, w1, w2, x, lnm, wg, wu, wd, lnp, p, wpg, wpp)


ROUTE_IDX1, ROUTE_IDX2, ROUTE_W1, ROUTE_W2, ROUTE_RANK1, ROUTE_RANK2 = range(6)


def _router_kernel(x_ref, wh_ref, wl_ref, br_ref, info_ref, info_t_ref, cnt_ref, carry_ref):
    i = pl.program_id(0)

    @pl.when(i == 0)
    def _():
        carry_ref[...] = jnp.zeros(carry_ref.shape, F32)

    x = x_ref[...]
    tm = x.shape[0]
    xh = x.astype(BF16)
    xl = (x - xh.astype(F32)).astype(BF16)
    logits = (_dot(xh, wh_ref[...]) + _dot(xl, wh_ref[...]) + _dot(xh, wl_ref[...])
              + br_ref[0:1, :])
    lane = lax.broadcasted_iota(jnp.int32, logits.shape, 1)
    neg = jnp.float32(-jnp.inf)
    logits = jnp.where(lane < N_EXPERTS, logits, neg)
    m1 = jnp.max(logits, axis=-1, keepdims=True)
    i1 = jnp.min(jnp.where(logits == m1, lane, LANES), axis=-1, keepdims=True)
    sel1 = lane == i1
    rest = jnp.where(sel1, neg, logits)
    m2 = jnp.max(rest, axis=-1, keepdims=True)
    i2 = jnp.min(jnp.where(rest == m2, lane, LANES), axis=-1, keepdims=True)
    sel2 = lane == i2
    e2 = jnp.exp(m2 - m1)
    den = 1.0 + e2

    sel = jnp.where(jnp.logical_or(sel1, sel2), 1.0, 0.0)
    row = lax.broadcasted_iota(jnp.int32, (tm, tm), 0)
    col = lax.broadcasted_iota(jnp.int32, (tm, tm), 1)
    earlier = jnp.where(row > col, 1.0, 0.0).astype(BF16)
    rank = _dot(earlier, sel.astype(BF16)) + carry_ref[0:1, :]
    rank1 = jnp.sum(jnp.where(sel1, rank, 0.0), axis=-1, keepdims=True)
    rank2 = jnp.sum(jnp.where(sel2, rank, 0.0), axis=-1, keepdims=True)

    info = jnp.zeros(logits.shape, F32)
    for slot, val in ((ROUTE_IDX1, i1.astype(F32)), (ROUTE_IDX2, i2.astype(F32)),
                      (ROUTE_W1, 1.0 / den), (ROUTE_W2, e2 / den),
                      (ROUTE_RANK1, rank1), (ROUTE_RANK2, rank2)):
        info = jnp.where(lane == slot, val, info)
    info_ref[...] = info
    info_t_ref[...] = jnp.transpose(info)[0:8, :]
    carry_ref[0:1, :] = carry_ref[0:1, :] + jnp.sum(sel, axis=0, keepdims=True)
    cnt_ref[...] = jnp.broadcast_to(carry_ref[0:1, :], cnt_ref.shape)


def _router(x, wh, wl, br, tm):
    T, D = x.shape
    return pl.pallas_call(
        _router_kernel,
        grid=(T // tm,),
        in_specs=[pl.BlockSpec((tm, D), lambda i: (i, 0)),
                  pl.BlockSpec(wh.shape, lambda i: (0, 0)),
                  pl.BlockSpec(wl.shape, lambda i: (0, 0)),
                  pl.BlockSpec(br.shape, lambda i: (0, 0))],
        out_specs=(pl.BlockSpec((tm, LANES), lambda i: (i, 0)),
                   pl.BlockSpec((8, tm), lambda i: (0, i)),
                   pl.BlockSpec((8, LANES), lambda i: (0, 0))),
        out_shape=(jax.ShapeDtypeStruct((T, LANES), F32),
                   jax.ShapeDtypeStruct((8, T), F32),
                   jax.ShapeDtypeStruct((8, LANES), F32)),
        scratch_shapes=[pltpu.VMEM((8, LANES), F32)],
        compiler_params=_params("arbitrary"),
        name="router",
    )(x, wh, wl, br)


def _moe_expert_kernel(te_ref, nv_ref, nr_ref, src_ref, dst_ref, x_hbm, wg_ref, wu_ref, wd_ref, y_hbm,
                       xg_ref, xb_ref, acc_ref, yo_ref, gidx_ref, sidx_ref,
                       sem_idx, sem_g, sem_s, *, n_j):
    s = pl.program_id(0)
    j = pl.program_id(1)
    tm = xb_ref.shape[0]
    rows_per_step = tm // n_j
    n_blk = D_MODEL // LANES
    nv = nv_ref[0]
    gather_on = s < nv
    compute_on = jnp.logical_and(s >= 1, s - 1 < nv)
    scatter_on = jnp.logical_and(s >= 2, s - 2 < nv)
    slot_cur = lax.rem(s, 2)
    slot_prev = 1 - slot_cur

    def token_tile(first_row):
        return pl.ds(pl.multiple_of(first_row, n_blk), n_blk)

    def gather_copy(slot, r, src_row):
        return pltpu.make_async_copy(x_hbm.at[token_tile(src_row), :],
                                     xg_ref.at[slot, token_tile(r * n_blk), :], sem_g)

    def scatter_copy(slot, r, dst_row):
        return pltpu.make_async_copy(yo_ref.at[slot, token_tile(r * n_blk), :],
                                     y_hbm.at[token_tile(dst_row), :], sem_s)

    def real_rows(tile):
        return nr_ref[jnp.clip(tile, 0, nr_ref.shape[0] - 1)]

    def drain(copy_of_row, n_rows=None):
        def body(g, carry):
            r0 = pl.multiple_of(g * SUBLANES, SUBLANES)
            for u in range(SUBLANES):
                copy_of_row(r0 + u).wait()
            return carry

        def one(r, carry):
            copy_of_row(r).wait()
            return carry

        def all_rows():
            lax.fori_loop(0, tm // SUBLANES, body, 0)

        def some_rows():
            lax.fori_loop(0, n_rows, one, 0)

        if n_rows is None:
            all_rows()
        else:
            pl.when(n_rows == tm)(all_rows)
            pl.when(n_rows < tm)(some_rows)

    stage_g = pltpu.make_async_copy(src_ref.at[0], gidx_ref, sem_idx.at[0])
    stage_s = pltpu.make_async_copy(dst_ref.at[0], sidx_ref, sem_idx.at[1])

    @pl.when(j == 0)
    def _():
        pl.when(gather_on)(stage_g.start)
        pl.when(scatter_on)(stage_s.start)

        @pl.when(compute_on)
        def _():
            drain(lambda r: gather_copy(slot_prev, r, 0))
            for blk in range(n_blk):
                xb_ref[:, blk * LANES:(blk + 1) * LANES] = (
                    xg_ref[slot_prev, _token_rows(blk, tm), :].astype(BF16))
            acc_ref[...] = jnp.zeros(acc_ref.shape, F32)

        @pl.when(jnp.logical_and(s >= 3, s - 3 < nv))
        def _():
            drain(lambda r: scatter_copy(slot_prev, r, 0), real_rows(s - 3))

        pl.when(gather_on)(stage_g.wait)
        pl.when(scatter_on)(stage_s.wait)

    base = pl.multiple_of(j * rows_per_step, rows_per_step)

    def issue_gathers():
        for u in range(rows_per_step):
            gather_copy(slot_cur, base + u, gidx_ref[0, base + u]).start(priority=1)

    def issue_scatters():
        n_real = real_rows(s - 2)

        @pl.when(base + rows_per_step <= n_real)
        def _():
            for u in range(rows_per_step):
                scatter_copy(slot_cur, base + u, sidx_ref[0, base + u]).start()

        @pl.when(base + rows_per_step > n_real)
        def _():
            def one(r, carry):
                scatter_copy(slot_cur, r, sidx_ref[0, r]).start()
                return carry
            lax.fori_loop(base, jnp.maximum(base, jnp.minimum(base + rows_per_step, n_real)),
                          one, 0)

    def compute():
        xb = xb_ref[...]
        hid = _silu(_dot(xb, wg_ref[0])) * _dot(xb, wu_ref[0])
        acc_ref[...] += _dot(hid.astype(BF16), wd_ref[0])

    pl.when(gather_on)(issue_gathers)
    pl.when(compute_on)(compute)
    pl.when(scatter_on)(issue_scatters)

    last_j = j == n_j - 1

    @pl.when(jnp.logical_and(last_j, compute_on))
    def _():
        for blk in range(n_blk):
            yo_ref[slot_prev, _token_rows(blk, tm), :] = acc_ref[:, blk * LANES:(blk + 1) * LANES]

    @pl.when(jnp.logical_and(jnp.logical_and(last_j, s == pl.num_programs(0) - 1), scatter_on))
    def _():
        drain(lambda r: scatter_copy(slot_cur, r, 0), real_rows(s - 2))


def _moe_experts(tile_expert, n_valid, n_real, src_rows, dst_rows, x, wg, wu, wd, n_out_rows,
                 tm, tf):
    E, D, F = wg.shape
    n_tiles = src_rows.shape[0]
    n_j = F // tf
    assert tm % (n_j * SUBLANES) == 0

    def computing(s, nv):
        return jnp.logical_and(s >= 1, s - 1 < nv[0])

    def tile_of(s):
        return jnp.clip(s - 1, 0, n_tiles - 1)

    def w_map(s, j, te, nv, nr):
        return (te[tile_of(s)], 0, jnp.where(computing(s, nv), j, 0))

    def wd_map(s, j, te, nv, nr):
        return (te[tile_of(s)], jnp.where(computing(s, nv), j, 0), 0)

    grid_spec = pltpu.PrefetchScalarGridSpec(
        num_scalar_prefetch=3,
        grid=(n_tiles + 2, n_j),
        in_specs=[pl.BlockSpec((1, 1, tm),
                               lambda s, j, te, nv, nr: (jnp.minimum(s, n_tiles - 1), 0, 0)),
                  pl.BlockSpec((1, 1, tm),
                               lambda s, j, te, nv, nr: (jnp.clip(s - 2, 0, n_tiles - 1), 0, 0)),
                  pl.BlockSpec(memory_space=pl.ANY),
                  pl.BlockSpec((1, D, tf), w_map),
                  pl.BlockSpec((1, D, tf), w_map),
                  pl.BlockSpec((1, tf, D), wd_map)],
        out_specs=pl.BlockSpec(memory_space=pl.ANY),
        scratch_shapes=[pltpu.VMEM((2, tm * D // LANES, LANES), F32),
                        pltpu.VMEM((tm, D), BF16),
                        pltpu.VMEM((tm, D), F32),
                        pltpu.VMEM((2, tm * D // LANES, LANES), F32),
                        pltpu.SMEM((1, tm), jnp.int32), pltpu.SMEM((1, tm), jnp.int32),
                        pltpu.SemaphoreType.DMA((2,)), pltpu.SemaphoreType.DMA(()),
                        pltpu.SemaphoreType.DMA(())],
    )
    return pl.pallas_call(
        functools.partial(_moe_expert_kernel, n_j=n_j),
        grid_spec=grid_spec,
        out_shape=jax.ShapeDtypeStruct((n_out_rows * D // LANES, LANES), F32),
        compiler_params=_params("arbitrary", "arbitrary"),
        name="moe_experts",
    )(tile_expert, n_valid, n_real, src_rows, dst_rows, x, wg, wu, wd)


def _moe_combine_kernel(x_ref, info_ref, ya_ref, yb_ref, lnp_ref, p_ref, wpg_ref, wpp_ref, o_ref):
    info = info_ref[...]
    tm = x_ref.shape[0]
    w1 = info[:, ROUTE_W1:ROUTE_W1 + 1]
    w2 = info[:, ROUTE_W2:ROUTE_W2 + 1]
    mix = jnp.concatenate(
        [w1 * ya_ref[_token_rows(blk, tm), :] + w2 * yb_ref[_token_rows(blk, tm), :]
         for blk in range(D_MODEL // LANES)], axis=1)
    _ffn_epilogue(x_ref[...], mix, lnp_ref, p_ref, wpg_ref, wpp_ref, o_ref)


def _moe_combine(x, info, y, lnp, p, wpg, wpp, tm):
    T, D = x.shape
    y_block = (tm * D // LANES, LANES)
    return pl.pallas_call(
        _moe_combine_kernel,
        grid=(T // tm,),
        in_specs=[pl.BlockSpec((tm, D), lambda i: (i, 0)),
                  pl.BlockSpec((tm, LANES), lambda i: (i, 0)),
                  pl.BlockSpec(y_block, lambda i: (i, 0)),
                  pl.BlockSpec(y_block, lambda i: (T // tm + i, 0)),
                  pl.BlockSpec(lnp.shape, lambda i: (0, 0)),
                  pl.BlockSpec((tm, PLE_DIM), lambda i: (i, 0)),
                  pl.BlockSpec(wpg.shape, lambda i: (0, 0)),
                  pl.BlockSpec(wpp.shape, lambda i: (0, 0))],
        out_specs=pl.BlockSpec((tm, D), lambda i: (i, 0)),
        out_shape=jax.ShapeDtypeStruct((T, D), F32),
        compiler_params=_params("arbitrary"),
        name="moe_combine",
    )(x, info, y, y, lnp, p, wpg, wpp)


def _route_plan(info_t, counts, tm):
    T = info_t.shape[1]
    n_tiles = (2 * T + N_EXPERTS * (tm - 1)) // tm + 1
    n_rows = n_tiles * tm
    as_int = lambda col: info_t[col].astype(jnp.int32)
    idx1, idx2 = as_int(ROUTE_IDX1), as_int(ROUTE_IDX2)
    counts = counts[0, :N_EXPERTS].astype(jnp.int32)
    padded = ((counts + tm - 1) // tm) * tm
    ends = jnp.cumsum(padded)
    starts = ends - padded
    grouped_row = jnp.concatenate([starts[idx1] + as_int(ROUTE_RANK1),
                                   starts[idx2] + as_int(ROUTE_RANK2)])
    out_row = jnp.arange(2 * T, dtype=jnp.int32)
    owner = jnp.full((n_rows,), -1, jnp.int32).at[grouped_row].set(out_row)
    real = owner >= 0
    dst = jnp.maximum(owner, 0)
    src = jnp.where(real, jnp.where(owner >= T, owner - T, owner), 0)
    n_real = jnp.sum(real.reshape(n_tiles, tm), axis=1).astype(jnp.int32)
    tile_start = jnp.arange(n_tiles, dtype=jnp.int32) * tm
    tile_expert = jnp.minimum(jnp.sum(tile_start[:, None] >= ends[None, :], axis=1),
                              N_EXPERTS - 1).astype(jnp.int32)
    n_valid = (ends[-1] // tm).astype(jnp.int32).reshape(1)
    shape = (n_tiles, 1, tm)
    n_blk = D_MODEL // LANES
    return (src * n_blk).reshape(shape), (dst * n_blk).reshape(shape), tile_expert, n_valid, n_real


def _c_front_kernel(x_ref, wm_ref, ws_ref, gb_ref, q_ref, k_ref, v_ref, op_ref, sc_ref, sct_ref):
    xb = x_ref[0].astype(BF16)
    q_ref[0] = _dot(xb, wm_ref[:, 0:C_QK]) * (C_QK_HEAD ** -0.5)
    k_ref[0] = _dot(xb, wm_ref[:, C_QK:2 * C_QK])
    v_ref[0] = _dot(xb, wm_ref[:, 2 * C_QK:2 * C_QK + C_V])
    op_ref[0] = _dot(xb, wm_ref[:, 2 * C_QK + C_V:2 * C_QK + 2 * C_V])
    hs = _dot(xb, ws_ref[...]) + gb_ref[0:1, :]
    capped = GATE_CAP * jnp.tanh(hs / GATE_CAP)
    lane = lax.broadcasted_iota(jnp.int32, hs.shape, 1)
    log_f = jnp.minimum(capped, 0.0) - jnp.log1p(jnp.exp(-jnp.abs(capped)))
    sc = jnp.where(lane < N_HEADS, capped, _chunk_cumsum(log_f))
    sc_ref[0] = sc
    sct_ref[0] = jnp.transpose(sc)[0:8, :]


def _c_front(x, wm, ws, gb, ts):
    B, S, D = x.shape
    tok = lambda w: pl.BlockSpec((1, ts, w), lambda b, s: (b, s, 0))
    full = lambda a: pl.BlockSpec(a.shape, lambda b, s: (0,) * a.ndim)
    out_shape = (
        jax.ShapeDtypeStruct((B, S, C_QK), F32),
        jax.ShapeDtypeStruct((B, S, C_QK), F32),
        jax.ShapeDtypeStruct((B, S, C_V), F32),
        jax.ShapeDtypeStruct((B, S, C_V), F32),
        jax.ShapeDtypeStruct((B, S, LANES), F32),
        jax.ShapeDtypeStruct((B, 8, S), F32),
    )
    out_specs = (tok(C_QK), tok(C_QK), tok(C_V), tok(C_V), tok(LANES),
                 pl.BlockSpec((1, 8, ts), lambda b, s: (b, 0, s)))
    return pl.pallas_call(
        _c_front_kernel,
        grid=(B, S // ts),
        in_specs=[tok(D), full(wm), full(ws), full(gb)],
        out_specs=out_specs,
        out_shape=out_shape,
        compiler_params=_params("arbitrary", "arbitrary"),
        name="c_front",
    )(x, wm, ws, gb)


def _mlstm_kernel(q_ref, k_ref, v_ref, op_ref, sc_ref, sct_ref, ng_ref, o_ref,
                  c_ref, n_ref, m_ref):
    s = pl.program_id(1)
    ts = q_ref.shape[1]

    @pl.when(s == 0)
    def _():
        c_ref[...] = jnp.zeros(c_ref.shape, F32)
        n_ref[...] = jnp.zeros(n_ref.shape, F32)
        m_ref[...] = jnp.zeros(m_ref.shape, F32)

    row = lax.broadcasted_iota(jnp.int32, (CHUNK, CHUNK), 0)
    col = lax.broadcasted_iota(jnp.int32, (CHUNK, CHUNK), 1)
    causal = row >= col
    neg = jnp.float32(-jnp.inf)

    n_chunks = ts // CHUNK
    pairs = [(c, h) for c in range(n_chunks) for h in range(N_HEADS)]
    n = len(pairs)
    each = lambda f: [f(i) for i in range(n)]
    rows = lambda i: slice(pairs[i][0] * CHUNK, (pairs[i][0] + 1) * CHUNK)
    kcols = lambda i: slice(pairs[i][1] * C_QK_HEAD, (pairs[i][1] + 1) * C_QK_HEAD)
    vcols = lambda i: slice(pairs[i][1] * C_V_HEAD, (pairs[i][1] + 1) * C_V_HEAD)
    head = lambda i: pairs[i][1]

    qs = each(lambda i: q_ref[0, rows(i), kcols(i)])
    ks = each(lambda i: k_ref[0, rows(i), kcols(i)])
    qbs = each(lambda i: qs[i].astype(BF16))
    vbs = each(lambda i: v_ref[0, rows(i), vcols(i)].astype(BF16))
    qk = each(lambda i: _dot_nt(qbs[i], ks[i].astype(BF16)))
    lis = each(lambda i: sc_ref[0, rows(i), head(i):head(i) + 1])
    bcs = each(lambda i: sc_ref[0, rows(i), N_HEADS + head(i):N_HEADS + head(i) + 1])
    b_lasts = each(lambda i: bcs[i][CHUNK - 1:CHUNK, :])
    bc_wide = each(lambda i: jnp.broadcast_to(bcs[i], (CHUNK, CHUNK)))
    dmat = each(lambda i: jnp.where(
        causal,
        bc_wide[i] - sct_ref[0, N_HEADS + head(i):N_HEADS + head(i) + 1, rows(i)]
        + sct_ref[0, head(i):head(i) + 1, rows(i)], neg))
    m_intras = each(lambda i: jnp.max(dmat[i], axis=-1, keepdims=True))
    pmat = each(lambda i: jnp.exp(dmat[i] - m_intras[i]) * qk[i])
    dens = each(lambda i: jnp.sum(pmat[i], axis=-1, keepdims=True))
    num_intra = each(lambda i: _dot(pmat[i].astype(BF16), vbs[i]))
    g_kvs = each(lambda i: b_lasts[i] - bcs[i] + lis[i])
    m_kvs = each(lambda i: jnp.max(g_kvs[i], axis=0, keepdims=True))

    m_run = [m_ref[h, :, 0:1] for h in range(N_HEADS)]
    m_sts, m_news = [], []
    for i in range(n):
        m_sts.append(m_run[head(i)])
        m_run[head(i)] = jnp.maximum(b_lasts[i] + m_run[head(i)], m_kvs[i])
        m_news.append(m_run[head(i)])
    decs = each(lambda i: jnp.exp(b_lasts[i] + m_sts[i] - m_news[i]))
    k_scale = each(lambda i: jnp.exp(g_kvs[i] - m_news[i]))
    kws = each(lambda i: ks[i] * k_scale[i])
    kv = each(lambda i: _dot_tn(kws[i].astype(BF16), vbs[i]))
    k_sum = each(lambda i: jnp.sum(kws[i], axis=0, keepdims=True))
    dec_c = each(lambda i: jnp.broadcast_to(decs[i], (C_QK_HEAD, C_V_HEAD)))
    c_run = [c_ref[h] for h in range(N_HEADS)]
    n_run = [n_ref[h] for h in range(N_HEADS)]
    c_sts, n_sts = [], []
    for i in range(n):
        c_sts.append(c_run[head(i)].astype(BF16))
        n_sts.append(n_run[head(i)])
        c_run[head(i)] = dec_c[i] * c_run[head(i)] + kv[i]
        n_run[head(i)] = decs[i] * n_run[head(i)] + k_sum[i]
    for h in range(N_HEADS):
        c_ref[h] = c_run[h]
        n_ref[h] = n_run[h]
        m_ref[h] = jnp.broadcast_to(m_run[h], (1, LANES))

    q_c = each(lambda i: _dot(qbs[i], c_sts[i]))
    q_n = each(lambda i: jnp.sum(qs[i] * n_sts[i], axis=-1, keepdims=True))
    inter = each(lambda i: bcs[i] + m_sts[i])
    m_t = each(lambda i: jnp.maximum(inter[i], m_intras[i]))
    s_inter = each(lambda i: jnp.exp(inter[i] - m_t[i]))
    s_intra = each(lambda i: jnp.exp(m_intras[i] - m_t[i]))
    den = each(lambda i: jnp.maximum(jnp.abs(s_inter[i] * q_n[i] + s_intra[i] * dens[i]),
                                     jnp.exp(-m_t[i])))
    hh = each(lambda i: (s_inter[i] * q_c[i] + s_intra[i] * num_intra[i]) / den[i])
    ms = each(lambda i: jnp.mean(hh[i] * hh[i], axis=-1, keepdims=True))
    scale = each(lambda i: lax.rsqrt(ms[i] + 1e-6))
    for i in range(n):
        gate = ng_ref[0:1, vcols(i)] * _sigmoid(op_ref[0, rows(i), vcols(i)])
        o_ref[0, rows(i), vcols(i)] = (hh[i] * scale[i] * gate).astype(BF16)


def _mlstm(q, k, v, op, sc, sct, ng, ts):
    B, S, _ = q.shape
    tok = lambda w: pl.BlockSpec((1, ts, w), lambda b, s: (b, s, 0))
    return pl.pallas_call(
        _mlstm_kernel,
        grid=(B, S // ts),
        in_specs=[tok(C_QK), tok(C_QK), tok(C_V), tok(C_V), tok(LANES),
                  pl.BlockSpec((1, 8, ts), lambda b, s: (b, 0, s)),
                  pl.BlockSpec(ng.shape, lambda b, s: (0, 0))],
        out_specs=tok(C_V),
        out_shape=jax.ShapeDtypeStruct((B, S, C_V), BF16),
        scratch_shapes=[pltpu.VMEM((N_HEADS, C_QK_HEAD, C_V_HEAD), F32),
                        pltpu.VMEM((N_HEADS, 1, C_QK_HEAD), F32),
                        pltpu.VMEM((N_HEADS, 1, LANES), F32)],
        compiler_params=_params("arbitrary", "arbitrary"),
        name="mlstm_scan",
    )(q, k, v, op, sc, sct, ng)


def _pad_lanes(a, offset=0):
    return jnp.pad(a, ((0, 0), (offset, LANES - offset - a.shape[1])))


def kernel(x, p, ab_w_in, ab_conv_qkv, ab_a_log, ab_dt_bias, ab_o_norm_g, ab_dw_w, ab_dw_b,
           ab_cn_g, ab_cn_b, ab_w_out, ffn_w_gate, ffn_w_up, ffn_w_down, c_w_in, c_b_i, c_b_f,
           c_norm_g, c_w_out, moe_w_router, moe_b_router, moe_w_gate, moe_w_up, moe_w_down,
           ln_mix_g, ln_mix_b, ln_ffn_g, ln_ffn_b, ple_w_proj, ple_w_gate):
    B, S, D = x.shape
    T = B * S
    ts_front, ts_scan, tm_big, tf = 512, 512, 1024, 512
    tm_moe = 7 * LANES

    w_in = ab_w_in[0]
    n_main = 4 * A_DIM
    wm = jnp.concatenate([w_in[:, :n_main], w_in[:, n_main + 2 * N_HEADS:]], axis=1).astype(BF16)
    ws = _pad_lanes(w_in[:, n_main:n_main + 2 * N_HEADS]).astype(BF16)
    gp = jnp.concatenate([_pad_lanes(ab_a_log[0][None, :], N_HEADS),
                          _pad_lanes(ab_dt_bias[0][None, :], N_HEADS)], axis=0)
    dwp = jnp.stack([ab_dw_b[0], ab_cn_g[0], ab_cn_b[0]], axis=0)
    q, k, v, z, sc, sct, u = _ab_front(x, wm, ws, ab_conv_qkv[0], gp, ab_dw_w[0], dwp, ts_front)
    o_a = _gdn(q, k, v, z, sc, sct, ab_o_norm_g[0][None, :], ts_scan)
    w_out = ab_w_out[0].astype(BF16)
    lnm = jnp.stack([ln_mix_g[0], ln_mix_b[0]], axis=0)
    lnp = jnp.stack([ln_ffn_g[0], ln_ffn_b[0]], axis=0)
    x2 = _ffn(o_a.reshape(T, A_DIM), u.reshape(T, B_CH), w_out[:A_DIM], w_out[A_DIM:],
              x.reshape(T, D), lnm,
              ffn_w_gate[0].astype(BF16), ffn_w_up[0].astype(BF16),
              ffn_w_down[0].astype(BF16), lnp, p[0].reshape(T, PLE_DIM),
              ple_w_gate[0].astype(BF16), ple_w_proj[0].astype(BF16), tm_big, tf)

    w_in = c_w_in[0]
    n_main = 2 * C_QK + 2 * C_V
    wm = w_in[:, :n_main].astype(BF16)
    ws = _pad_lanes(w_in[:, n_main:]).astype(BF16)
    gb = _pad_lanes(jnp.concatenate([c_b_i[0], c_b_f[0]])[None, :])
    q, k, v, op, sc, sct = _c_front(x2.reshape(B, S, D), wm, ws, gb, ts_front)
    hh = _mlstm(q, k, v, op, sc, sct, c_norm_g[0][None, :], ts_scan)
    lnp = jnp.stack([ln_mix_g[1], ln_mix_b[1]], axis=0)
    x3, x3_tok = _proj_ln([hh.reshape(T, C_V)], [c_w_out[0].astype(BF16)], x2, lnp, tm_big,
                          token_major_copy=True)

    wr = _pad_lanes(moe_w_router[0])
    wr_hi = wr.astype(BF16)
    wr_lo = (wr - wr_hi.astype(F32)).astype(BF16)
    info, info_t, counts = _router(x3, wr_hi, wr_lo, _pad_lanes(moe_b_router[0][None, :]),
                                   tm_big)
    src, dst, tile_expert, n_valid, n_real = _route_plan(info_t, counts, tm_moe)
    y = _moe_experts(tile_expert, n_valid, n_real, src, dst, x3_tok, moe_w_gate[0].astype(BF16),
                     moe_w_up[0].astype(BF16), moe_w_down[0].astype(BF16), 2 * T, tm_moe, tf)
    lnp = jnp.stack([ln_ffn_g[1], ln_ffn_b[1]], axis=0)
    x4 = _moe_combine(x3, info, y, lnp, p[1].reshape(T, PLE_DIM),
                      ple_w_gate[1].astype(BF16), ple_w_proj[1].astype(BF16), tm_big)
    return x4.reshape(B, S, D)
```

```python
import functools

import jax
import jax.numpy as jnp
from jax import lax
from jax.experimental import pallas as pl
from jax.experimental.pallas import tpu as pltpu

F32 = jnp.float32
BF16 = jnp.bfloat16

D_MODEL = 1024
N_HEADS = 4
CHUNK = 64
A_HEAD = 128
A_DIM = N_HEADS * A_HEAD
SHORT_CONV = 4
B_CH = 512
B_CONV = 31
C_QK_HEAD = 128
C_V_HEAD = 256
C_QK = N_HEADS * C_QK_HEAD
C_V = N_HEADS * C_V_HEAD
GATE_CAP = 15.0
D_FF = 3584
N_EXPERTS = 8
PLE_DIM = 256
DEPTH = 2
DN_ALPHA = (2 * DEPTH) ** 0.25
LANES = 128
SUBLANES = 8
QKV_HALO = 8
DW_HALO = 32
VMEM_LIMIT = 56 * 1024 * 1024


def _dot(a, b):
    return jnp.dot(a, b, preferred_element_type=F32)


def _dot_nt(a, b):
    return lax.dot_general(a, b, (((1,), (1,)), ((), ())), preferred_element_type=F32)


def _dot_tn(a, b):
    return lax.dot_general(a, b, (((0,), (0,)), ((), ())), preferred_element_type=F32)


def _sigmoid(t):
    return 1.0 / (1.0 + jnp.exp(-t))


def _silu(t):
    return t * _sigmoid(t)


def _softplus(t):
    return jnp.maximum(t, 0.0) + jnp.log1p(jnp.exp(-jnp.abs(t)))


def _layer_norm(t, g, b):
    mu = jnp.mean(t, axis=-1, keepdims=True)
    c = t - mu
    var = jnp.mean(c * c, axis=-1, keepdims=True)
    return c * lax.rsqrt(var + 1e-5) * g + b


def _chunk_cumsum(val):
    row = lax.broadcasted_iota(jnp.int32, val.shape, 0) & (CHUNK - 1)
    sh = 1
    while sh < CHUNK:
        val = val + jnp.where(row >= sh, pltpu.roll(val, sh, axis=0), 0.0)
        sh *= 2
    return val


def _params(*sem):
    return pltpu.CompilerParams(dimension_semantics=sem, vmem_limit_bytes=VMEM_LIMIT)


def _ab_front_kernel(x_ref, wm_ref, ws_ref, cw_ref, gp_ref, dww_ref, dwp_ref,
                     q_ref, k_ref, v_ref, z_ref, sc_ref, sct_ref, u_ref,
                     qkv_ext, u_ext, u_shift):
    s = pl.program_id(1)
    ts = x_ref.shape[1]
    xb = x_ref[0].astype(BF16)

    @pl.when(s == 0)
    def _():
        qkv_ext[0:QKV_HALO, :] = jnp.zeros((QKV_HALO, 3 * A_DIM), F32)
        u_ext[0:DW_HALO, :] = jnp.zeros((DW_HALO, B_CH), F32)

    @pl.when(s > 0)
    def _():
        qkv_ext[0:QKV_HALO, :] = qkv_ext[ts:ts + QKV_HALO, :]
        u_ext[0:DW_HALO, :] = u_ext[ts:ts + DW_HALO, :]

    qkv_ext[QKV_HALO:QKV_HALO + ts, :] = _dot(xb, wm_ref[:, 0:3 * A_DIM])
    acc = None
    for j in range(SHORT_CONV):
        off = QKV_HALO - (SHORT_CONV - 1) + j
        term = cw_ref[j:j + 1, :] * qkv_ext[off:off + ts, :]
        acc = term if acc is None else acc + term
    qkv = _silu(acc)
    for h in range(N_HEADS):
        lo, hi = h * A_HEAD, (h + 1) * A_HEAD
        qh = qkv[:, lo:hi]
        kh = qkv[:, A_DIM + lo:A_DIM + hi]
        qn = lax.rsqrt(jnp.sum(qh * qh, axis=-1, keepdims=True) + 1e-6)
        kn = lax.rsqrt(jnp.sum(kh * kh, axis=-1, keepdims=True) + 1e-6)
        q_ref[0, :, lo:hi] = qh * qn * (A_HEAD ** -0.5)
        k_ref[0, :, lo:hi] = kh * kn
    v_ref[0] = qkv[:, 2 * A_DIM:3 * A_DIM]
    z_ref[0] = _dot(xb, wm_ref[:, 3 * A_DIM:4 * A_DIM])

    hs = _dot(xb, ws_ref[...])
    lane = lax.broadcasted_iota(jnp.int32, hs.shape, 1)
    beta = _sigmoid(hs)
    g = -jnp.exp(gp_ref[0:1, :]) * _softplus(hs + gp_ref[1:2, :])
    sc = jnp.where(lane < N_HEADS, beta, _chunk_cumsum(g))
    sc_ref[0] = sc
    sct_ref[0] = jnp.transpose(sc)[0:8, :]

    glu = _dot(xb, wm_ref[:, 4 * A_DIM:4 * A_DIM + 2 * B_CH])
    u_ext[DW_HALO:DW_HALO + ts, :] = glu[:, 0:B_CH] * _sigmoid(glu[:, B_CH:2 * B_CH])
    first_off = DW_HALO - (B_CONV - 1)
    span = ts + DW_HALO - 8
    acc = None
    for res in range(8):
        offs = [o for o in range(first_off, DW_HALO + 1) if o % 8 == res]
        if res:
            u_shift[...] = u_ext[res:res + span, :]
        src = u_shift if res else u_ext
        for off in offs:
            lo = off - res if res else off
            term = dww_ref[off - first_off:off - first_off + 1, :] * src[lo:lo + ts, :]
            acc = term if acc is None else acc + term
    u = acc + dwp_ref[0:1, :]
    u = _layer_norm(u, dwp_ref[1:2, :], dwp_ref[2:3, :])
    u_ref[0] = _silu(u).astype(BF16)


def _ab_front(x, wm, ws, cw, gp, dww, dwp, ts):
    B, S, D = x.shape
    grid = (B, S // ts)
    tok = lambda w: pl.BlockSpec((1, ts, w), lambda b, s: (b, s, 0))
    full = lambda a: pl.BlockSpec(a.shape, lambda b, s: (0,) * a.ndim)
    out_shape = (
        jax.ShapeDtypeStruct((B, S, A_DIM), F32),
        jax.ShapeDtypeStruct((B, S, A_DIM), F32),
        jax.ShapeDtypeStruct((B, S, A_DIM), F32),
        jax.ShapeDtypeStruct((B, S, A_DIM), F32),
        jax.ShapeDtypeStruct((B, S, LANES), F32),
        jax.ShapeDtypeStruct((B, 8, S), F32),
        jax.ShapeDtypeStruct((B, S, B_CH), BF16),
    )
    out_specs = (tok(A_DIM), tok(A_DIM), tok(A_DIM), tok(A_DIM), tok(LANES),
                 pl.BlockSpec((1, 8, ts), lambda b, s: (b, 0, s)), tok(B_CH))
    return pl.pallas_call(
        _ab_front_kernel,
        grid=grid,
        in_specs=[tok(D), full(wm), full(ws), full(cw), full(gp), full(dww), full(dwp)],
        out_specs=out_specs,
        out_shape=out_shape,
        scratch_shapes=[pltpu.VMEM((ts + QKV_HALO, 3 * A_DIM), F32),
                        pltpu.VMEM((ts + DW_HALO, B_CH), F32),
                        pltpu.VMEM((ts + DW_HALO - 8, B_CH), F32)],
        compiler_params=_params("arbitrary", "arbitrary"),
        name="ab_front",
    )(x, wm, ws, cw, gp, dww, dwp)


def _gdn_kernel(q_ref, k_ref, v_ref, z_ref, sc_ref, sct_ref, ng_ref, o_ref, st_ref):
    s = pl.program_id(1)
    ts = q_ref.shape[1]

    @pl.when(s == 0)
    def _():
        st_ref[...] = jnp.zeros(st_ref.shape, F32)

    row = lax.broadcasted_iota(jnp.int32, (CHUNK, CHUNK), 0)
    col = lax.broadcasted_iota(jnp.int32, (CHUNK, CHUNK), 1)
    causal = row >= col
    strict = row > col
    eye = jnp.where(row == col, 1.0, 0.0).astype(F32)

    n_chunks = ts // CHUNK
    pairs = [(c, h) for c in range(n_chunks) for h in range(N_HEADS)]
    pw, tinv, attn, rhs, k_dec, q_dec, d_last = [], [], [], [], [], [], []
    for c, h in pairs:
        r0, r1 = c * CHUNK, (c + 1) * CHUNK
        lo, hi = h * A_HEAD, (h + 1) * A_HEAD
        qh = q_ref[0, r0:r1, lo:hi]
        kh = k_ref[0, r0:r1, lo:hi]
        vh = v_ref[0, r0:r1, lo:hi]
        beta = sc_ref[0, r0:r1, h:h + 1]
        gc = sc_ref[0, r0:r1, N_HEADS + h:N_HEADS + h + 1]
        gc_row = sct_ref[0, N_HEADS + h:N_HEADS + h + 1, r0:r1]
        gc_last = gc[CHUNK - 1:CHUNK, :]
        decay = jnp.where(causal, jnp.exp(jnp.where(causal, gc - gc_row, 0.0)), 0.0)
        kb = kh * beta
        egc = jnp.exp(gc)
        kq = _dot_nt(jnp.concatenate([kb, qh], axis=0).astype(BF16), kh.astype(BF16))
        a_neg = jnp.where(strict, -kq[0:CHUNK] * decay, 0.0)
        pw.append(a_neg)
        tinv.append(eye + a_neg)
        attn.append((kq[CHUNK:2 * CHUNK] * decay).astype(BF16))
        rhs.append(jnp.concatenate([vh * beta, kb * egc], axis=1).astype(BF16))
        k_dec.append((kh * jnp.exp(gc_last - gc)).astype(BF16))
        q_dec.append(qh * egc)
        d_last.append(jnp.exp(gc_last))
    head_blk = lax.broadcasted_iota(jnp.int32, (CHUNK, N_HEADS * CHUNK), 1) // CHUNK

    def block_diag(m):
        return jnp.concatenate([jnp.where(head_blk == h, m, 0.0) for h in range(N_HEADS)],
                               axis=0).astype(BF16)

    side_by_side = lambda mats, c: jnp.concatenate(mats[c * N_HEADS:(c + 1) * N_HEADS], axis=1)
    qc = [side_by_side(pw, c) for c in range(n_chunks)]
    yc = [side_by_side(tinv, c) for c in range(n_chunks)]
    qc = [_dot(qc[c].astype(BF16), block_diag(qc[c])) for c in range(n_chunks)]
    for _ in range(4):
        both = [_dot(jnp.concatenate([yc[c], qc[c]], axis=0).astype(BF16), block_diag(qc[c]))
                for c in range(n_chunks)]
        yc = [yc[c] + both[c][0:CHUNK] for c in range(n_chunks)]
        qc = [both[c][CHUNK:2 * CHUNK] for c in range(n_chunks)]
    last = [_dot(yc[c].astype(BF16), block_diag(qc[c])) for c in range(n_chunks)]
    yc = [yc[c] + last[c] for c in range(n_chunks)]
    tinv = [yc[c][:, h * CHUNK:(h + 1) * CHUNK] for c, h in pairs]
    uw = [_dot(tinv[i].astype(BF16), rhs[i]).astype(BF16) for i in range(len(pairs))]
    auw = [_dot(attn[i], uw[i]) for i in range(len(pairs))]
    kuw = [_dot_tn(k_dec[i], uw[i]) for i in range(len(pairs))]
    st = [st_ref[h] for h in range(N_HEADS)]
    for i, (c, h) in enumerate(pairs):
        r0, r1 = c * CHUNK, (c + 1) * CHUNK
        lo, hi = h * A_HEAD, (h + 1) * A_HEAD
        stb = st[h].astype(BF16)
        q_eff = (q_dec[i] - auw[i][:, A_HEAD:]).astype(BF16)
        o = _dot(q_eff, stb) + auw[i][:, 0:A_HEAD]
        st[h] = (d_last[i] * st[h] + kuw[i][:, 0:A_HEAD]
                 - _dot(kuw[i][:, A_HEAD:].astype(BF16), stb))
        o = o * lax.rsqrt(jnp.mean(o * o, axis=-1, keepdims=True) + 1e-6) * ng_ref[0:1, :]
        o_ref[0, r0:r1, lo:hi] = (o * _silu(z_ref[0, r0:r1, lo:hi])).astype(BF16)
    for h in range(N_HEADS):
        st_ref[h] = st[h]


def _gdn(q, k, v, z, sc, sct, ng, ts):
    B, S, _ = q.shape
    tok = lambda w: pl.BlockSpec((1, ts, w), lambda b, s: (b, s, 0))
    return pl.pallas_call(
        _gdn_kernel,
        grid=(B, S // ts),
        in_specs=[tok(A_DIM), tok(A_DIM), tok(A_DIM), tok(A_DIM), tok(LANES),
                  pl.BlockSpec((1, 8, ts), lambda b, s: (b, 0, s)),
                  pl.BlockSpec(ng.shape, lambda b, s: (0, 0))],
        out_specs=tok(A_DIM),
        out_shape=jax.ShapeDtypeStruct((B, S, A_DIM), BF16),
        scratch_shapes=[pltpu.VMEM((N_HEADS, A_HEAD, A_HEAD), F32)],
        compiler_params=_params("arbitrary", "arbitrary"),
        name="gdn_scan",
    )(q, k, v, z, sc, sct, ng)


def _token_rows(s, n_tokens):
    return pl.ds(s, n_tokens, stride=D_MODEL // LANES)


def _proj_ln_kernel(*refs, n_in, token_major_copy):
    a_refs = refs[0:n_in]
    w_refs = refs[n_in:2 * n_in]
    x_ref, lnp_ref, o_ref = refs[2 * n_in:2 * n_in + 3]
    mix = None
    for a_ref, w_ref in zip(a_refs, w_refs):
        t = _dot(a_ref[...], w_ref[...])
        mix = t if mix is None else mix + t
    y = _layer_norm(DN_ALPHA * x_ref[...] + mix, lnp_ref[0:1, :], lnp_ref[1:2, :])
    o_ref[...] = y
    if token_major_copy:
        ot_ref = refs[2 * n_in + 3]
        for s in range(D_MODEL // LANES):
            ot_ref[_token_rows(s, y.shape[0]), :] = y[:, s * LANES:(s + 1) * LANES]


def _proj_ln(acts, weights, x, lnp, tm, token_major_copy=False):
    T, D = x.shape
    n_in = len(acts)
    in_specs = ([pl.BlockSpec((tm, a.shape[1]), lambda i: (i, 0)) for a in acts]
                + [pl.BlockSpec(w.shape, lambda i: (0, 0)) for w in weights]
                + [pl.BlockSpec((tm, D), lambda i: (i, 0)),
                   pl.BlockSpec(lnp.shape, lambda i: (0, 0))])
    out_specs = [pl.BlockSpec((tm, D), lambda i: (i, 0))]
    out_shape = [jax.ShapeDtypeStruct((T, D), F32)]
    if token_major_copy:
        out_specs.append(pl.BlockSpec((tm * D // LANES, LANES), lambda i: (i, 0)))
        out_shape.append(jax.ShapeDtypeStruct((T * D // LANES, LANES), F32))
    outs = pl.pallas_call(
        functools.partial(_proj_ln_kernel, n_in=n_in, token_major_copy=token_major_copy),
        grid=(T // tm,),
        in_specs=in_specs,
        out_specs=tuple(out_specs),
        out_shape=tuple(out_shape),
        compiler_params=_params("arbitrary"),
        name="proj_ln",
    )(*acts, *weights, x, lnp)
    return outs if token_major_copy else outs[0]


def _ffn_epilogue(x, acc, lnp_ref, p_ref, wpg_ref, wpp_ref, o_ref):
    y = _layer_norm(DN_ALPHA * x + acc, lnp_ref[0:1, :], lnp_ref[1:2, :])
    gate = _sigmoid(_dot(y.astype(BF16), wpg_ref[...]))
    o_ref[...] = y + gate * _dot(p_ref[...].astype(BF16), wpp_ref[...])


def _ffn_kernel(a1_ref, a2_ref, w1_ref, w2_ref, x_ref, lnm_ref,
                wg_ref, wu_ref, wd_ref, lnp_ref, p_ref, wpg_ref, wpp_ref,
                o_ref, xb_ref, x1_ref):
    j = pl.program_id(1)

    @pl.when(j == 0)
    def _():
        mix = _dot(a1_ref[...], w1_ref[...]) + _dot(a2_ref[...], w2_ref[...])
        x1 = _layer_norm(DN_ALPHA * x_ref[...] + mix, lnm_ref[0:1, :], lnm_ref[1:2, :])
        x1_ref[...] = x1
        xb_ref[...] = x1.astype(BF16)
        o_ref[...] = jnp.zeros(o_ref.shape, F32)

    xb = xb_ref[...]
    hid = _silu(_dot(xb, wg_ref[...])) * _dot(xb, wu_ref[...])
    o_ref[...] += _dot(hid.astype(BF16), wd_ref[...])

    @pl.when(j == pl.num_programs(1) - 1)
    def _():
        _ffn_epilogue(x1_ref[...], o_ref[...], lnp_ref, p_ref, wpg_ref, wpp_ref, o_ref)


def _ffn(a1, a2, w1, w2, x, lnm, wg, wu, wd, lnp, p, wpg, wpp, tm, tf):
    T, D = x.shape
    F = wg.shape[1]
    return pl.pallas_call(
        _ffn_kernel,
        grid=(T // tm, F // tf),
        in_specs=[pl.BlockSpec((tm, a1.shape[1]), lambda i, j: (i, 0)),
                  pl.BlockSpec((tm, a2.shape[1]), lambda i, j: (i, 0)),
                  pl.BlockSpec(w1.shape, lambda i, j: (0, 0)),
                  pl.BlockSpec(w2.shape, lambda i, j: (0, 0)),
                  pl.BlockSpec((tm, D), lambda i, j: (i, 0)),
                  pl.BlockSpec(lnm.shape, lambda i, j: (0, 0)),
                  pl.BlockSpec((D, tf), lambda i, j: (0, j)),
                  pl.BlockSpec((D, tf), lambda i, j: (0, j)),
                  pl.BlockSpec((tf, D), lambda i, j: (j, 0)),
                  pl.BlockSpec(lnp.shape, lambda i, j: (0, 0)),
                  pl.BlockSpec((tm, PLE_DIM), lambda i, j: (i, 0)),
                  pl.BlockSpec(wpg.shape, lambda i, j: (0, 0)),
                  pl.BlockSpec(wpp.shape, lambda i, j: (0, 0))],
        out_specs=pl.BlockSpec((tm, D), lambda i, j: (i, 0)),
        out_shape=jax.ShapeDtypeStruct((T, D), F32),
        scratch_shapes=[pltpu.VMEM((tm, D), BF16), pltpu.VMEM((tm, D), F32)],
        compiler_params=_params("arbitrary", "arbitrary"),
        name="ffn",
    )(a1, a2, w1, w2, x, lnm, wg, wu, wd, lnp, p, wpg, wpp)


ROUTE_IDX1, ROUTE_IDX2, ROUTE_W1, ROUTE_W2, ROUTE_RANK1, ROUTE_RANK2 = range(6)


def _router_kernel(x_ref, wh_ref, wl_ref, br_ref, info_ref, info_t_ref, cnt_ref, carry_ref):
    i = pl.program_id(0)

    @pl.when(i == 0)
    def _():
        carry_ref[...] = jnp.zeros(carry_ref.shape, F32)

    x = x_ref[...]
    tm = x.shape[0]
    xh = x.astype(BF16)
    xl = (x - xh.astype(F32)).astype(BF16)
    logits = (_dot(xh, wh_ref[...]) + _dot(xl, wh_ref[...]) + _dot(xh, wl_ref[...])
              + br_ref[0:1, :])
    lane = lax.broadcasted_iota(jnp.int32, logits.shape, 1)
    neg = jnp.float32(-jnp.inf)
    logits = jnp.where(lane < N_EXPERTS, logits, neg)
    m1 = jnp.max(logits, axis=-1, keepdims=True)
    i1 = jnp.min(jnp.where(logits == m1, lane, LANES), axis=-1, keepdims=True)
    sel1 = lane == i1
    rest = jnp.where(sel1, neg, logits)
    m2 = jnp.max(rest, axis=-1, keepdims=True)
    i2 = jnp.min(jnp.where(rest == m2, lane, LANES), axis=-1, keepdims=True)
    sel2 = lane == i2
    e2 = jnp.exp(m2 - m1)
    den = 1.0 + e2

    sel = jnp.where(jnp.logical_or(sel1, sel2), 1.0, 0.0)
    row = lax.broadcasted_iota(jnp.int32, (tm, tm), 0)
    col = lax.broadcasted_iota(jnp.int32, (tm, tm), 1)
    earlier = jnp.where(row > col, 1.0, 0.0).astype(BF16)
    rank = _dot(earlier, sel.astype(BF16)) + carry_ref[0:1, :]
    rank1 = jnp.sum(jnp.where(sel1, rank, 0.0), axis=-1, keepdims=True)
    rank2 = jnp.sum(jnp.where(sel2, rank, 0.0), axis=-1, keepdims=True)

    info = jnp.zeros(logits.shape, F32)
    for slot, val in ((ROUTE_IDX1, i1.astype(F32)), (ROUTE_IDX2, i2.astype(F32)),
                      (ROUTE_W1, 1.0 / den), (ROUTE_W2, e2 / den),
                      (ROUTE_RANK1, rank1), (ROUTE_RANK2, rank2)):
        info = jnp.where(lane == slot, val, info)
    info_ref[...] = info
    info_t_ref[...] = jnp.transpose(info)[0:8, :]
    carry_ref[0:1, :] = carry_ref[0:1, :] + jnp.sum(sel, axis=0, keepdims=True)
    cnt_ref[...] = jnp.broadcast_to(carry_ref[0:1, :], cnt_ref.shape)


def _router(x, wh, wl, br, tm):
    T, D = x.shape
    return pl.pallas_call(
        _router_kernel,
        grid=(T // tm,),
        in_specs=[pl.BlockSpec((tm, D), lambda i: (i, 0)),
                  pl.BlockSpec(wh.shape, lambda i: (0, 0)),
                  pl.BlockSpec(wl.shape, lambda i: (0, 0)),
                  pl.BlockSpec(br.shape, lambda i: (0, 0))],
        out_specs=(pl.BlockSpec((tm, LANES), lambda i: (i, 0)),
                   pl.BlockSpec((8, tm), lambda i: (0, i)),
                   pl.BlockSpec((8, LANES), lambda i: (0, 0))),
        out_shape=(jax.ShapeDtypeStruct((T, LANES), F32),
                   jax.ShapeDtypeStruct((8, T), F32),
                   jax.ShapeDtypeStruct((8, LANES), F32)),
        scratch_shapes=[pltpu.VMEM((8, LANES), F32)],
        compiler_params=_params("arbitrary"),
        name="router",
    )(x, wh, wl, br)


def _moe_expert_kernel(te_ref, nv_ref, nr_ref, src_ref, dst_ref, x_hbm, wg_ref, wu_ref, wd_ref, y_hbm,
                       xg_ref, xb_ref, acc_ref, yo_ref, gidx_ref, sidx_ref,
                       sem_idx, sem_g, sem_s, *, n_j):
    s = pl.program_id(0)
    j = pl.program_id(1)
    tm = xb_ref.shape[0]
    rows_per_step = tm // n_j
    n_blk = D_MODEL // LANES
    nv = nv_ref[0]
    gather_on = s < nv
    compute_on = jnp.logical_and(s >= 1, s - 1 < nv)
    scatter_on = jnp.logical_and(s >= 2, s - 2 < nv)
    slot_cur = lax.rem(s, 2)
    slot_prev = 1 - slot_cur

    def token_tile(first_row):
        return pl.ds(pl.multiple_of(first_row, n_blk), n_blk)

    def gather_copy(slot, r, src_row):
        return pltpu.make_async_copy(x_hbm.at[token_tile(src_row), :],
                                     xg_ref.at[slot, token_tile(r * n_blk), :], sem_g)

    def scatter_copy(slot, r, dst_row):
        return pltpu.make_async_copy(yo_ref.at[slot, token_tile(r * n_blk), :],
                                     y_hbm.at[token_tile(dst_row), :], sem_s)

    def real_rows(tile):
        return nr_ref[jnp.clip(tile, 0, nr_ref.shape[0] - 1)]

    def drain(copy_of_row, n_rows=None):
        def body(g, carry):
            r0 = pl.multiple_of(g * SUBLANES, SUBLANES)
            for u in range(SUBLANES):
                copy_of_row(r0 + u).wait()
            return carry

        def one(r, carry):
            copy_of_row(r).wait()
            return carry

        def all_rows():
            lax.fori_loop(0, tm // SUBLANES, body, 0)

        def some_rows():
            lax.fori_loop(0, n_rows, one, 0)

        if n_rows is None:
            all_rows()
        else:
            pl.when(n_rows == tm)(all_rows)
            pl.when(n_rows < tm)(some_rows)

    stage_g = pltpu.make_async_copy(src_ref.at[0], gidx_ref, sem_idx.at[0])
    stage_s = pltpu.make_async_copy(dst_ref.at[0], sidx_ref, sem_idx.at[1])

    @pl.when(j == 0)
    def _():
        pl.when(gather_on)(stage_g.start)
        pl.when(scatter_on)(stage_s.start)

        @pl.when(compute_on)
        def _():
            drain(lambda r: gather_copy(slot_prev, r, 0))
            for blk in range(n_blk):
                xb_ref[:, blk * LANES:(blk + 1) * LANES] = (
                    xg_ref[slot_prev, _token_rows(blk, tm), :].astype(BF16))
            acc_ref[...] = jnp.zeros(acc_ref.shape, F32)

        @pl.when(jnp.logical_and(s >= 3, s - 3 < nv))
        def _():
            drain(lambda r: scatter_copy(slot_prev, r, 0), real_rows(s - 3))

        pl.when(gather_on)(stage_g.wait)
        pl.when(scatter_on)(stage_s.wait)

    base = pl.multiple_of(j * rows_per_step, rows_per_step)

    def issue_gathers():
        for u in range(rows_per_step):
            gather_copy(slot_cur, base + u, gidx_ref[0, base + u]).start(priority=1)

    def issue_scatters():
        n_real = real_rows(s - 2)

        @pl.when(base + rows_per_step <= n_real)
        def _():
            for u in range(rows_per_step):
                scatter_copy(slot_cur, base + u, sidx_ref[0, base + u]).start()

        @pl.when(base + rows_per_step > n_real)
        def _():
            def one(r, carry):
                scatter_copy(slot_cur, r, sidx_ref[0, r]).start()
                return carry
            lax.fori_loop(base, jnp.maximum(base, jnp.minimum(base + rows_per_step, n_real)),
                          one, 0)

    def compute():
        xb = xb_ref[...]
        hid = _silu(_dot(xb, wg_ref[0])) * _dot(xb, wu_ref[0])
        acc_ref[...] += _dot(hid.astype(BF16), wd_ref[0])

    pl.when(gather_on)(issue_gathers)
    pl.when(compute_on)(compute)
    pl.when(scatter_on)(issue_scatters)

    last_j = j == n_j - 1

    @pl.when(jnp.logical_and(last_j, compute_on))
    def _():
        for blk in range(n_blk):
            yo_ref[slot_prev, _token_rows(blk, tm), :] = acc_ref[:, blk * LANES:(blk + 1) * LANES]

    @pl.when(jnp.logical_and(jnp.logical_and(last_j, s == pl.num_programs(0) - 1), scatter_on))
    def _():
        drain(lambda r: scatter_copy(slot_cur, r, 0), real_rows(s - 2))


def _moe_experts(tile_expert, n_valid, n_real, src_rows, dst_rows, x, wg, wu, wd, n_out_rows,
                 tm, tf):
    E, D, F = wg.shape
    n_tiles = src_rows.shape[0]
    n_j = F // tf
    assert tm % (n_j * SUBLANES) == 0

    def computing(s, nv):
        return jnp.logical_and(s >= 1, s - 1 < nv[0])

    def tile_of(s):
        return jnp.clip(s - 1, 0, n_tiles - 1)

    def w_map(s, j, te, nv, nr):
        return (te[tile_of(s)], 0, jnp.where(computing(s, nv), j, 0))

    def wd_map(s, j, te, nv, nr):
        return (te[tile_of(s)], jnp.where(computing(s, nv), j, 0), 0)

    grid_spec = pltpu.PrefetchScalarGridSpec(
        num_scalar_prefetch=3,
        grid=(n_tiles + 2, n_j),
        in_specs=[pl.BlockSpec((1, 1, tm),
                               lambda s, j, te, nv, nr: (jnp.minimum(s, n_tiles - 1), 0, 0)),
                  pl.BlockSpec((1, 1, tm),
                               lambda s, j, te, nv, nr: (jnp.clip(s - 2, 0, n_tiles - 1), 0, 0)),
                  pl.BlockSpec(memory_space=pl.ANY),
                  pl.BlockSpec((1, D, tf), w_map),
                  pl.BlockSpec((1, D, tf), w_map),
                  pl.BlockSpec((1, tf, D), wd_map)],
        out_specs=pl.BlockSpec(memory_space=pl.ANY),
        scratch_shapes=[pltpu.VMEM((2, tm * D // LANES, LANES), F32),
                        pltpu.VMEM((tm, D), BF16),
                        pltpu.VMEM((tm, D), F32),
                        pltpu.VMEM((2, tm * D // LANES, LANES), F32),
                        pltpu.SMEM((1, tm), jnp.int32), pltpu.SMEM((1, tm), jnp.int32),
                        pltpu.SemaphoreType.DMA((2,)), pltpu.SemaphoreType.DMA(()),
                        pltpu.SemaphoreType.DMA(())],
    )
    return pl.pallas_call(
        functools.partial(_moe_expert_kernel, n_j=n_j),
        grid_spec=grid_spec,
        out_shape=jax.ShapeDtypeStruct((n_out_rows * D // LANES, LANES), F32),
        compiler_params=_params("arbitrary", "arbitrary"),
        name="moe_experts",
    )(tile_expert, n_valid, n_real, src_rows, dst_rows, x, wg, wu, wd)


def _moe_combine_kernel(x_ref, info_ref, ya_ref, yb_ref, lnp_ref, p_ref, wpg_ref, wpp_ref, o_ref):
    info = info_ref[...]
    tm = x_ref.shape[0]
    w1 = info[:, ROUTE_W1:ROUTE_W1 + 1]
    w2 = info[:, ROUTE_W2:ROUTE_W2 + 1]
    mix = jnp.concatenate(
        [w1 * ya_ref[_token_rows(blk, tm), :] + w2 * yb_ref[_token_rows(blk, tm), :]
         for blk in range(D_MODEL // LANES)], axis=1)
    _ffn_epilogue(x_ref[...], mix, lnp_ref, p_ref, wpg_ref, wpp_ref, o_ref)


def _moe_combine(x, info, y, lnp, p, wpg, wpp, tm):
    T, D = x.shape
    y_block = (tm * D // LANES, LANES)
    return pl.pallas_call(
        _moe_combine_kernel,
        grid=(T // tm,),
        in_specs=[pl.BlockSpec((tm, D), lambda i: (i, 0)),
                  pl.BlockSpec((tm, LANES), lambda i: (i, 0)),
                  pl.BlockSpec(y_block, lambda i: (i, 0)),
                  pl.BlockSpec(y_block, lambda i: (T // tm + i, 0)),
                  pl.BlockSpec(lnp.shape, lambda i: (0, 0)),
                  pl.BlockSpec((tm, PLE_DIM), lambda i: (i, 0)),
                  pl.BlockSpec(wpg.shape, lambda i: (0, 0)),
                  pl.BlockSpec(wpp.shape, lambda i: (0, 0))],
        out_specs=pl.BlockSpec((tm, D), lambda i: (i, 0)),
        out_shape=jax.ShapeDtypeStruct((T, D), F32),
        compiler_params=_params("arbitrary"),
        name="moe_combine",
    )(x, info, y, y, lnp, p, wpg, wpp)


def _route_plan(info_t, counts, tm):
    T = info_t.shape[1]
    n_tiles = (2 * T + N_EXPERTS * (tm - 1)) // tm + 1
    n_rows = n_tiles * tm
    as_int = lambda col: info_t[col].astype(jnp.int32)
    idx1, idx2 = as_int(ROUTE_IDX1), as_int(ROUTE_IDX2)
    counts = counts[0, :N_EXPERTS].astype(jnp.int32)
    padded = ((counts + tm - 1) // tm) * tm
    ends = jnp.cumsum(padded)
    starts = ends - padded
    grouped_row = jnp.concatenate([starts[idx1] + as_int(ROUTE_RANK1),
                                   starts[idx2] + as_int(ROUTE_RANK2)])
    out_row = jnp.arange(2 * T, dtype=jnp.int32)
    _, compact = lax.sort((grouped_row, out_row), num_keys=1)
    compact = jnp.concatenate([compact, jnp.zeros((n_rows - 2 * T,), jnp.int32)])
    cstarts = jnp.cumsum(counts) - counts
    row_id = jnp.arange(n_rows, dtype=jnp.int32)
    owner = jnp.full((n_rows,), -1, jnp.int32)
    for e in range(N_EXPERTS):
        in_group = jnp.logical_and(row_id >= starts[e], row_id < starts[e] + counts[e])
        owner = jnp.where(in_group, jnp.roll(compact, starts[e] - cstarts[e]), owner)
    real = owner >= 0
    dst = jnp.maximum(owner, 0)
    src = jnp.where(real, jnp.where(owner >= T, owner - T, owner), 0)
    n_real = jnp.sum(real.reshape(n_tiles, tm), axis=1).astype(jnp.int32)
    tile_start = jnp.arange(n_tiles, dtype=jnp.int32) * tm
    tile_expert = jnp.minimum(jnp.sum(tile_start[:, None] >= ends[None, :], axis=1),
                              N_EXPERTS - 1).astype(jnp.int32)
    n_valid = (ends[-1] // tm).astype(jnp.int32).reshape(1)
    shape = (n_tiles, 1, tm)
    n_blk = D_MODEL // LANES
    return (src * n_blk).reshape(shape), (dst * n_blk).reshape(shape), tile_expert, n_valid, n_real


def _c_front_kernel(x_ref, wm_ref, ws_ref, gb_ref, q_ref, k_ref, v_ref, op_ref, sc_ref, sct_ref):
    xb = x_ref[0].astype(BF16)
    q_ref[0] = _dot(xb, wm_ref[:, 0:C_QK]) * (C_QK_HEAD ** -0.5)
    k_ref[0] = _dot(xb, wm_ref[:, C_QK:2 * C_QK])
    v_ref[0] = _dot(xb, wm_ref[:, 2 * C_QK:2 * C_QK + C_V])
    op_ref[0] = _dot(xb, wm_ref[:, 2 * C_QK + C_V:2 * C_QK + 2 * C_V])
    hs = _dot(xb, ws_ref[...]) + gb_ref[0:1, :]
    capped = GATE_CAP * jnp.tanh(hs / GATE_CAP)
    lane = lax.broadcasted_iota(jnp.int32, hs.shape, 1)
    log_f = jnp.minimum(capped, 0.0) - jnp.log1p(jnp.exp(-jnp.abs(capped)))
    sc = jnp.where(lane < N_HEADS, capped, _chunk_cumsum(log_f))
    sc_ref[0] = sc
    sct_ref[0] = jnp.transpose(sc)[0:8, :]


def _c_front(x, wm, ws, gb, ts):
    B, S, D = x.shape
    tok = lambda w: pl.BlockSpec((1, ts, w), lambda b, s: (b, s, 0))
    full = lambda a: pl.BlockSpec(a.shape, lambda b, s: (0,) * a.ndim)
    out_shape = (
        jax.ShapeDtypeStruct((B, S, C_QK), F32),
        jax.ShapeDtypeStruct((B, S, C_QK), F32),
        jax.ShapeDtypeStruct((B, S, C_V), F32),
        jax.ShapeDtypeStruct((B, S, C_V), F32),
        jax.ShapeDtypeStruct((B, S, LANES), F32),
        jax.ShapeDtypeStruct((B, 8, S), F32),
    )
    out_specs = (tok(C_QK), tok(C_QK), tok(C_V), tok(C_V), tok(LANES),
                 pl.BlockSpec((1, 8, ts), lambda b, s: (b, 0, s)))
    return pl.pallas_call(
        _c_front_kernel,
        grid=(B, S // ts),
        in_specs=[tok(D), full(wm), full(ws), full(gb)],
        out_specs=out_specs,
        out_shape=out_shape,
        compiler_params=_params("arbitrary", "arbitrary"),
        name="c_front",
    )(x, wm, ws, gb)


def _mlstm_kernel(q_ref, k_ref, v_ref, op_ref, sc_ref, sct_ref, ng_ref, o_ref,
                  c_ref, n_ref, m_ref):
    s = pl.program_id(1)
    ts = q_ref.shape[1]

    @pl.when(s == 0)
    def _():
        c_ref[...] = jnp.zeros(c_ref.shape, F32)
        n_ref[...] = jnp.zeros(n_ref.shape, F32)
        m_ref[...] = jnp.zeros(m_ref.shape, F32)

    row = lax.broadcasted_iota(jnp.int32, (CHUNK, CHUNK), 0)
    col = lax.broadcasted_iota(jnp.int32, (CHUNK, CHUNK), 1)
    causal = row >= col
    neg = jnp.float32(-jnp.inf)

    n_chunks = ts // CHUNK
    pairs = [(c, h) for c in range(n_chunks) for h in range(N_HEADS)]
    n = len(pairs)
    each = lambda f: [f(i) for i in range(n)]
    rows = lambda i: slice(pairs[i][0] * CHUNK, (pairs[i][0] + 1) * CHUNK)
    kcols = lambda i: slice(pairs[i][1] * C_QK_HEAD, (pairs[i][1] + 1) * C_QK_HEAD)
    vcols = lambda i: slice(pairs[i][1] * C_V_HEAD, (pairs[i][1] + 1) * C_V_HEAD)
    head = lambda i: pairs[i][1]

    qs = each(lambda i: q_ref[0, rows(i), kcols(i)])
    ks = each(lambda i: k_ref[0, rows(i), kcols(i)])
    qbs = each(lambda i: qs[i].astype(BF16))
    vbs = each(lambda i: v_ref[0, rows(i), vcols(i)].astype(BF16))
    qk = each(lambda i: _dot_nt(qbs[i], ks[i].astype(BF16)))
    lis = each(lambda i: sc_ref[0, rows(i), head(i):head(i) + 1])
    bcs = each(lambda i: sc_ref[0, rows(i), N_HEADS + head(i):N_HEADS + head(i) + 1])
    b_lasts = each(lambda i: bcs[i][CHUNK - 1:CHUNK, :])
    bc_wide = each(lambda i: jnp.broadcast_to(bcs[i], (CHUNK, CHUNK)))
    dmat = each(lambda i: jnp.where(
        causal,
        bc_wide[i] - sct_ref[0, N_HEADS + head(i):N_HEADS + head(i) + 1, rows(i)]
        + sct_ref[0, head(i):head(i) + 1, rows(i)], neg))
    m_intras = each(lambda i: jnp.max(dmat[i], axis=-1, keepdims=True))
    pmat = each(lambda i: jnp.exp(dmat[i] - m_intras[i]) * qk[i])
    dens = each(lambda i: jnp.sum(pmat[i], axis=-1, keepdims=True))
    num_intra = each(lambda i: _dot(pmat[i].astype(BF16), vbs[i]))
    g_kvs = each(lambda i: b_lasts[i] - bcs[i] + lis[i])
    m_kvs = each(lambda i: jnp.max(g_kvs[i], axis=0, keepdims=True))

    m_run = [m_ref[h, :, 0:1] for h in range(N_HEADS)]
    m_sts, m_news = [], []
    for i in range(n):
        m_sts.append(m_run[head(i)])
        m_run[head(i)] = jnp.maximum(b_lasts[i] + m_run[head(i)], m_kvs[i])
        m_news.append(m_run[head(i)])
    decs = each(lambda i: jnp.exp(b_lasts[i] + m_sts[i] - m_news[i]))
    k_scale = each(lambda i: jnp.exp(g_kvs[i] - m_news[i]))
    kws = each(lambda i: ks[i] * k_scale[i])
    kv = each(lambda i: _dot_tn(kws[i].astype(BF16), vbs[i]))
    k_sum = each(lambda i: jnp.sum(kws[i], axis=0, keepdims=True))
    dec_c = each(lambda i: jnp.broadcast_to(decs[i], (C_QK_HEAD, C_V_HEAD)))
    c_run = [c_ref[h] for h in range(N_HEADS)]
    n_run = [n_ref[h] for h in range(N_HEADS)]
    c_sts, n_sts = [], []
    for i in range(n):
        c_sts.append(c_run[head(i)].astype(BF16))
        n_sts.append(n_run[head(i)])
        c_run[head(i)] = dec_c[i] * c_run[head(i)] + kv[i]
        n_run[head(i)] = decs[i] * n_run[head(i)] + k_sum[i]
    for h in range(N_HEADS):
        c_ref[h] = c_run[h]
        n_ref[h] = n_run[h]
        m_ref[h] = jnp.broadcast_to(m_run[h], (1, LANES))

    q_c = each(lambda i: _dot(qbs[i], c_sts[i]))
    q_n = each(lambda i: jnp.sum(qs[i] * n_sts[i], axis=-1, keepdims=True))
    inter = each(lambda i: bcs[i] + m_sts[i])
    m_t = each(lambda i: jnp.maximum(inter[i], m_intras[i]))
    s_inter = each(lambda i: jnp.exp(inter[i] - m_t[i]))
    s_intra = each(lambda i: jnp.exp(m_intras[i] - m_t[i]))
    den = each(lambda i: jnp.maximum(jnp.abs(s_inter[i] * q_n[i] + s_intra[i] * dens[i]),
                                     jnp.exp(-m_t[i])))
    hh = each(lambda i: (s_inter[i] * q_c[i] + s_intra[i] * num_intra[i]) / den[i])
    ms = each(lambda i: jnp.mean(hh[i] * hh[i], axis=-1, keepdims=True))
    scale = each(lambda i: lax.rsqrt(ms[i] + 1e-6))
    for i in range(n):
        gate = ng_ref[0:1, vcols(i)] * _sigmoid(op_ref[0, rows(i), vcols(i)])
        o_ref[0, rows(i), vcols(i)] = (hh[i] * scale[i] * gate).astype(BF16)


def _mlstm(q, k, v, op, sc, sct, ng, ts):
    B, S, _ = q.shape
    tok = lambda w: pl.BlockSpec((1, ts, w), lambda b, s: (b, s, 0))
    return pl.pallas_call(
        _mlstm_kernel,
        grid=(B, S // ts),
        in_specs=[tok(C_QK), tok(C_QK), tok(C_V), tok(C_V), tok(LANES),
                  pl.BlockSpec((1, 8, ts), lambda b, s: (b, 0, s)),
                  pl.BlockSpec(ng.shape, lambda b, s: (0, 0))],
        out_specs=tok(C_V),
        out_shape=jax.ShapeDtypeStruct((B, S, C_V), BF16),
        scratch_shapes=[pltpu.VMEM((N_HEADS, C_QK_HEAD, C_V_HEAD), F32),
                        pltpu.VMEM((N_HEADS, 1, C_QK_HEAD), F32),
                        pltpu.VMEM((N_HEADS, 1, LANES), F32)],
        compiler_params=_params("arbitrary", "arbitrary"),
        name="mlstm_scan",
    )(q, k, v, op, sc, sct, ng)


def _pad_lanes(a, offset=0):
    return jnp.pad(a, ((0, 0), (offset, LANES - offset - a.shape[1])))


def kernel(x, p, ab_w_in, ab_conv_qkv, ab_a_log, ab_dt_bias, ab_o_norm_g, ab_dw_w, ab_dw_b,
           ab_cn_g, ab_cn_b, ab_w_out, ffn_w_gate, ffn_w_up, ffn_w_down, c_w_in, c_b_i, c_b_f,
           c_norm_g, c_w_out, moe_w_router, moe_b_router, moe_w_gate, moe_w_up, moe_w_down,
           ln_mix_g, ln_mix_b, ln_ffn_g, ln_ffn_b, ple_w_proj, ple_w_gate):
    B, S, D = x.shape
    T = B * S
    ts_front, ts_scan, tm_big, tf = 512, 512, 1024, 512
    tm_moe = 7 * LANES

    w_in = ab_w_in[0]
    n_main = 4 * A_DIM
    wm = jnp.concatenate([w_in[:, :n_main], w_in[:, n_main + 2 * N_HEADS:]], axis=1).astype(BF16)
    ws = _pad_lanes(w_in[:, n_main:n_main + 2 * N_HEADS]).astype(BF16)
    gp = jnp.concatenate([_pad_lanes(ab_a_log[0][None, :], N_HEADS),
                          _pad_lanes(ab_dt_bias[0][None, :], N_HEADS)], axis=0)
    dwp = jnp.stack([ab_dw_b[0], ab_cn_g[0], ab_cn_b[0]], axis=0)
    q, k, v, z, sc, sct, u = _ab_front(x, wm, ws, ab_conv_qkv[0], gp, ab_dw_w[0], dwp, ts_front)
    o_a = _gdn(q, k, v, z, sc, sct, ab_o_norm_g[0][None, :], ts_scan)
    w_out = ab_w_out[0].astype(BF16)
    lnm = jnp.stack([ln_mix_g[0], ln_mix_b[0]], axis=0)
    lnp = jnp.stack([ln_ffn_g[0], ln_ffn_b[0]], axis=0)
    x2 = _ffn(o_a.reshape(T, A_DIM), u.reshape(T, B_CH), w_out[:A_DIM], w_out[A_DIM:],
              x.reshape(T, D), lnm,
              ffn_w_gate[0].astype(BF16), ffn_w_up[0].astype(BF16),
              ffn_w_down[0].astype(BF16), lnp, p[0].reshape(T, PLE_DIM),
              ple_w_gate[0].astype(BF16), ple_w_proj[0].astype(BF16), tm_big, tf)

    w_in = c_w_in[0]
    n_main = 2 * C_QK + 2 * C_V
    wm = w_in[:, :n_main].astype(BF16)
    ws = _pad_lanes(w_in[:, n_main:]).astype(BF16)
    gb = _pad_lanes(jnp.concatenate([c_b_i[0], c_b_f[0]])[None, :])
    q, k, v, op, sc, sct = _c_front(x2.reshape(B, S, D), wm, ws, gb, ts_front)
    hh = _mlstm(q, k, v, op, sc, sct, c_norm_g[0][None, :], ts_scan)
    lnp = jnp.stack([ln_mix_g[1], ln_mix_b[1]], axis=0)
    x3, x3_tok = _proj_ln([hh.reshape(T, C_V)], [c_w_out[0].astype(BF16)], x2, lnp, tm_big,
                          token_major_copy=True)

    wr = _pad_lanes(moe_w_router[0])
    wr_hi = wr.astype(BF16)
    wr_lo = (wr - wr_hi.astype(F32)).astype(BF16)
    info, info_t, counts = _router(x3, wr_hi, wr_lo, _pad_lanes(moe_b_router[0][None, :]),
                                   tm_big)
    src, dst, tile_expert, n_valid, n_real = _route_plan(info_t, counts, tm_moe)
    y = _moe_experts(tile_expert, n_valid, n_real, src, dst, x3_tok, moe_w_gate[0].astype(BF16),
                     moe_w_up[0].astype(BF16), moe_w_down[0].astype(BF16), 2 * T, tm_moe, tf)
    lnp = jnp.stack([ln_ffn_g[1], ln_ffn_b[1]], axis=0)
    x4 = _moe_combine(x3, info, y, lnp, p[1].reshape(T, PLE_DIM),
                      ple_w_gate[1].astype(BF16), ple_w_proj[1].astype(BF16), tm_big)
    return x4.reshape(B, S, D)
```
